```python
import math
import jax, jax.numpy as jnp
from jax import lax
import numpy as np

D_MODEL = 1024
BATCH = 4
SEQ = 4096
DEPTH = 2

N_MIXERS = 2
N_MLA_LAYERS = (DEPTH + 1) // 2
N_DIFF_LAYERS = DEPTH // 2

MLA_NOPE = 128
MLA_ROPE = 64
MLA_V = 128
MLA_HEADS = D_MODEL // MLA_V
MLA_Q_LORA = 384
MLA_KV_LORA = 256
MLA_ROPE_THETA = 10000.0

DIFF_HEAD_DIM = 64
DIFF_HEADS = D_MODEL // (2 * DIFF_HEAD_DIM)
DIFF_ROT_DIM = DIFF_HEAD_DIM // 4
ROPE_THETA = 500000.0

D_FF = 4 * D_MODEL
Q_BLOCK = 128
DEEPNORM_ALPHA = (2 * DEPTH) ** 0.25
DEEPNORM_BETA = (8 * DEPTH) ** -0.25
LN_EPS = 1e-5
RMS_EPS = 1e-6

kernel_name = "hybrid_mla_diffattn_deepnorm"


def _layernorm(x, g, b):
    xf = x.astype(jnp.float32)
    mu = jnp.mean(xf, axis=-1, keepdims=True)
    var = jnp.mean(jnp.square(xf - mu), axis=-1, keepdims=True)
    y = (xf - mu) * lax.rsqrt(var + LN_EPS) * g.astype(jnp.float32) + b.astype(jnp.float32)
    return y.astype(x.dtype)


def _rmsnorm(x, g):
    xf = x.astype(jnp.float32)
    y = xf * lax.rsqrt(jnp.mean(jnp.square(xf), axis=-1, keepdims=True) + RMS_EPS)
    return (y * g.astype(jnp.float32)).astype(x.dtype)


def _rope(x, positions, theta):
    d = x.shape[-1]
    inv_freq = theta ** (-jnp.arange(0, d, 2, dtype=jnp.float32) / d)
    ang = positions.astype(jnp.float32)[:, :, None] * inv_freq
    cos = jnp.cos(ang)[:, :, None, :]
    sin = jnp.sin(ang)[:, :, None, :]
    xf = x.astype(jnp.float32)
    x1, x2 = xf[..., : d // 2], xf[..., d // 2:]
    out = jnp.concatenate([x1 * cos - x2 * sin, x2 * cos + x1 * sin], axis=-1)
    return out.astype(x.dtype)


def _partial_rope(x, positions, rot_dim, theta):
    return jnp.concatenate([_rope(x[..., :rot_dim], positions, theta), x[..., rot_dim:]], axis=-1)


def _to_blocks(t):
    b, s = t.shape[0], t.shape[1]
    return t.reshape(b, s // Q_BLOCK, Q_BLOCK, *t.shape[2:]).swapaxes(0, 1)


def _from_blocks(t):
    n, b, qb = t.shape[0], t.shape[1], t.shape[2]
    return t.swapaxes(0, 1).reshape(b, n * qb, *t.shape[3:])


def _causal_probs(q_blk, k, q_start, scale):
    s = jnp.einsum('bqhd,bkhd->bhqk', q_blk, k).astype(jnp.float32) * scale
    q_idx = q_start + jnp.arange(q_blk.shape[1])
    k_idx = jnp.arange(k.shape[1])
    s = jnp.where(k_idx[None, :] <= q_idx[:, None], s, -jnp.inf)
    return jax.nn.softmax(s, axis=-1)


def _mla(x, positions, w_dq, q_norm, w_uq, w_dkv, kv_norm, w_ukv, w_o):
    b, s, _ = x.shape
    h = MLA_HEADS
    cq = _rmsnorm(x @ w_dq, q_norm)
    q = (cq @ w_uq).reshape(b, s, h, MLA_NOPE + MLA_ROPE)
    q = jnp.concatenate([q[..., :MLA_NOPE], _rope(q[..., MLA_NOPE:], positions, MLA_ROPE_THETA)], axis=-1)
    ckv = x @ w_dkv
    c_kv = _rmsnorm(ckv[..., :MLA_KV_LORA], kv_norm)
    k_pe = _rope(ckv[..., None, MLA_KV_LORA:], positions, MLA_ROPE_THETA)
    kv = (c_kv @ w_ukv).reshape(b, s, h, MLA_NOPE + MLA_V)
    k = jnp.concatenate([kv[..., :MLA_NOPE], jnp.broadcast_to(k_pe, (b, s, h, MLA_ROPE))], axis=-1)
    v = kv[..., MLA_NOPE:]
    scale = (MLA_NOPE + MLA_ROPE) ** -0.5
    starts = jnp.arange(s // Q_BLOCK) * Q_BLOCK

    def block(args):
        q_blk, start = args
        p = _causal_probs(q_blk, k, start, scale)
        return jnp.einsum('bhqk,bkhd->bqhd', p.astype(v.dtype), v)

    o = _from_blocks(lax.map(block, (_to_blocks(q), starts)))
    return o.reshape(b, s, h * MLA_V) @ w_o


def _diff_attn(x, positions, w_qkv, lq1, lk1, lq2, lk2, subln, w_o, lambda_init):
    b, s, _ = x.shape
    h, dh = DIFF_HEADS, DIFF_HEAD_DIM
    q, k, v = jnp.split(x @ w_qkv, 3, axis=-1)
    q = _partial_rope(q.reshape(b, s, 2 * h, dh), positions, DIFF_ROT_DIM, ROPE_THETA).reshape(b, s, h, 2, dh)
    k = _partial_rope(k.reshape(b, s, 2 * h, dh), positions, DIFF_ROT_DIM, ROPE_THETA).reshape(b, s, h, 2, dh)
    v = v.reshape(b, s, h, 2 * dh)
    k1, k2 = k[..., 0, :], k[..., 1, :]
    lam = (jnp.exp(jnp.sum(lq1.astype(jnp.float32) * lk1.astype(jnp.float32)))
           - jnp.exp(jnp.sum(lq2.astype(jnp.float32) * lk2.astype(jnp.float32)))
           + lambda_init)
    scale = dh ** -0.5
    starts = jnp.arange(s // Q_BLOCK) * Q_BLOCK

    def block(args):
        q_blk, start = args
        p1 = _causal_probs(q_blk[..., 0, :], k1, start, scale)
        p2 = _causal_probs(q_blk[..., 1, :], k2, start, scale)
        a = p1 - lam * p2
        return jnp.einsum('bhqk,bkhd->bqhd', a.astype(v.dtype), v)

    o = _from_blocks(lax.map(block, (_to_blocks(q), starts)))
    o = _rmsnorm(o, subln) * (1.0 - lambda_init)
    return o.reshape(b, s, h * 2 * dh) @ w_o


def _sqrelu_mlp(x, w_up, w_down):
    return jnp.square(jax.nn.relu(x @ w_up)) @ w_down


def setup_inputs(seed: int = 0) -> dict:
    key = jax.random.key(seed)
    ks = jax.random.split(key, 24)
    f32 = jnp.float32
    nrm = lambda k, shape, scale: jax.random.normal(k, shape, f32) * scale
    na, nd, L = N_MLA_LAYERS, N_DIFF_LAYERS, DEPTH
    x = jax.random.normal(ks[0], (BATCH, SEQ, D_MODEL), f32)
    offsets = jax.random.randint(ks[1], (BATCH, 1), 0, 4096, dtype=jnp.int32)
    positions = (jnp.arange(SEQ, dtype=jnp.int32)[None, :] + offsets).astype(jnp.int32)
    return {
        "x": x,
        "positions": positions,
        "ln_mix_g": 1.0 + nrm(ks[2], (L, D_MODEL), 0.02),
        "ln_mix_b": nrm(ks[3], (L, D_MODEL), 0.02),
        "mla_w_dq": nrm(ks[4], (na, D_MODEL, MLA_Q_LORA), D_MODEL ** -0.5),
        "mla_q_norm": 1.0 + nrm(ks[5], (na, MLA_Q_LORA), 0.02),
        "mla_w_uq": nrm(ks[6], (na, MLA_Q_LORA, MLA_HEADS * (MLA_NOPE + MLA_ROPE)), MLA_Q_LORA ** -0.5),
        "mla_w_dkv": nrm(ks[7], (na, D_MODEL, MLA_KV_LORA + MLA_ROPE), D_MODEL ** -0.5),
        "mla_kv_norm": 1.0 + nrm(ks[8], (na, MLA_KV_LORA), 0.02),
        "mla_w_ukv": nrm(ks[9], (na, MLA_KV_LORA, MLA_HEADS * (MLA_NOPE + MLA_V)), MLA_KV_LORA ** -0.5),
        "mla_w_o": nrm(ks[10], (na, MLA_HEADS * MLA_V, D_MODEL), DEEPNORM_BETA * (MLA_HEADS * MLA_V) ** -0.5),
        "diff_w_qkv": nrm(ks[11], (nd, D_MODEL, 3 * DIFF_HEADS * 2 * DIFF_HEAD_DIM), D_MODEL ** -0.5),
        "diff_lambda_q1": nrm(ks[12], (nd, DIFF_HEAD_DIM), 0.1),
        "diff_lambda_k1": nrm(ks[13], (nd, DIFF_HEAD_DIM), 0.1),
        "diff_lambda_q2": nrm(ks[14], (nd, DIFF_HEAD_DIM), 0.1),
        "diff_lambda_k2": nrm(ks[15], (nd, DIFF_HEAD_DIM), 0.1),
        "diff_subln": 1.0 + nrm(ks[16], (nd, 2 * DIFF_HEAD_DIM), 0.02),
        "diff_w_o": nrm(ks[17], (nd, DIFF_HEADS * 2 * DIFF_HEAD_DIM, D_MODEL), DEEPNORM_BETA * D_MODEL ** -0.5),
        "ln_ffn_g": 1.0 + nrm(ks[18], (L, D_MODEL), 0.02),
        "ln_ffn_b": nrm(ks[19], (L, D_MODEL), 0.02),
        "ffn_w_up": nrm(ks[20], (L, D_MODEL, D_FF), D_MODEL ** -0.5),
        "ffn_w_down": nrm(ks[21], (L, D_FF, D_MODEL), DEEPNORM_BETA * D_FF ** -0.5),
    }


def reference(x, positions, ln_mix_g, ln_mix_b, mla_w_dq, mla_q_norm, mla_w_uq, mla_w_dkv,
              mla_kv_norm, mla_w_ukv, mla_w_o, diff_w_qkv, diff_lambda_q1, diff_lambda_k1,
              diff_lambda_q2, diff_lambda_k2, diff_subln, diff_w_o, ln_ffn_g, ln_ffn_b,
              ffn_w_up, ffn_w_down):
    for i in range(DEPTH):
        j = i // N_MIXERS
        if i % N_MIXERS == 0:
            m = _mla(x, positions, mla_w_dq[j], mla_q_norm[j], mla_w_uq[j], mla_w_dkv[j],
                     mla_kv_norm[j], mla_w_ukv[j], mla_w_o[j])
        else:
            lambda_init = 0.8 - 0.6 * math.exp(-0.3 * i)
            m = _diff_attn(x, positions, diff_w_qkv[j], diff_lambda_q1[j], diff_lambda_k1[j],
                           diff_lambda_q2[j], diff_lambda_k2[j], diff_subln[j], diff_w_o[j],
                           lambda_init)
        x = _layernorm(DEEPNORM_ALPHA * x + m, ln_mix_g[i], ln_mix_b[i])
        x = _layernorm(DEEPNORM_ALPHA * x + _sqrelu_mlp(x, ffn_w_up[i], ffn_w_down[i]),
                       ln_ffn_g[i], ln_ffn_b[i])
    return x
```

```python
import functools
import math

import jax
import jax.numpy as jnp
from jax import lax
from jax.experimental import pallas as pl
from jax.experimental.pallas import tpu as pltpu

D_MODEL = 1024
DEPTH = 2

MLA_NOPE = 128
MLA_ROPE = 64
MLA_V = 128
MLA_HEADS = 8
MLA_Q_LORA = 384
MLA_KV_LORA = 256
MLA_ROPE_THETA = 10000.0
MLA_QK_PAD = 256

DIFF_HEAD_DIM = 64
DIFF_HEADS = 8
DIFF_ROT_DIM = 16
DIFF_ROPE_THETA = 500000.0

D_FF = 4 * D_MODEL
DEEPNORM_ALPHA = (2 * DEPTH) ** 0.25
LN_EPS = 1e-5
RMS_EPS = 1e-6

LANES = 128
NEG_BIG = -1e30
VMEM_LIMIT_BYTES = 48 * 1024 * 1024

F32 = jnp.float32
BF16 = jnp.bfloat16


def _cparams(*sem):
    return pltpu.CompilerParams(dimension_semantics=sem, vmem_limit_bytes=VMEM_LIMIT_BYTES)


def _dot(a, b):
    return jnp.dot(a, b, preferred_element_type=F32)


def _rms(x, g):
    return x * lax.rsqrt(jnp.mean(jnp.square(x), axis=-1, keepdims=True) + RMS_EPS) * g


def _layernorm(y, g, b):
    mu = jnp.mean(y, axis=-1, keepdims=True)
    d = y - mu
    var = jnp.mean(jnp.square(d), axis=-1, keepdims=True)
    return d * lax.rsqrt(var + LN_EPS) * g + b


def _rope_table_kernel(pos_ref, invf_ref, cos_ref, sin_ref):
    ang = pos_ref[...].astype(F32) * invf_ref[...]
    cos_ref[...] = jnp.cos(ang)
    sin_ref[...] = jnp.sin(ang)


def _rope_tables(positions, half, theta):
    t = positions.size
    rep = LANES // half
    inv_freq = theta ** (-jnp.arange(0, 2 * half, 2, dtype=F32) / (2 * half))
    pos_dense = jnp.repeat(positions.reshape(t // rep, rep), half, axis=1)
    invf = jnp.tile(inv_freq, rep).reshape(1, LANES)
    rows = t // rep
    cos, sin = pl.pallas_call(
        _rope_table_kernel,
        out_shape=[jax.ShapeDtypeStruct((rows, LANES), F32)] * 2,
        name="rope_table",
    )(pos_dense, invf)
    return cos.reshape(t, half), sin.reshape(t, half)


def _rotate_half_in_lanes(x, half, group, c_tab, s_tab):
    lane = lax.broadcasted_iota(jnp.int32, x.shape, 1)
    from_above = pltpu.roll(x, LANES - half, axis=1)
    from_below = pltpu.roll(x, half, axis=1)
    partner = jnp.where(lane % group < half, from_above, from_below)
    return x * c_tab + partner * s_tab


def _mla_proj_kernel(x_ref, wdq_ref, qn_ref, wuq_ref, wdkv_ref, kvn_ref, wukv_ref,
                     c_ref, s_ref, q_ref, k_ref, v_ref, *, scale):
    xb = x_ref[...].astype(BF16)
    c_tab = c_ref[...]
    s_tab = s_ref[...]
    rope = functools.partial(_rotate_half_in_lanes, half=MLA_ROPE // 2, group=LANES,
                             c_tab=c_tab, s_tab=s_tab)

    cq = _rms(_dot(xb, wdq_ref[...]), qn_ref[...]).astype(BF16)
    q = _dot(cq, wuq_ref[...])
    ckv = _dot(xb, wdkv_ref[...])
    c_kv = _rms(ckv[:, :MLA_KV_LORA], kvn_ref[...]).astype(BF16)
    k_pe = rope(ckv[:, MLA_KV_LORA:]).astype(BF16)
    kv = _dot(c_kv, wukv_ref[...])

    for h in range(MLA_HEADS):
        base = h * MLA_QK_PAD
        q_ref[:, base:base + LANES] = (q[:, base:base + LANES] * scale).astype(BF16)
        q_ref[:, base + LANES:base + 2 * LANES] = (rope(q[:, base + LANES:base + 2 * LANES]) * scale).astype(BF16)
        k_ref[:, base:base + LANES] = kv[:, base:base + LANES].astype(BF16)
        k_ref[:, base + LANES:base + 2 * LANES] = k_pe
        v_ref[:, h * MLA_V:(h + 1) * MLA_V] = kv[:, base + LANES:base + 2 * LANES].astype(BF16)


def _mla_proj(x2d, c_tab, s_tab, w_dq, q_norm, w_uq, w_dkv, kv_norm, w_ukv, tm):
    t = x2d.shape[0]
    h = MLA_HEADS
    pad = MLA_QK_PAD - MLA_NOPE - MLA_ROPE
    wuq = jnp.pad(w_uq.reshape(MLA_Q_LORA, h, MLA_NOPE + MLA_ROPE), ((0, 0), (0, 0), (0, pad)))
    wuq = wuq.reshape(MLA_Q_LORA, h * MLA_QK_PAD).astype(BF16)
    wdkv = jnp.pad(w_dkv, ((0, 0), (0, pad))).astype(BF16)
    row = lambda i: (i, 0)
    full = lambda i: (0, 0)
    return pl.pallas_call(
        functools.partial(_mla_proj_kernel, scale=(MLA_NOPE + MLA_ROPE) ** -0.5),
        grid=(t // tm,),
        in_specs=[
            pl.BlockSpec((tm, D_MODEL), row),
            pl.BlockSpec((D_MODEL, MLA_Q_LORA), full),
            pl.BlockSpec((1, MLA_Q_LORA), full),
            pl.BlockSpec((MLA_Q_LORA, h * MLA_QK_PAD), full),
            pl.BlockSpec((D_MODEL, MLA_KV_LORA + LANES), full),
            pl.BlockSpec((1, MLA_KV_LORA), full),
            pl.BlockSpec((MLA_KV_LORA, h * (MLA_NOPE + MLA_V)), full),
            pl.BlockSpec((tm, LANES), row),
            pl.BlockSpec((tm, LANES), row),
        ],
        out_specs=[
            pl.BlockSpec((tm, h * MLA_QK_PAD), row),
            pl.BlockSpec((tm, h * MLA_QK_PAD), row),
            pl.BlockSpec((tm, h * MLA_V), row),
        ],
        out_shape=[
            jax.ShapeDtypeStruct((t, h * MLA_QK_PAD), BF16),
            jax.ShapeDtypeStruct((t, h * MLA_QK_PAD), BF16),
            jax.ShapeDtypeStruct((t, h * MLA_V), BF16),
        ],
        compiler_params=_cparams("parallel"),
        name="mla_proj",
    )(x2d, w_dq.astype(BF16), q_norm.reshape(1, -1), wuq, wdkv, kv_norm.reshape(1, -1),
      w_ukv.astype(BF16), c_tab, s_tab)


def _diff_proj_kernel(x_ref, w_ref, c_ref, s_ref, q_ref, k_ref, v_ref, *, scale):
    xb = x_ref[...].astype(BF16)
    c_tab = c_ref[...]
    s_tab = s_ref[...]
    rope = functools.partial(_rotate_half_in_lanes, half=DIFF_ROT_DIM // 2, group=DIFF_HEAD_DIM,
                             c_tab=c_tab, s_tab=s_tab)
    qkv = _dot(xb, w_ref[...])
    for j in range(D_MODEL // LANES):
        lo, hi = j * LANES, (j + 1) * LANES
        q_ref[:, lo:hi] = (rope(qkv[:, lo:hi]) * scale).astype(BF16)
        k_ref[:, lo:hi] = rope(qkv[:, D_MODEL + lo:D_MODEL + hi]).astype(BF16)
    v_ref[...] = qkv[:, 2 * D_MODEL:].astype(BF16)


def _diff_proj(x2d, c_tab, s_tab, w_qkv, tm):
    t = x2d.shape[0]
    row = lambda i: (i, 0)
    full = lambda i: (0, 0)
    return pl.pallas_call(
        functools.partial(_diff_proj_kernel, scale=DIFF_HEAD_DIM ** -0.5),
        grid=(t // tm,),
        in_specs=[
            pl.BlockSpec((tm, D_MODEL), row),
            pl.BlockSpec((D_MODEL, 3 * D_MODEL), full),
            pl.BlockSpec((tm, LANES), row),
            pl.BlockSpec((tm, LANES), row),
        ],
        out_specs=[pl.BlockSpec((tm, D_MODEL), row)] * 3,
        out_shape=[jax.ShapeDtypeStruct((t, D_MODEL), BF16)] * 3,
        compiler_params=_cparams("parallel"),
        name="diff_proj",
    )(x2d, w_qkv.astype(BF16), c_tab, s_tab)


def _flash_loop(q, k_ref, v_ref, m_ref, l_ref, acc_ref, q_start, tq, tk):
    rows = q.shape[0]
    m_ref[...] = jnp.full(m_ref.shape, NEG_BIG, F32)
    l_ref[...] = jnp.zeros(l_ref.shape, F32)
    acc_ref[...] = jnp.zeros(acc_ref.shape, F32)

    def step(j, masked):
        start = pl.multiple_of(j * tk, tk)
        k = k_ref[pl.ds(start, tk), :]
        v = v_ref[pl.ds(start, tk), :]
        s = lax.dot_general(q, k, (((1,), (1,)), ((), ())), preferred_element_type=F32)
        if masked:
            q_pos = q_start + lax.broadcasted_iota(jnp.int32, (rows, tk), 0) % tq
            k_pos = start + lax.broadcasted_iota(jnp.int32, (rows, tk), 1)
            s = jnp.where(k_pos <= q_pos, s, NEG_BIG)
        m_prev = m_ref[...]
        m_new = jnp.maximum(m_prev, jnp.max(s, axis=-1, keepdims=True))
        alpha = jnp.exp(m_prev - m_new)
        p = jnp.exp(s - m_new)
        l_ref[...] = alpha * l_ref[...] + jnp.sum(p, axis=-1, keepdims=True)
        acc_ref[...] = alpha * acc_ref[...] + _dot(p.astype(BF16), v)
        m_ref[...] = m_new

    n_full = q_start // tk
    n_all = (q_start + tq + tk - 1) // tk

    def full_body(j, carry):
        step(j, masked=False)
        return carry

    def diag_body(j, carry):
        step(j, masked=True)
        return carry

    lax.fori_loop(0, n_full, full_body, 0)
    lax.fori_loop(n_full, n_all, diag_body, 0)


def _mla_flash_kernel(q_ref, k_ref, v_ref, o_ref, m_ref, l_ref, acc_ref, *, tq, tk):
    q_start = pl.program_id(2) * tq
    _flash_loop(q_ref[...], k_ref, v_ref, m_ref, l_ref, acc_ref, q_start, tq, tk)
    o_ref[...] = (acc_ref[...] / l_ref[...]).astype(o_ref.dtype)


def _diff_flash_kernel(q_ref, k_ref, v_ref, lq1_ref, lk1_ref, lq2_ref, lk2_ref, subln_ref,
                       o_ref, m_ref, l_ref, acc_ref, *, tq, tk, lambda_init):
    q_start = pl.program_id(2) * tq
    q = q_ref[...]
    lane = lax.broadcasted_iota(jnp.int32, q.shape, 1)
    zero = jnp.zeros_like(q)
    q_stack = jnp.concatenate([jnp.where(lane < DIFF_HEAD_DIM, q, zero),
                               jnp.where(lane < DIFF_HEAD_DIM, zero, q)], axis=0)
    _flash_loop(q_stack, k_ref, v_ref, m_ref, l_ref, acc_ref, q_start, tq, tk)
    lam = (jnp.exp(jnp.sum(lq1_ref[...] * lk1_ref[...]))
           - jnp.exp(jnp.sum(lq2_ref[...] * lk2_ref[...])) + lambda_init)
    o1 = acc_ref[:tq, :] / l_ref[:tq, :]
    o2 = acc_ref[tq:, :] / l_ref[tq:, :]
    o = _rms(o1 - lam * o2, subln_ref[...]) * (1.0 - lambda_init)
    o_ref[...] = o.astype(o_ref.dtype)


def _flash_specs(seq, n_q, tq, dk, dv):
    q_spec = pl.BlockSpec((tq, dk), lambda b, h, i: (b * n_q + i, h))
    k_spec = pl.BlockSpec((seq, dk), lambda b, h, i: (b, h))
    v_spec = pl.BlockSpec((seq, dv), lambda b, h, i: (b, h))
    o_spec = pl.BlockSpec((tq, dv), lambda b, h, i: (b * n_q + i, h))
    return q_spec, k_spec, v_spec, o_spec


def _mla_flash(q, k, v, batch, seq, tq, tk):
    n_q = seq // tq
    q_spec, k_spec, v_spec, o_spec = _flash_specs(seq, n_q, tq, MLA_QK_PAD, MLA_V)
    return pl.pallas_call(
        functools.partial(_mla_flash_kernel, tq=tq, tk=tk),
        grid=(batch, MLA_HEADS, n_q),
        in_specs=[q_spec, k_spec, v_spec],
        out_specs=o_spec,
        out_shape=jax.ShapeDtypeStruct((batch * seq, MLA_HEADS * MLA_V), BF16),
        scratch_shapes=[pltpu.VMEM((tq, 1), F32), pltpu.VMEM((tq, 1), F32),
                        pltpu.VMEM((tq, MLA_V), F32)],
        compiler_params=_cparams("parallel", "parallel", "arbitrary"),
        name="mla_flash",
    )(q, k, v)


def _diff_flash(q, k, v, lq1, lk1, lq2, lk2, subln, lambda_init, batch, seq, tq, tk):
    n_q = seq // tq
    dv = 2 * DIFF_HEAD_DIM
    q_spec, k_spec, v_spec, o_spec = _flash_specs(seq, n_q, tq, dv, dv)
    vec = lambda n: pl.BlockSpec((1, n), lambda b, h, i: (0, 0))
    return pl.pallas_call(
        functools.partial(_diff_flash_kernel, tq=tq, tk=tk, lambda_init=lambda_init),
        grid=(batch, DIFF_HEADS, n_q),
        in_specs=[q_spec, k_spec, v_spec, vec(DIFF_HEAD_DIM), vec(DIFF_HEAD_DIM),
                  vec(DIFF_HEAD_DIM), vec(DIFF_HEAD_DIM), vec(dv)],
        out_specs=o_spec,
        out_shape=jax.ShapeDtypeStruct((batch * seq, DIFF_HEADS * dv), BF16),
        scratch_shapes=[pltpu.VMEM((2 * tq, 1), F32), pltpu.VMEM((2 * tq, 1), F32),
                        pltpu.VMEM((2 * tq, dv), F32)],
        compiler_params=_cparams("parallel", "parallel", "arbitrary"),
        name="diff_flash",
    )(q, k, v, lq1.reshape(1, -1), lk1.reshape(1, -1), lq2.reshape(1, -1), lk2.reshape(1, -1),
      subln.reshape(1, -1))


def _out_proj_ln_kernel(o_ref, w_ref, x_ref, g_ref, b_ref, y_ref):
    y = DEEPNORM_ALPHA * x_ref[...] + _dot(o_ref[...], w_ref[...])
    y_ref[...] = _layernorm(y, g_ref[...], b_ref[...])


def _out_proj_ln(o, w_o, x2d, g, b, tm):
    t = x2d.shape[0]
    row = lambda i: (i, 0)
    full = lambda i: (0, 0)
    return pl.pallas_call(
        _out_proj_ln_kernel,
        grid=(t // tm,),
        in_specs=[
            pl.BlockSpec((tm, D_MODEL), row),
            pl.BlockSpec((D_MODEL, D_MODEL), full),
            pl.BlockSpec((tm, D_MODEL), row),
            pl.BlockSpec((1, D_MODEL), full),
            pl.BlockSpec((1, D_MODEL), full),
        ],
        out_specs=pl.BlockSpec((tm, D_MODEL), row),
        out_shape=jax.ShapeDtypeStruct((t, D_MODEL), F32),
        compiler_params=_cparams("parallel"),
        name="out_proj_ln",
    )(o, w_o.astype(BF16), x2d, g.reshape(1, -1), b.reshape(1, -1))


def _mlp_ln_kernel(x_ref, wu_ref, wd_ref, g_ref, b_ref, y_ref, acc_ref):
    j = pl.program_id(1)

    @pl.when(j == 0)
    def _():
        acc_ref[...] = jnp.zeros(acc_ref.shape, F32)

    hidden = jnp.square(jnp.maximum(_dot(x_ref[...].astype(BF16), wu_ref[...]), 0.0))
    acc_ref[...] += _dot(hidden.astype(BF16), wd_ref[...])

    @pl.when(j == pl.num_programs(1) - 1)
    def _():
        y = DEEPNORM_ALPHA * x_ref[...] + acc_ref[...]
        y_ref[...] = _layernorm(y, g_ref[...], b_ref[...])


def _mlp_ln(x2d, w_up, w_down, g, b, tm, tf):
    t = x2d.shape[0]
    return pl.pallas_call(
        _mlp_ln_kernel,
        grid=(t // tm, D_FF // tf),
        in_specs=[
            pl.BlockSpec((tm, D_MODEL), lambda i, j: (i, 0)),
            pl.BlockSpec((D_MODEL, tf), lambda i, j: (0, j)),
            pl.BlockSpec((tf, D_MODEL), lambda i, j: (j, 0)),
            pl.BlockSpec((1, D_MODEL), lambda i, j: (0, 0)),
            pl.BlockSpec((1, D_MODEL), lambda i, j: (0, 0)),
        ],
        out_specs=pl.BlockSpec((tm, D_MODEL), lambda i, j: (i, 0)),
        out_shape=jax.ShapeDtypeStruct((t, D_MODEL), F32),
        scratch_shapes=[pltpu.VMEM((tm, D_MODEL), F32)],
        compiler_params=_cparams("parallel", "arbitrary"),
        name="mlp_ln",
    )(x2d, w_up.astype(BF16), w_down.astype(BF16), g.reshape(1, -1), b.reshape(1, -1))


def _lane_tables(cos, sin, group):
    t, half = cos.shape
    rest = group - 2 * half
    c = jnp.concatenate([cos, cos, jnp.ones((t, rest), F32)], axis=1)
    s = jnp.concatenate([-sin, sin, jnp.zeros((t, rest), F32)], axis=1)
    reps = LANES // group
    return jnp.tile(c, (1, reps)), jnp.tile(s, (1, reps))


def kernel(x, positions, ln_mix_g, ln_mix_b, mla_w_dq, mla_q_norm, mla_w_uq, mla_w_dkv, mla_kv_norm, mla_w_ukv, mla_w_o, diff_w_qkv, diff_lambda_q1, diff_lambda_k1, diff_lambda_q2, diff_lambda_k2, diff_subln, diff_w_o, ln_ffn_g, ln_ffn_b, ffn_w_up, ffn_w_down):
    batch, seq, _ = x.shape
    x2d = x.reshape(batch * seq, D_MODEL)
    pos = positions.reshape(-1)

    mla_cos, mla_sin = _rope_tables(pos, MLA_ROPE // 2, MLA_ROPE_THETA)
    mla_c, mla_s = _lane_tables(mla_cos, mla_sin, LANES)
    diff_cos, diff_sin = _rope_tables(pos, DIFF_ROT_DIM // 2, DIFF_ROPE_THETA)
    diff_c, diff_s = _lane_tables(diff_cos, diff_sin, DIFF_HEAD_DIM)

    tm = 512
    q, k, v = _mla_proj(x2d, mla_c, mla_s, mla_w_dq[0], mla_q_norm[0], mla_w_uq[0],
                        mla_w_dkv[0], mla_kv_norm[0], mla_w_ukv[0], tm)
    o = _mla_flash(q, k, v, batch, seq, tq=512, tk=512)
    x2d = _out_proj_ln(o, mla_w_o[0], x2d, ln_mix_g[0], ln_mix_b[0], tm)
    x2d = _mlp_ln(x2d, ffn_w_up[0], ffn_w_down[0], ln_ffn_g[0], ln_ffn_b[0], tm, 512)

    lambda_init = 0.8 - 0.6 * math.exp(-0.3 * 1)
    q, k, v = _diff_proj(x2d, diff_c, diff_s, diff_w_qkv[0], tm)
    o = _diff_flash(q, k, v, diff_lambda_q1[0], diff_lambda_k1[0], diff_lambda_q2[0],
                    diff_lambda_k2[0], diff_subln[0], lambda_init, batch, seq, tq=256, tk=512)
    x2d = _out_proj_ln(o, diff_w_o[0], x2d, ln_mix_g[1], ln_mix_b[1], tm)
    x2d = _mlp_ln(x2d, ffn_w_up[1], ffn_w_down[1], ln_ffn_g[1], ln_ffn_b[1], tm, 512)
    return x2d.reshape(batch, seq, D_MODEL)
```

```python
import functools
import math

import jax
import jax.numpy as jnp
from jax import lax
from jax.experimental import pallas as pl
from jax.experimental.pallas import tpu as pltpu

D_MODEL = 1024
DEPTH = 2

MLA_NOPE = 128
MLA_ROPE = 64
MLA_V = 128
MLA_HEADS = 8
MLA_Q_LORA = 384
MLA_KV_LORA = 256
MLA_ROPE_THETA = 10000.0
MLA_QK_PAD = 256

DIFF_HEAD_DIM = 64
DIFF_HEADS = 8
DIFF_ROT_DIM = 16
DIFF_ROPE_THETA = 500000.0

D_FF = 4 * D_MODEL
DEEPNORM_ALPHA = (2 * DEPTH) ** 0.25
LN_EPS = 1e-5
RMS_EPS = 1e-6

LANES = 128
NEG_BIG = -1e30
LOG2_E = math.log2(math.e)
VMEM_LIMIT_BYTES = 48 * 1024 * 1024

F32 = jnp.float32
BF16 = jnp.bfloat16


def _cparams(*sem):
    return pltpu.CompilerParams(dimension_semantics=sem, vmem_limit_bytes=VMEM_LIMIT_BYTES)


def _dot(a, b):
    return jnp.dot(a, b, preferred_element_type=F32)


def _rms(x, g):
    return x * lax.rsqrt(jnp.mean(jnp.square(x), axis=-1, keepdims=True) + RMS_EPS) * g


def _layernorm(y, g, b):
    mu = jnp.mean(y, axis=-1, keepdims=True)
    d = y - mu
    var = jnp.mean(jnp.square(d), axis=-1, keepdims=True)
    return d * lax.rsqrt(var + LN_EPS) * g + b


def _rope_table_kernel(pos_ref, invf_ref, cos_ref, sin_ref):
    ang = pos_ref[...].astype(F32) * invf_ref[...]
    cos_ref[...] = jnp.cos(ang)
    sin_ref[...] = jnp.sin(ang)


def _rope_tables(positions, half, theta):
    t = positions.size
    rep = LANES // half
    inv_freq = theta ** (-jnp.arange(0, 2 * half, 2, dtype=F32) / (2 * half))
    pos_dense = jnp.repeat(positions.reshape(t // rep, rep), half, axis=1)
    invf = jnp.tile(inv_freq, rep).reshape(1, LANES)
    rows = t // rep
    cos, sin = pl.pallas_call(
        _rope_table_kernel,
        out_shape=[jax.ShapeDtypeStruct((rows, LANES), F32)] * 2,
        name="rope_table",
    )(pos_dense, invf)
    return cos.reshape(t, half), sin.reshape(t, half)


def _rotate_half_in_lanes(x, half, group, c_tab, s_tab):
    lane = lax.broadcasted_iota(jnp.int32, x.shape, 1)
    from_above = pltpu.roll(x, LANES - half, axis=1)
    from_below = pltpu.roll(x, half, axis=1)
    partner = jnp.where(lane % group < half, from_above, from_below)
    return x * c_tab + partner * s_tab


def _mla_proj_kernel(x_ref, wdq_ref, qn_ref, wuq_ref, wdkv_ref, kvn_ref, wukv_ref,
                     c_ref, s_ref, q_ref, k_ref, v_ref, *, scale):
    xb = x_ref[...].astype(BF16)
    c_tab = c_ref[...]
    s_tab = s_ref[...]
    rope = functools.partial(_rotate_half_in_lanes, half=MLA_ROPE // 2, group=LANES,
                             c_tab=c_tab, s_tab=s_tab)

    cq = _rms(_dot(xb, wdq_ref[...]), qn_ref[...]).astype(BF16)
    q = _dot(cq, wuq_ref[...])
    ckv = _dot(xb, wdkv_ref[...])
    c_kv = _rms(ckv[:, :MLA_KV_LORA], kvn_ref[...]).astype(BF16)
    k_pe = rope(ckv[:, MLA_KV_LORA:]).astype(BF16)
    kv = _dot(c_kv, wukv_ref[...])

    for h in range(MLA_HEADS):
        base = h * MLA_QK_PAD
        q_ref[:, base:base + LANES] = (q[:, base:base + LANES] * scale).astype(BF16)
        q_ref[:, base + LANES:base + 2 * LANES] = (rope(q[:, base + LANES:base + 2 * LANES]) * scale).astype(BF16)
        k_ref[:, base:base + LANES] = kv[:, base:base + LANES].astype(BF16)
        k_ref[:, base + LANES:base + 2 * LANES] = k_pe
        v_ref[:, h * MLA_V:(h + 1) * MLA_V] = kv[:, base + LANES:base + 2 * LANES].astype(BF16)


def _mla_proj(x2d, c_tab, s_tab, w_dq, q_norm, w_uq, w_dkv, kv_norm, w_ukv, tm):
    t = x2d.shape[0]
    h = MLA_HEADS
    pad = MLA_QK_PAD - MLA_NOPE - MLA_ROPE
    wuq = jnp.pad(w_uq.reshape(MLA_Q_LORA, h, MLA_NOPE + MLA_ROPE), ((0, 0), (0, 0), (0, pad)))
    wuq = wuq.reshape(MLA_Q_LORA, h * MLA_QK_PAD).astype(BF16)
    wdkv = jnp.pad(w_dkv, ((0, 0), (0, pad))).astype(BF16)
    row = lambda i: (i, 0)
    full = lambda i: (0, 0)
    return pl.pallas_call(
        functools.partial(_mla_proj_kernel, scale=LOG2_E * (MLA_NOPE + MLA_ROPE) ** -0.5),
        grid=(t // tm,),
        in_specs=[
            pl.BlockSpec((tm, D_MODEL), row),
            pl.BlockSpec((D_MODEL, MLA_Q_LORA), full),
            pl.BlockSpec((1, MLA_Q_LORA), full),
            pl.BlockSpec((MLA_Q_LORA, h * MLA_QK_PAD), full),
            pl.BlockSpec((D_MODEL, MLA_KV_LORA + LANES), full),
            pl.BlockSpec((1, MLA_KV_LORA), full),
            pl.BlockSpec((MLA_KV_LORA, h * (MLA_NOPE + MLA_V)), full),
            pl.BlockSpec((tm, LANES), row),
            pl.BlockSpec((tm, LANES), row),
        ],
        out_specs=[
            pl.BlockSpec((tm, h * MLA_QK_PAD), row),
            pl.BlockSpec((tm, h * MLA_QK_PAD), row),
            pl.BlockSpec((tm, h * MLA_V), row),
        ],
        out_shape=[
            jax.ShapeDtypeStruct((t, h * MLA_QK_PAD), BF16),
            jax.ShapeDtypeStruct((t, h * MLA_QK_PAD), BF16),
            jax.ShapeDtypeStruct((t, h * MLA_V), BF16),
        ],
        compiler_params=_cparams("parallel"),
        name="mla_proj",
    )(x2d, w_dq.astype(BF16), q_norm.reshape(1, -1), wuq, wdkv, kv_norm.reshape(1, -1),
      w_ukv.astype(BF16), c_tab, s_tab)


def _diff_proj_kernel(x_ref, w_ref, c_ref, s_ref, q_ref, k_ref, v_ref, *, scale):
    xb = x_ref[...].astype(BF16)
    c_tab = c_ref[...]
    s_tab = s_ref[...]
    rope = functools.partial(_rotate_half_in_lanes, half=DIFF_ROT_DIM // 2, group=DIFF_HEAD_DIM,
                             c_tab=c_tab, s_tab=s_tab)
    qkv = _dot(xb, w_ref[...])
    for j in range(D_MODEL // LANES):
        lo, hi = j * LANES, (j + 1) * LANES
        q_ref[:, lo:hi] = (rope(qkv[:, lo:hi]) * scale).astype(BF16)
        k_ref[:, lo:hi] = rope(qkv[:, D_MODEL + lo:D_MODEL + hi]).astype(BF16)
    v_ref[...] = qkv[:, 2 * D_MODEL:].astype(BF16)


def _diff_proj(x2d, c_tab, s_tab, w_qkv, tm):
    t = x2d.shape[0]
    row = lambda i: (i, 0)
    full = lambda i: (0, 0)
    return pl.pallas_call(
        functools.partial(_diff_proj_kernel, scale=LOG2_E * DIFF_HEAD_DIM ** -0.5),
        grid=(t // tm,),
        in_specs=[
            pl.BlockSpec((tm, D_MODEL), row),
            pl.BlockSpec((D_MODEL, 3 * D_MODEL), full),
            pl.BlockSpec((tm, LANES), row),
            pl.BlockSpec((tm, LANES), row),
        ],
        out_specs=[pl.BlockSpec((tm, D_MODEL), row)] * 3,
        out_shape=[jax.ShapeDtypeStruct((t, D_MODEL), BF16)] * 3,
        compiler_params=_cparams("parallel"),
        name="diff_proj",
    )(x2d, w_qkv.astype(BF16), c_tab, s_tab)


def _flash_loop(q, k_ref, v_ref, m_ref, l_ref, acc_ref, q_start, tq, tk):
    rows = q.shape[0]
    m_ref[...] = jnp.full(m_ref.shape, NEG_BIG, F32)
    l_ref[...] = jnp.zeros(l_ref.shape, F32)
    acc_ref[...] = jnp.zeros(acc_ref.shape, F32)

    def step(j, masked):
        start = pl.multiple_of(j * tk, tk)
        k = k_ref[pl.ds(start, tk), :]
        v = v_ref[pl.ds(start, tk), :]
        s = lax.dot_general(q, k, (((1,), (1,)), ((), ())), preferred_element_type=F32)
        if masked:
            q_pos = q_start + lax.broadcasted_iota(jnp.int32, (rows, tk), 0) % tq
            k_pos = start + lax.broadcasted_iota(jnp.int32, (rows, tk), 1)
            s = jnp.where(k_pos <= q_pos, s, NEG_BIG)
        m_prev = m_ref[...]
        m_new = jnp.maximum(m_prev, jnp.max(s, axis=-1, keepdims=True))
        alpha = jnp.exp2(m_prev - m_new)
        p = [jnp.exp2(s[:, c:c + LANES] - m_new) for c in range(0, tk, LANES)]
        l_ref[...] = alpha * l_ref[...] + functools.reduce(jnp.add, p)
        pv = _dot(jnp.concatenate(p, axis=1).astype(BF16), v)
        acc_ref[...] = alpha * acc_ref[...] + pv
        m_ref[...] = m_new

    n_full = q_start // tk
    n_all = (q_start + tq + tk - 1) // tk

    def full_body(j, carry):
        step(j, masked=False)
        return carry

    def diag_body(j, carry):
        step(j, masked=True)
        return carry

    lax.fori_loop(0, n_full, full_body, 0)
    lax.fori_loop(n_full, n_all, diag_body, 0)


def _row_sum(l):
    return jnp.sum(l, axis=-1, keepdims=True)


def _mla_flash_kernel(q_ref, k_ref, v_ref, o_ref, m_ref, l_ref, acc_ref, *, tq, tk):
    q_start = pl.program_id(2) * tq
    _flash_loop(q_ref[...], k_ref, v_ref, m_ref, l_ref, acc_ref, q_start, tq, tk)
    o_ref[...] = (acc_ref[...] / _row_sum(l_ref[...])).astype(o_ref.dtype)


def _diff_flash_kernel(q_ref, k_ref, v_ref, lq1_ref, lk1_ref, lq2_ref, lk2_ref, subln_ref,
                       o_ref, m_ref, l_ref, acc_ref, *, tq, tk, lambda_init):
    q_start = pl.program_id(2) * tq
    q = q_ref[...]
    lane = lax.broadcasted_iota(jnp.int32, q.shape, 1)
    zero = jnp.zeros_like(q)
    q_stack = jnp.concatenate([jnp.where(lane < DIFF_HEAD_DIM, q, zero),
                               jnp.where(lane < DIFF_HEAD_DIM, zero, q)], axis=0)
    _flash_loop(q_stack, k_ref, v_ref, m_ref, l_ref, acc_ref, q_start, tq, tk)
    lam = (jnp.exp(jnp.sum(lq1_ref[...] * lk1_ref[...]))
           - jnp.exp(jnp.sum(lq2_ref[...] * lk2_ref[...])) + lambda_init)
    o1 = acc_ref[:tq, :] / _row_sum(l_ref[:tq, :])
    o2 = acc_ref[tq:, :] / _row_sum(l_ref[tq:, :])
    o = _rms(o1 - lam * o2, subln_ref[...]) * (1.0 - lambda_init)
    o_ref[...] = o.astype(o_ref.dtype)


def _flash_specs(seq, n_q, tq, dk, dv):
    q_spec = pl.BlockSpec((tq, dk), lambda b, h, i: (b * n_q + i, h))
    k_spec = pl.BlockSpec((seq, dk), lambda b, h, i: (b, h))
    v_spec = pl.BlockSpec((seq, dv), lambda b, h, i: (b, h))
    o_spec = pl.BlockSpec((tq, dv), lambda b, h, i: (b * n_q + i, h))
    return q_spec, k_spec, v_spec, o_spec


def _mla_flash(q, k, v, batch, seq, tq, tk):
    n_q = seq // tq
    q_spec, k_spec, v_spec, o_spec = _flash_specs(seq, n_q, tq, MLA_QK_PAD, MLA_V)
    return pl.pallas_call(
        functools.partial(_mla_flash_kernel, tq=tq, tk=tk),
        grid=(batch, MLA_HEADS, n_q),
        in_specs=[q_spec, k_spec, v_spec],
        out_specs=o_spec,
        out_shape=jax.ShapeDtypeStruct((batch * seq, MLA_HEADS * MLA_V), BF16),
        scratch_shapes=[pltpu.VMEM((tq, LANES), F32), pltpu.VMEM((tq, LANES), F32),
                        pltpu.VMEM((tq, MLA_V), F32)],
        compiler_params=_cparams("parallel", "parallel", "arbitrary"),
        name="mla_flash",
    )(q, k, v)


def _diff_flash(q, k, v, lq1, lk1, lq2, lk2, subln, lambda_init, batch, seq, tq, tk):
    n_q = seq // tq
    dv = 2 * DIFF_HEAD_DIM
    q_spec, k_spec, v_spec, o_spec = _flash_specs(seq, n_q, tq, dv, dv)
    vec = lambda n: pl.BlockSpec((1, n), lambda b, h, i: (0, 0))
    return pl.pallas_call(
        functools.partial(_diff_flash_kernel, tq=tq, tk=tk, lambda_init=lambda_init),
        grid=(batch, DIFF_HEADS, n_q),
        in_specs=[q_spec, k_spec, v_spec, vec(DIFF_HEAD_DIM), vec(DIFF_HEAD_DIM),
                  vec(DIFF_HEAD_DIM), vec(DIFF_HEAD_DIM), vec(dv)],
        out_specs=o_spec,
        out_shape=jax.ShapeDtypeStruct((batch * seq, DIFF_HEADS * dv), BF16),
        scratch_shapes=[pltpu.VMEM((2 * tq, LANES), F32), pltpu.VMEM((2 * tq, LANES), F32),
                        pltpu.VMEM((2 * tq, dv), F32)],
        compiler_params=_cparams("parallel", "parallel", "arbitrary"),
        name="diff_flash",
    )(q, k, v, lq1.reshape(1, -1), lk1.reshape(1, -1), lq2.reshape(1, -1), lk2.reshape(1, -1),
      subln.reshape(1, -1))


def _out_proj_ln_kernel(o_ref, w_ref, x_ref, g_ref, b_ref, y_ref):
    y = DEEPNORM_ALPHA * x_ref[...] + _dot(o_ref[...], w_ref[...])
    y_ref[...] = _layernorm(y, g_ref[...], b_ref[...])


def _out_proj_ln(o, w_o, x2d, g, b, tm):
    t = x2d.shape[0]
    row = lambda i: (i, 0)
    full = lambda i: (0, 0)
    return pl.pallas_call(
        _out_proj_ln_kernel,
        grid=(t // tm,),
        in_specs=[
            pl.BlockSpec((tm, D_MODEL), row),
            pl.BlockSpec((D_MODEL, D_MODEL), full),
            pl.BlockSpec((tm, D_MODEL), row),
            pl.BlockSpec((1, D_MODEL), full),
            pl.BlockSpec((1, D_MODEL), full),
        ],
        out_specs=pl.BlockSpec((tm, D_MODEL), row),
        out_shape=jax.ShapeDtypeStruct((t, D_MODEL), F32),
        compiler_params=_cparams("parallel"),
        name="out_proj_ln",
    )(o, w_o.astype(BF16), x2d, g.reshape(1, -1), b.reshape(1, -1))


def _mlp_ln_kernel(x_ref, wu_ref, wd_ref, g_ref, b_ref, y_ref, acc_ref):
    j = pl.program_id(1)

    @pl.when(j == 0)
    def _():
        acc_ref[...] = jnp.zeros(acc_ref.shape, F32)

    hidden = jnp.square(jnp.maximum(_dot(x_ref[...].astype(BF16), wu_ref[...]), 0.0))
    acc_ref[...] += _dot(hidden.astype(BF16), wd_ref[...])

    @pl.when(j == pl.num_programs(1) - 1)
    def _():
        y = DEEPNORM_ALPHA * x_ref[...] + acc_ref[...]
        y_ref[...] = _layernorm(y, g_ref[...], b_ref[...])


def _mlp_ln(x2d, w_up, w_down, g, b, tm, tf):
    t = x2d.shape[0]
    return pl.pallas_call(
        _mlp_ln_kernel,
        grid=(t // tm, D_FF // tf),
        in_specs=[
            pl.BlockSpec((tm, D_MODEL), lambda i, j: (i, 0)),
            pl.BlockSpec((D_MODEL, tf), lambda i, j: (0, j)),
            pl.BlockSpec((tf, D_MODEL), lambda i, j: (j, 0)),
            pl.BlockSpec((1, D_MODEL), lambda i, j: (0, 0)),
            pl.BlockSpec((1, D_MODEL), lambda i, j: (0, 0)),
        ],
        out_specs=pl.BlockSpec((tm, D_MODEL), lambda i, j: (i, 0)),
        out_shape=jax.ShapeDtypeStruct((t, D_MODEL), F32),
        scratch_shapes=[pltpu.VMEM((tm, D_MODEL), F32)],
        compiler_params=_cparams("parallel", "arbitrary"),
        name="mlp_ln",
    )(x2d, w_up.astype(BF16), w_down.astype(BF16), g.reshape(1, -1), b.reshape(1, -1))


def _lane_tables(cos, sin, group):
    t, half = cos.shape
    rest = group - 2 * half
    c = jnp.concatenate([cos, cos, jnp.ones((t, rest), F32)], axis=1)
    s = jnp.concatenate([-sin, sin, jnp.zeros((t, rest), F32)], axis=1)
    reps = LANES // group
    return jnp.tile(c, (1, reps)), jnp.tile(s, (1, reps))


def kernel(x, positions, ln_mix_g, ln_mix_b, mla_w_dq, mla_q_norm, mla_w_uq, mla_w_dkv, mla_kv_norm, mla_w_ukv, mla_w_o, diff_w_qkv, diff_lambda_q1, diff_lambda_k1, diff_lambda_q2, diff_lambda_k2, diff_subln, diff_w_o, ln_ffn_g, ln_ffn_b, ffn_w_up, ffn_w_down):
    batch, seq, _ = x.shape
    x2d = x.reshape(batch * seq, D_MODEL)
    pos = positions.reshape(-1)

    mla_cos, mla_sin = _rope_tables(pos, MLA_ROPE // 2, MLA_ROPE_THETA)
    mla_c, mla_s = _lane_tables(mla_cos, mla_sin, LANES)
    diff_cos, diff_sin = _rope_tables(pos, DIFF_ROT_DIM // 2, DIFF_ROPE_THETA)
    diff_c, diff_s = _lane_tables(diff_cos, diff_sin, DIFF_HEAD_DIM)

    tm = 512
    q, k, v = _mla_proj(x2d, mla_c, mla_s, mla_w_dq[0], mla_q_norm[0], mla_w_uq[0],
                        mla_w_dkv[0], mla_kv_norm[0], mla_w_ukv[0], tm)
    o = _mla_flash(q, k, v, batch, seq, tq=512, tk=512)
    x2d = _out_proj_ln(o, mla_w_o[0], x2d, ln_mix_g[0], ln_mix_b[0], tm)
    x2d = _mlp_ln(x2d, ffn_w_up[0], ffn_w_down[0], ln_ffn_g[0], ln_ffn_b[0], tm, 512)

    lambda_init = 0.8 - 0.6 * math.exp(-0.3 * 1)
    q, k, v = _diff_proj(x2d, diff_c, diff_s, diff_w_qkv[0], tm)
    o = _diff_flash(q, k, v, diff_lambda_q1[0], diff_lambda_k1[0], diff_lambda_q2[0],
                    diff_lambda_k2[0], diff_subln[0], lambda_init, batch, seq, tq=256, tk=512)
    x2d = _out_proj_ln(o, diff_w_o[0], x2d, ln_mix_g[1], ln_mix_b[1], tm)
    x2d = _mlp_ln(x2d, ffn_w_up[1], ffn_w_down[1], ln_ffn_g[1], ln_ffn_b[1], tm, 512)
    return x2d.reshape(batch, seq, D_MODEL)
```

```python
import functools
import math

import jax
import jax.numpy as jnp
from jax import lax
from jax.experimental import pallas as pl
from jax.experimental.pallas import tpu as pltpu

D_MODEL = 1024
DEPTH = 2

MLA_NOPE = 128
MLA_ROPE = 64
MLA_V = 128
MLA_HEADS = 8
MLA_Q_LORA = 384
MLA_KV_LORA = 256
MLA_ROPE_THETA = 10000.0
MLA_QK_PAD = 256

DIFF_HEAD_DIM = 64
DIFF_HEADS = 8
DIFF_ROT_DIM = 16
DIFF_ROPE_THETA = 500000.0

D_FF = 4 * D_MODEL
DEEPNORM_ALPHA = (2 * DEPTH) ** 0.25
LN_EPS = 1e-5
RMS_EPS = 1e-6

LANES = 128
NEG_BIG = -1e30
LOG2_E = math.log2(math.e)
VMEM_LIMIT_BYTES = 48 * 1024 * 1024

F32 = jnp.float32
BF16 = jnp.bfloat16


def _cparams(*sem):
    return pltpu.CompilerParams(dimension_semantics=sem, vmem_limit_bytes=VMEM_LIMIT_BYTES)


def _dot(a, b):
    return jnp.dot(a, b, preferred_element_type=F32)


def _rms(x, g):
    return x * lax.rsqrt(jnp.mean(jnp.square(x), axis=-1, keepdims=True) + RMS_EPS) * g


def _layernorm(y, g, b):
    mu = jnp.mean(y, axis=-1, keepdims=True)
    d = y - mu
    var = jnp.mean(jnp.square(d), axis=-1, keepdims=True)
    return d * lax.rsqrt(var + LN_EPS) * g + b


def _rope_table_kernel(pos_ref, invf_ref, cos_ref, sin_ref):
    ang = pos_ref[...].astype(F32) * invf_ref[...]
    cos_ref[...] = jnp.cos(ang)
    sin_ref[...] = jnp.sin(ang)


def _rope_tables(positions, half, theta):
    t = positions.size
    rep = LANES // half
    inv_freq = theta ** (-jnp.arange(0, 2 * half, 2, dtype=F32) / (2 * half))
    pos_dense = jnp.repeat(positions.reshape(t // rep, rep), half, axis=1)
    invf = jnp.tile(inv_freq, rep).reshape(1, LANES)
    rows = t // rep
    cos, sin = pl.pallas_call(
        _rope_table_kernel,
        out_shape=[jax.ShapeDtypeStruct((rows, LANES), F32)] * 2,
        name="rope_table",
    )(pos_dense, invf)
    return cos.reshape(t, half), sin.reshape(t, half)


def _rotate_half_in_lanes(x, half, group, c_tab, s_tab):
    lane = lax.broadcasted_iota(jnp.int32, x.shape, 1)
    from_above = pltpu.roll(x, LANES - half, axis=1)
    from_below = pltpu.roll(x, half, axis=1)
    partner = jnp.where(lane % group < half, from_above, from_below)
    return x * c_tab + partner * s_tab


def _mla_proj_kernel(x_ref, wdq_ref, qn_ref, wuq_ref, wdkv_ref, kvn_ref, wukv_ref,
                     c_ref, s_ref, q_ref, k_ref, v_ref, *, scale):
    xb = x_ref[...].astype(BF16)
    c_tab = c_ref[...]
    s_tab = s_ref[...]
    rope = functools.partial(_rotate_half_in_lanes, half=MLA_ROPE // 2, group=LANES,
                             c_tab=c_tab, s_tab=s_tab)

    cq = _rms(_dot(xb, wdq_ref[...]), qn_ref[...]).astype(BF16)
    q = _dot(cq, wuq_ref[...])
    ckv = _dot(xb, wdkv_ref[...])
    c_kv = _rms(ckv[:, :MLA_KV_LORA], kvn_ref[...]).astype(BF16)
    k_pe = rope(ckv[:, MLA_KV_LORA:]).astype(BF16)
    kv = _dot(c_kv, wukv_ref[...])

    for h in range(MLA_HEADS):
        base = h * MLA_QK_PAD
        q_ref[:, base:base + LANES] = (q[:, base:base + LANES] * scale).astype(BF16)
        q_ref[:, base + LANES:base + 2 * LANES] = (rope(q[:, base + LANES:base + 2 * LANES]) * scale).astype(BF16)
        k_ref[:, base:base + LANES] = kv[:, base:base + LANES].astype(BF16)
        k_ref[:, base + LANES:base + 2 * LANES] = k_pe
        v_ref[:, h * MLA_V:(h + 1) * MLA_V] = kv[:, base + LANES:base + 2 * LANES].astype(BF16)


def _mla_proj(x2d, c_tab, s_tab, w_dq, q_norm, w_uq, w_dkv, kv_norm, w_ukv, tm):
    t = x2d.shape[0]
    h = MLA_HEADS
    pad = MLA_QK_PAD - MLA_NOPE - MLA_ROPE
    wuq = jnp.pad(w_uq.reshape(MLA_Q_LORA, h, MLA_NOPE + MLA_ROPE), ((0, 0), (0, 0), (0, pad)))
    wuq = wuq.reshape(MLA_Q_LORA, h * MLA_QK_PAD).astype(BF16)
    wdkv = jnp.pad(w_dkv, ((0, 0), (0, pad))).astype(BF16)
    row = lambda i: (i, 0)
    full = lambda i: (0, 0)
    return pl.pallas_call(
        functools.partial(_mla_proj_kernel, scale=LOG2_E * (MLA_NOPE + MLA_ROPE) ** -0.5),
        grid=(t // tm,),
        in_specs=[
            pl.BlockSpec((tm, D_MODEL), row),
            pl.BlockSpec((D_MODEL, MLA_Q_LORA), full),
            pl.BlockSpec((1, MLA_Q_LORA), full),
            pl.BlockSpec((MLA_Q_LORA, h * MLA_QK_PAD), full),
            pl.BlockSpec((D_MODEL, MLA_KV_LORA + LANES), full),
            pl.BlockSpec((1, MLA_KV_LORA), full),
            pl.BlockSpec((MLA_KV_LORA, h * (MLA_NOPE + MLA_V)), full),
            pl.BlockSpec((tm, LANES), row),
            pl.BlockSpec((tm, LANES), row),
        ],
        out_specs=[
            pl.BlockSpec((tm, h * MLA_QK_PAD), row),
            pl.BlockSpec((tm, h * MLA_QK_PAD), row),
            pl.BlockSpec((tm, h * MLA_V), row),
        ],
        out_shape=[
            jax.ShapeDtypeStruct((t, h * MLA_QK_PAD), BF16),
            jax.ShapeDtypeStruct((t, h * MLA_QK_PAD), BF16),
            jax.ShapeDtypeStruct((t, h * MLA_V), BF16),
        ],
        compiler_params=_cparams("parallel"),
        name="mla_proj",
    )(x2d, w_dq.astype(BF16), q_norm.reshape(1, -1), wuq, wdkv, kv_norm.reshape(1, -1),
      w_ukv.astype(BF16), c_tab, s_tab)


def _diff_proj_kernel(x_ref, w_ref, c_ref, s_ref, q_ref, k_ref, v_ref, *, scale):
    xb = x_ref[...].astype(BF16)
    c_tab = c_ref[...]
    s_tab = s_ref[...]
    rope = functools.partial(_rotate_half_in_lanes, half=DIFF_ROT_DIM // 2, group=DIFF_HEAD_DIM,
                             c_tab=c_tab, s_tab=s_tab)
    qkv = _dot(xb, w_ref[...])
    for j in range(D_MODEL // LANES):
        lo, hi = j * LANES, (j + 1) * LANES
        q_ref[:, lo:hi] = (rope(qkv[:, lo:hi]) * scale).astype(BF16)
        k_ref[:, lo:hi] = rope(qkv[:, D_MODEL + lo:D_MODEL + hi]).astype(BF16)
    v_ref[...] = qkv[:, 2 * D_MODEL:].astype(BF16)


def _diff_proj(x2d, c_tab, s_tab, w_qkv, tm):
    t = x2d.shape[0]
    row = lambda i: (i, 0)
    full = lambda i: (0, 0)
    return pl.pallas_call(
        functools.partial(_diff_proj_kernel, scale=LOG2_E * DIFF_HEAD_DIM ** -0.5),
        grid=(t // tm,),
        in_specs=[
            pl.BlockSpec((tm, D_MODEL), row),
            pl.BlockSpec((D_MODEL, 3 * D_MODEL), full),
            pl.BlockSpec((tm, LANES), row),
            pl.BlockSpec((tm, LANES), row),
        ],
        out_specs=[pl.BlockSpec((tm, D_MODEL), row)] * 3,
        out_shape=[jax.ShapeDtypeStruct((t, D_MODEL), BF16)] * 3,
        compiler_params=_cparams("parallel"),
        name="diff_proj",
    )(x2d, w_qkv.astype(BF16), c_tab, s_tab)


def _flash_loop(qs_ref, k_ref, v_ref, s_refs, m_ref, l_ref, acc_ref, q_start, tq, tk):
    assert tk % tq == 0
    rows = qs_ref.shape[0]
    s0_ref, s1_ref = s_refs
    m_ref[...] = jnp.full(m_ref.shape, NEG_BIG, F32)
    l_ref[...] = jnp.zeros(l_ref.shape, F32)
    acc_ref[...] = jnp.zeros(acc_ref.shape, F32)

    def scores(j, s_ref):
        start = pl.multiple_of(j * tk, tk)
        k = k_ref[pl.ds(start, tk), :]
        s_ref[...] = lax.dot_general(qs_ref[...], k, (((1,), (1,)), ((), ())),
                                     preferred_element_type=F32)

    def consume(j, s_ref, masked):
        start = pl.multiple_of(j * tk, tk)
        v = v_ref[pl.ds(start, tk), :]
        if masked:
            q_pos = q_start + lax.broadcasted_iota(jnp.int32, (rows, tk), 0) % tq
            k_pos = start + lax.broadcasted_iota(jnp.int32, (rows, tk), 1)
            s_ref[...] = jnp.where(k_pos <= q_pos, s_ref[...], NEG_BIG)
        m_prev = m_ref[...]
        m_new = jnp.maximum(m_prev, jnp.max(s_ref[...], axis=-1, keepdims=True))
        alpha = jnp.exp2(m_prev - m_new)
        p = [jnp.exp2(s_ref[:, c:c + LANES] - m_new) for c in range(0, tk, LANES)]
        l_ref[...] = alpha * l_ref[...] + functools.reduce(jnp.add, p)
        pv = _dot(jnp.concatenate(p, axis=1).astype(BF16), v)
        acc_ref[...] = alpha * acc_ref[...] + pv
        m_ref[...] = m_new

    n_full = q_start // tk

    def pair(i, carry):
        j = 2 * i
        scores(j + 1, s1_ref)
        consume(j, s0_ref, masked=False)
        scores(j + 2, s0_ref)
        consume(j + 1, s1_ref, masked=False)
        return carry

    scores(0, s0_ref)
    lax.fori_loop(0, n_full // 2, pair, 0)

    @pl.when(n_full % 2 == 1)
    def _():
        scores(n_full, s1_ref)
        consume(n_full - 1, s0_ref, masked=False)
        consume(n_full, s1_ref, masked=True)

    @pl.when(n_full % 2 == 0)
    def _():
        consume(n_full, s0_ref, masked=True)


def _row_sum(l):
    return jnp.sum(l, axis=-1, keepdims=True)


def _mla_flash_kernel(q_ref, k_ref, v_ref, o_ref, s0_ref, s1_ref, m_ref, l_ref, acc_ref, *, tq, tk):
    q_start = pl.program_id(2) * tq
    _flash_loop(q_ref, k_ref, v_ref, (s0_ref, s1_ref), m_ref, l_ref, acc_ref, q_start, tq, tk)
    o_ref[...] = (acc_ref[...] / _row_sum(l_ref[...])).astype(o_ref.dtype)


def _diff_flash_kernel(q_ref, k_ref, v_ref, lq1_ref, lk1_ref, lq2_ref, lk2_ref, subln_ref,
                       o_ref, qs_ref, s0_ref, s1_ref, m_ref, l_ref, acc_ref, *, tq, tk, lambda_init):
    q_start = pl.program_id(2) * tq
    q = q_ref[...]
    lane = lax.broadcasted_iota(jnp.int32, q.shape, 1)
    zero = jnp.zeros_like(q)
    qs_ref[:tq, :] = jnp.where(lane < DIFF_HEAD_DIM, q, zero)
    qs_ref[tq:, :] = jnp.where(lane < DIFF_HEAD_DIM, zero, q)
    _flash_loop(qs_ref, k_ref, v_ref, (s0_ref, s1_ref), m_ref, l_ref, acc_ref, q_start, tq, tk)
    lam = (jnp.exp(jnp.sum(lq1_ref[...] * lk1_ref[...]))
           - jnp.exp(jnp.sum(lq2_ref[...] * lk2_ref[...])) + lambda_init)
    o1 = acc_ref[:tq, :] / _row_sum(l_ref[:tq, :])
    o2 = acc_ref[tq:, :] / _row_sum(l_ref[tq:, :])
    o = _rms(o1 - lam * o2, subln_ref[...]) * (1.0 - lambda_init)
    o_ref[...] = o.astype(o_ref.dtype)


def _flash_scratch(rows, tk, dv):
    return [pltpu.VMEM((rows, tk), F32), pltpu.VMEM((rows, tk), F32),
            pltpu.VMEM((rows, LANES), F32), pltpu.VMEM((rows, LANES), F32),
            pltpu.VMEM((rows, dv), F32)]


def _flash_specs(seq, n_q, tq, dk, dv):
    q_spec = pl.BlockSpec((tq, dk), lambda b, h, i: (b * n_q + i, h))
    k_spec = pl.BlockSpec((seq, dk), lambda b, h, i: (b, h))
    v_spec = pl.BlockSpec((seq, dv), lambda b, h, i: (b, h))
    o_spec = pl.BlockSpec((tq, dv), lambda b, h, i: (b * n_q + i, h))
    return q_spec, k_spec, v_spec, o_spec


def _mla_flash(q, k, v, batch, seq, tq, tk):
    n_q = seq // tq
    q_spec, k_spec, v_spec, o_spec = _flash_specs(seq, n_q, tq, MLA_QK_PAD, MLA_V)
    return pl.pallas_call(
        functools.partial(_mla_flash_kernel, tq=tq, tk=tk),
        grid=(batch, MLA_HEADS, n_q),
        in_specs=[q_spec, k_spec, v_spec],
        out_specs=o_spec,
        out_shape=jax.ShapeDtypeStruct((batch * seq, MLA_HEADS * MLA_V), BF16),
        scratch_shapes=_flash_scratch(tq, tk, MLA_V),
        compiler_params=_cparams("parallel", "parallel", "arbitrary"),
        name="mla_flash",
    )(q, k, v)


def _diff_flash(q, k, v, lq1, lk1, lq2, lk2, subln, lambda_init, batch, seq, tq, tk):
    n_q = seq // tq
    dv = 2 * DIFF_HEAD_DIM
    q_spec, k_spec, v_spec, o_spec = _flash_specs(seq, n_q, tq, dv, dv)
    vec = lambda n: pl.BlockSpec((1, n), lambda b, h, i: (0, 0))
    return pl.pallas_call(
        functools.partial(_diff_flash_kernel, tq=tq, tk=tk, lambda_init=lambda_init),
        grid=(batch, DIFF_HEADS, n_q),
        in_specs=[q_spec, k_spec, v_spec, vec(DIFF_HEAD_DIM), vec(DIFF_HEAD_DIM),
                  vec(DIFF_HEAD_DIM), vec(DIFF_HEAD_DIM), vec(dv)],
        out_specs=o_spec,
        out_shape=jax.ShapeDtypeStruct((batch * seq, DIFF_HEADS * dv), BF16),
        scratch_shapes=[pltpu.VMEM((2 * tq, dv), BF16)] + _flash_scratch(2 * tq, tk, dv),
        compiler_params=_cparams("parallel", "parallel", "arbitrary"),
        name="diff_flash",
    )(q, k, v, lq1.reshape(1, -1), lk1.reshape(1, -1), lq2.reshape(1, -1), lk2.reshape(1, -1),
      subln.reshape(1, -1))


def _out_proj_ln_kernel(o_ref, w_ref, x_ref, g_ref, b_ref, y_ref):
    y = DEEPNORM_ALPHA * x_ref[...] + _dot(o_ref[...], w_ref[...])
    y_ref[...] = _layernorm(y, g_ref[...], b_ref[...])


def _out_proj_ln(o, w_o, x2d, g, b, tm):
    t = x2d.shape[0]
    row = lambda i: (i, 0)
    full = lambda i: (0, 0)
    return pl.pallas_call(
        _out_proj_ln_kernel,
        grid=(t // tm,),
        in_specs=[
            pl.BlockSpec((tm, D_MODEL), row),
            pl.BlockSpec((D_MODEL, D_MODEL), full),
            pl.BlockSpec((tm, D_MODEL), row),
            pl.BlockSpec((1, D_MODEL), full),
            pl.BlockSpec((1, D_MODEL), full),
        ],
        out_specs=pl.BlockSpec((tm, D_MODEL), row),
        out_shape=jax.ShapeDtypeStruct((t, D_MODEL), F32),
        compiler_params=_cparams("parallel"),
        name="out_proj_ln",
    )(o, w_o.astype(BF16), x2d, g.reshape(1, -1), b.reshape(1, -1))


def _mlp_ln_kernel(x_ref, wu_ref, wd_ref, g_ref, b_ref, y_ref, acc_ref):
    j = pl.program_id(1)

    @pl.when(j == 0)
    def _():
        acc_ref[...] = jnp.zeros(acc_ref.shape, F32)

    hidden = jnp.square(jnp.maximum(_dot(x_ref[...].astype(BF16), wu_ref[...]), 0.0))
    acc_ref[...] += _dot(hidden.astype(BF16), wd_ref[...])

    @pl.when(j == pl.num_programs(1) - 1)
    def _():
        y = DEEPNORM_ALPHA * x_ref[...] + acc_ref[...]
        y_ref[...] = _layernorm(y, g_ref[...], b_ref[...])


def _mlp_ln(x2d, w_up, w_down, g, b, tm, tf):
    t = x2d.shape[0]
    return pl.pallas_call(
        _mlp_ln_kernel,
        grid=(t // tm, D_FF // tf),
        in_specs=[
            pl.BlockSpec((tm, D_MODEL), lambda i, j: (i, 0)),
            pl.BlockSpec((D_MODEL, tf), lambda i, j: (0, j)),
            pl.BlockSpec((tf, D_MODEL), lambda i, j: (j, 0)),
            pl.BlockSpec((1, D_MODEL), lambda i, j: (0, 0)),
            pl.BlockSpec((1, D_MODEL), lambda i, j: (0, 0)),
        ],
        out_specs=pl.BlockSpec((tm, D_MODEL), lambda i, j: (i, 0)),
        out_shape=jax.ShapeDtypeStruct((t, D_MODEL), F32),
        scratch_shapes=[pltpu.VMEM((tm, D_MODEL), F32)],
        compiler_params=_cparams("parallel", "arbitrary"),
        name="mlp_ln",
    )(x2d, w_up.astype(BF16), w_down.astype(BF16), g.reshape(1, -1), b.reshape(1, -1))


def _lane_tables(cos, sin, group):
    t, half = cos.shape
    rest = group - 2 * half
    c = jnp.concatenate([cos, cos, jnp.ones((t, rest), F32)], axis=1)
    s = jnp.concatenate([-sin, sin, jnp.zeros((t, rest), F32)], axis=1)
    reps = LANES // group
    return jnp.tile(c, (1, reps)), jnp.tile(s, (1, reps))


def kernel(x, positions, ln_mix_g, ln_mix_b, mla_w_dq, mla_q_norm, mla_w_uq, mla_w_dkv, mla_kv_norm, mla_w_ukv, mla_w_o, diff_w_qkv, diff_lambda_q1, diff_lambda_k1, diff_lambda_q2, diff_lambda_k2, diff_subln, diff_w_o, ln_ffn_g, ln_ffn_b, ffn_w_up, ffn_w_down):
    batch, seq, _ = x.shape
    x2d = x.reshape(batch * seq, D_MODEL)
    pos = positions.reshape(-1)

    mla_cos, mla_sin = _rope_tables(pos, MLA_ROPE // 2, MLA_ROPE_THETA)
    mla_c, mla_s = _lane_tables(mla_cos, mla_sin, LANES)
    diff_cos, diff_sin = _rope_tables(pos, DIFF_ROT_DIM // 2, DIFF_ROPE_THETA)
    diff_c, diff_s = _lane_tables(diff_cos, diff_sin, DIFF_HEAD_DIM)

    tm = 512
    q, k, v = _mla_proj(x2d, mla_c, mla_s, mla_w_dq[0], mla_q_norm[0], mla_w_uq[0],
                        mla_w_dkv[0], mla_kv_norm[0], mla_w_ukv[0], tm)
    o = _mla_flash(q, k, v, batch, seq, tq=512, tk=512)
    x2d = _out_proj_ln(o, mla_w_o[0], x2d, ln_mix_g[0], ln_mix_b[0], tm)
    x2d = _mlp_ln(x2d, ffn_w_up[0], ffn_w_down[0], ln_ffn_g[0], ln_ffn_b[0], tm, 512)

    lambda_init = 0.8 - 0.6 * math.exp(-0.3 * 1)
    q, k, v = _diff_proj(x2d, diff_c, diff_s, diff_w_qkv[0], tm)
    o = _diff_flash(q, k, v, diff_lambda_q1[0], diff_lambda_k1[0], diff_lambda_q2[0],
                    diff_lambda_k2[0], diff_subln[0], lambda_init, batch, seq, tq=512, tk=512)
    x2d = _out_proj_ln(o, diff_w_o[0], x2d, ln_mix_g[1], ln_mix_b[1], tm)
    x2d = _mlp_ln(x2d, ffn_w_up[1], ffn_w_down[1], ln_ffn_g[1], ln_ffn_b[1], tm, 512)
    return x2d.reshape(batch, seq, D_MODEL)
```

```python
import functools
import math

import jax
import jax.numpy as jnp
from jax import lax
from jax.experimental import pallas as pl
from jax.experimental.pallas import tpu as pltpu

D_MODEL = 1024
DEPTH = 2

MLA_NOPE = 128
MLA_ROPE = 64
MLA_V = 128
MLA_HEADS = 8
MLA_Q_LORA = 384
MLA_KV_LORA = 256
MLA_ROPE_THETA = 10000.0
MLA_QK_PAD = 256

DIFF_HEAD_DIM = 64
DIFF_HEADS = 8
DIFF_ROT_DIM = 16
DIFF_ROPE_THETA = 500000.0

D_FF = 4 * D_MODEL
DEEPNORM_ALPHA = (2 * DEPTH) ** 0.25
LN_EPS = 1e-5
RMS_EPS = 1e-6

LANES = 128
NEG_BIG = -1e30
LOG2_E = math.log2(math.e)
VMEM_LIMIT_BYTES = 48 * 1024 * 1024

F32 = jnp.float32
BF16 = jnp.bfloat16


def _cparams(*sem):
    return pltpu.CompilerParams(dimension_semantics=sem, vmem_limit_bytes=VMEM_LIMIT_BYTES)


def _dot(a, b):
    return jnp.dot(a, b, preferred_element_type=F32)


def _rms(x, g):
    return x * lax.rsqrt(jnp.mean(jnp.square(x), axis=-1, keepdims=True) + RMS_EPS) * g


def _layernorm(y, g, b):
    mu = jnp.mean(y, axis=-1, keepdims=True)
    d = y - mu
    var = jnp.mean(jnp.square(d), axis=-1, keepdims=True)
    return d * lax.rsqrt(var + LN_EPS) * g + b


def _rope_table_kernel(pos_ref, invf_ref, cos_ref, sin_ref):
    ang = pos_ref[...].astype(F32) * invf_ref[...]
    cos_ref[...] = jnp.cos(ang)
    sin_ref[...] = jnp.sin(ang)


def _rope_tables(positions, half, theta):
    t = positions.size
    rep = LANES // half
    inv_freq = theta ** (-jnp.arange(0, 2 * half, 2, dtype=F32) / (2 * half))
    pos_dense = jnp.repeat(positions.reshape(t // rep, rep), half, axis=1)
    invf = jnp.tile(inv_freq, rep).reshape(1, LANES)
    rows = t // rep
    cos, sin = pl.pallas_call(
        _rope_table_kernel,
        out_shape=[jax.ShapeDtypeStruct((rows, LANES), F32)] * 2,
        name="rope_table",
    )(pos_dense, invf)
    return cos.reshape(t, half), sin.reshape(t, half)


def _rotate_half_in_lanes(x, half, group, c_tab, s_tab):
    lane = lax.broadcasted_iota(jnp.int32, x.shape, 1)
    from_above = pltpu.roll(x, LANES - half, axis=1)
    from_below = pltpu.roll(x, half, axis=1)
    partner = jnp.where(lane % group < half, from_above, from_below)
    return x * c_tab + partner * s_tab


def _mla_proj_kernel(x_ref, wdq_ref, qn_ref, wuq_ref, wdkv_ref, kvn_ref, wukv_ref,
                     c_ref, s_ref, q_ref, k_ref, v_ref, *, scale):
    xb = x_ref[...].astype(BF16)
    c_tab = c_ref[...]
    s_tab = s_ref[...]
    rope = functools.partial(_rotate_half_in_lanes, half=MLA_ROPE // 2, group=LANES,
                             c_tab=c_tab, s_tab=s_tab)

    cq = _rms(_dot(xb, wdq_ref[...]), qn_ref[...]).astype(BF16)
    q = _dot(cq, wuq_ref[...])
    ckv = _dot(xb, wdkv_ref[...])
    c_kv = _rms(ckv[:, :MLA_KV_LORA], kvn_ref[...]).astype(BF16)
    k_pe = rope(ckv[:, MLA_KV_LORA:]).astype(BF16)
    kv = _dot(c_kv, wukv_ref[...])

    for h in range(MLA_HEADS):
        base = h * MLA_QK_PAD
        q_ref[:, base:base + LANES] = (q[:, base:base + LANES] * scale).astype(BF16)
        q_ref[:, base + LANES:base + 2 * LANES] = (rope(q[:, base + LANES:base + 2 * LANES]) * scale).astype(BF16)
        k_ref[:, base:base + LANES] = kv[:, base:base + LANES].astype(BF16)
        k_ref[:, base + LANES:base + 2 * LANES] = k_pe
        v_ref[:, h * MLA_V:(h + 1) * MLA_V] = kv[:, base + LANES:base + 2 * LANES].astype(BF16)


def _mla_proj(x2d, c_tab, s_tab, w_dq, q_norm, w_uq, w_dkv, kv_norm, w_ukv, tm):
    t = x2d.shape[0]
    h = MLA_HEADS
    pad = MLA_QK_PAD - MLA_NOPE - MLA_ROPE
    wuq = jnp.pad(w_uq.reshape(MLA_Q_LORA, h, MLA_NOPE + MLA_ROPE), ((0, 0), (0, 0), (0, pad)))
    wuq = wuq.reshape(MLA_Q_LORA, h * MLA_QK_PAD).astype(BF16)
    wdkv = jnp.pad(w_dkv, ((0, 0), (0, pad))).astype(BF16)
    row = lambda i: (i, 0)
    full = lambda i: (0, 0)
    return pl.pallas_call(
        functools.partial(_mla_proj_kernel, scale=LOG2_E * (MLA_NOPE + MLA_ROPE) ** -0.5),
        grid=(t // tm,),
        in_specs=[
            pl.BlockSpec((tm, D_MODEL), row),
            pl.BlockSpec((D_MODEL, MLA_Q_LORA), full),
            pl.BlockSpec((1, MLA_Q_LORA), full),
            pl.BlockSpec((MLA_Q_LORA, h * MLA_QK_PAD), full),
            pl.BlockSpec((D_MODEL, MLA_KV_LORA + LANES), full),
            pl.BlockSpec((1, MLA_KV_LORA), full),
            pl.BlockSpec((MLA_KV_LORA, h * (MLA_NOPE + MLA_V)), full),
            pl.BlockSpec((tm, LANES), row),
            pl.BlockSpec((tm, LANES), row),
        ],
        out_specs=[
            pl.BlockSpec((tm, h * MLA_QK_PAD), row),
            pl.BlockSpec((tm, h * MLA_QK_PAD), row),
            pl.BlockSpec((tm, h * MLA_V), row),
        ],
        out_shape=[
            jax.ShapeDtypeStruct((t, h * MLA_QK_PAD), BF16),
            jax.ShapeDtypeStruct((t, h * MLA_QK_PAD), BF16),
            jax.ShapeDtypeStruct((t, h * MLA_V), BF16),
        ],
        compiler_params=_cparams("parallel"),
        name="mla_proj",
    )(x2d, w_dq.astype(BF16), q_norm.reshape(1, -1), wuq, wdkv, kv_norm.reshape(1, -1),
      w_ukv.astype(BF16), c_tab, s_tab)


def _diff_proj_kernel(x_ref, w_ref, c_ref, s_ref, q_ref, k_ref, v_ref, *, scale):
    xb = x_ref[...].astype(BF16)
    c_tab = c_ref[...]
    s_tab = s_ref[...]
    rope = functools.partial(_rotate_half_in_lanes, half=DIFF_ROT_DIM // 2, group=DIFF_HEAD_DIM,
                             c_tab=c_tab, s_tab=s_tab)
    qkv = _dot(xb, w_ref[...])
    for j in range(D_MODEL // LANES):
        lo, hi = j * LANES, (j + 1) * LANES
        q_ref[:, lo:hi] = (rope(qkv[:, lo:hi]) * scale).astype(BF16)
        k_ref[:, lo:hi] = rope(qkv[:, D_MODEL + lo:D_MODEL + hi]).astype(BF16)
    v_ref[...] = qkv[:, 2 * D_MODEL:].astype(BF16)


def _diff_proj(x2d, c_tab, s_tab, w_qkv, tm):
    t = x2d.shape[0]
    row = lambda i: (i, 0)
    full = lambda i: (0, 0)
    return pl.pallas_call(
        functools.partial(_diff_proj_kernel, scale=LOG2_E * DIFF_HEAD_DIM ** -0.5),
        grid=(t // tm,),
        in_specs=[
            pl.BlockSpec((tm, D_MODEL), row),
            pl.BlockSpec((D_MODEL, 3 * D_MODEL), full),
            pl.BlockSpec((tm, LANES), row),
            pl.BlockSpec((tm, LANES), row),
        ],
        out_specs=[pl.BlockSpec((tm, D_MODEL), row)] * 3,
        out_shape=[jax.ShapeDtypeStruct((t, D_MODEL), BF16)] * 3,
        compiler_params=_cparams("parallel"),
        name="diff_proj",
    )(x2d, w_qkv.astype(BF16), c_tab, s_tab)


def _flash_loop(qs_ref, k_ref, v_ref, s_refs, m_ref, l_ref, acc_ref, q_start, tq, tk):
    assert tk % tq == 0
    rows = qs_ref.shape[0]
    s0_ref, s1_ref = s_refs
    m_ref[...] = jnp.full(m_ref.shape, NEG_BIG, F32)
    l_ref[...] = jnp.zeros(l_ref.shape, F32)
    acc_ref[...] = jnp.zeros(acc_ref.shape, F32)

    def scores(j, s_ref):
        start = pl.multiple_of(j * tk, tk)
        k = k_ref[pl.ds(start, tk), :]
        s_ref[...] = lax.dot_general(qs_ref[...], k, (((1,), (1,)), ((), ())),
                                     preferred_element_type=F32)

    def consume(j, s_ref, masked):
        start = pl.multiple_of(j * tk, tk)
        v = v_ref[pl.ds(start, tk), :]
        if masked:
            q_pos = q_start + lax.broadcasted_iota(jnp.int32, (rows, tk), 0) % tq
            k_pos = start + lax.broadcasted_iota(jnp.int32, (rows, tk), 1)
            s_ref[...] = jnp.where(k_pos <= q_pos, s_ref[...], NEG_BIG)
        m_prev = m_ref[...]
        m_new = jnp.maximum(m_prev, jnp.max(s_ref[...], axis=-1, keepdims=True))
        alpha = jnp.exp2(m_prev - m_new)
        p = [jnp.exp2(s_ref[:, c:c + LANES] - m_new) for c in range(0, tk, LANES)]
        l_ref[...] = alpha * l_ref[...] + functools.reduce(jnp.add, p)
        pv = _dot(jnp.concatenate(p, axis=1).astype(BF16), v)
        acc_ref[...] = alpha * acc_ref[...] + pv
        m_ref[...] = m_new

    n_full = q_start // tk

    def pair(i, carry):
        j = 2 * i
        scores(j + 1, s1_ref)
        consume(j, s0_ref, masked=False)
        scores(j + 2, s0_ref)
        consume(j + 1, s1_ref, masked=False)
        return carry

    scores(0, s0_ref)
    lax.fori_loop(0, n_full // 2, pair, 0)

    @pl.when(n_full % 2 == 1)
    def _():
        scores(n_full, s1_ref)
        consume(n_full - 1, s0_ref, masked=False)
        consume(n_full, s1_ref, masked=True)

    @pl.when(n_full % 2 == 0)
    def _():
        consume(n_full, s0_ref, masked=True)


def _row_sum(l):
    return jnp.sum(l, axis=-1, keepdims=True)


def _mla_flash_kernel(q_ref, k_ref, v_ref, o_ref, s0_ref, s1_ref, m_ref, l_ref, acc_ref, *, tq, tk):
    q_start = pl.program_id(2) * tq
    _flash_loop(q_ref, k_ref, v_ref, (s0_ref, s1_ref), m_ref, l_ref, acc_ref, q_start, tq, tk)
    o_ref[...] = (acc_ref[...] / _row_sum(l_ref[...])).astype(o_ref.dtype)


def _diff_flash_kernel(q_ref, k_ref, v_ref, lq1_ref, lk1_ref, lq2_ref, lk2_ref, subln_ref,
                       o_ref, qs_ref, s0_ref, s1_ref, m_ref, l_ref, acc_ref, *, tq, tk, lambda_init):
    q_start = pl.program_id(2) * tq
    q = q_ref[...]
    lane = lax.broadcasted_iota(jnp.int32, q.shape, 1)
    zero = jnp.zeros_like(q)
    qs_ref[:tq, :] = jnp.where(lane < DIFF_HEAD_DIM, q, zero)
    qs_ref[tq:, :] = jnp.where(lane < DIFF_HEAD_DIM, zero, q)
    _flash_loop(qs_ref, k_ref, v_ref, (s0_ref, s1_ref), m_ref, l_ref, acc_ref, q_start, tq, tk)
    lam = (jnp.exp(jnp.sum(lq1_ref[...] * lk1_ref[...]))
           - jnp.exp(jnp.sum(lq2_ref[...] * lk2_ref[...])) + lambda_init)
    o1 = acc_ref[:tq, :] / _row_sum(l_ref[:tq, :])
    o2 = acc_ref[tq:, :] / _row_sum(l_ref[tq:, :])
    o = _rms(o1 - lam * o2, subln_ref[...]) * (1.0 - lambda_init)
    o_ref[...] = o.astype(o_ref.dtype)


def _flash_scratch(rows, tk, dv):
    return [pltpu.VMEM((rows, tk), F32), pltpu.VMEM((rows, tk), F32),
            pltpu.VMEM((rows, LANES), F32), pltpu.VMEM((rows, LANES), F32),
            pltpu.VMEM((rows, dv), F32)]


def _flash_specs(seq, n_q, tq, dk, dv):
    q_spec = pl.BlockSpec((tq, dk), lambda b, h, i: (b * n_q + i, h))
    k_spec = pl.BlockSpec((seq, dk), lambda b, h, i: (b, h))
    v_spec = pl.BlockSpec((seq, dv), lambda b, h, i: (b, h))
    o_spec = pl.BlockSpec((tq, dv), lambda b, h, i: (b * n_q + i, h))
    return q_spec, k_spec, v_spec, o_spec


def _mla_flash(q, k, v, batch, seq, tq, tk):
    n_q = seq // tq
    q_spec, k_spec, v_spec, o_spec = _flash_specs(seq, n_q, tq, MLA_QK_PAD, MLA_V)
    return pl.pallas_call(
        functools.partial(_mla_flash_kernel, tq=tq, tk=tk),
        grid=(batch, MLA_HEADS, n_q),
        in_specs=[q_spec, k_spec, v_spec],
        out_specs=o_spec,
        out_shape=jax.ShapeDtypeStruct((batch * seq, MLA_HEADS * MLA_V), BF16),
        scratch_shapes=_flash_scratch(tq, tk, MLA_V),
        compiler_params=_cparams("parallel", "parallel", "arbitrary"),
        name="mla_flash",
    )(q, k, v)


def _diff_flash(q, k, v, lq1, lk1, lq2, lk2, subln, lambda_init, batch, seq, tq, tk):
    n_q = seq // tq
    dv = 2 * DIFF_HEAD_DIM
    q_spec, k_spec, v_spec, o_spec = _flash_specs(seq, n_q, tq, dv, dv)
    vec = lambda n: pl.BlockSpec((1, n), lambda b, h, i: (0, 0))
    return pl.pallas_call(
        functools.partial(_diff_flash_kernel, tq=tq, tk=tk, lambda_init=lambda_init),
        grid=(batch, DIFF_HEADS, n_q),
        in_specs=[q_spec, k_spec, v_spec, vec(DIFF_HEAD_DIM), vec(DIFF_HEAD_DIM),
                  vec(DIFF_HEAD_DIM), vec(DIFF_HEAD_DIM), vec(dv)],
        out_specs=o_spec,
        out_shape=jax.ShapeDtypeStruct((batch * seq, DIFF_HEADS * dv), BF16),
        scratch_shapes=[pltpu.VMEM((2 * tq, dv), BF16)] + _flash_scratch(2 * tq, tk, dv),
        compiler_params=_cparams("parallel", "parallel", "arbitrary"),
        name="diff_flash",
    )(q, k, v, lq1.reshape(1, -1), lk1.reshape(1, -1), lq2.reshape(1, -1), lk2.reshape(1, -1),
      subln.reshape(1, -1))


def _post_attn_kernel(o_ref, x_ref, wo_ref, g1_ref, b1_ref, wu_ref, wd_ref, g2_ref, b2_ref,
                      y_ref, x1_ref, x1b_ref, acc_ref):
    j = pl.program_id(1)

    @pl.when(j == 0)
    def _():
        x1 = _layernorm(DEEPNORM_ALPHA * x_ref[...] + _dot(o_ref[...], wo_ref[...]),
                        g1_ref[...], b1_ref[...])
        x1_ref[...] = x1
        x1b_ref[...] = x1.astype(BF16)

    hidden = jnp.square(jnp.maximum(_dot(x1b_ref[...], wu_ref[...]), 0.0))
    part = _dot(hidden.astype(BF16), wd_ref[...])

    @pl.when(j == 0)
    def _():
        acc_ref[...] = part

    @pl.when(j > 0)
    def _():
        acc_ref[...] += part

    @pl.when(j == pl.num_programs(1) - 1)
    def _():
        y = DEEPNORM_ALPHA * x1_ref[...] + acc_ref[...]
        y_ref[...] = _layernorm(y, g2_ref[...], b2_ref[...])


def _post_attn(o, x2d, w_o, g1, b1, w_up, w_down, g2, b2, tm, tf):
    t = x2d.shape[0]
    row = lambda i, j: (i, 0)
    full = lambda i, j: (0, 0)
    vec = pl.BlockSpec((1, D_MODEL), full)
    return pl.pallas_call(
        _post_attn_kernel,
        grid=(t // tm, D_FF // tf),
        in_specs=[
            pl.BlockSpec((tm, D_MODEL), row),
            pl.BlockSpec((tm, D_MODEL), row),
            pl.BlockSpec((D_MODEL, D_MODEL), full),
            vec, vec,
            pl.BlockSpec((D_MODEL, tf), lambda i, j: (0, j)),
            pl.BlockSpec((tf, D_MODEL), lambda i, j: (j, 0)),
            vec, vec,
        ],
        out_specs=pl.BlockSpec((tm, D_MODEL), row),
        out_shape=jax.ShapeDtypeStruct((t, D_MODEL), F32),
        scratch_shapes=[pltpu.VMEM((tm, D_MODEL), F32), pltpu.VMEM((tm, D_MODEL), BF16),
                        pltpu.VMEM((tm, D_MODEL), F32)],
        compiler_params=_cparams("parallel", "arbitrary"),
        name="post_attn",
    )(o, x2d, w_o.astype(BF16), g1.reshape(1, -1), b1.reshape(1, -1), w_up.astype(BF16),
      w_down.astype(BF16), g2.reshape(1, -1), b2.reshape(1, -1))


def _lane_tables(cos, sin, group):
    t, half = cos.shape
    rest = group - 2 * half
    c = jnp.concatenate([cos, cos, jnp.ones((t, rest), F32)], axis=1)
    s = jnp.concatenate([-sin, sin, jnp.zeros((t, rest), F32)], axis=1)
    reps = LANES // group
    return jnp.tile(c, (1, reps)), jnp.tile(s, (1, reps))


def kernel(x, positions, ln_mix_g, ln_mix_b, mla_w_dq, mla_q_norm, mla_w_uq, mla_w_dkv, mla_kv_norm, mla_w_ukv, mla_w_o, diff_w_qkv, diff_lambda_q1, diff_lambda_k1, diff_lambda_q2, diff_lambda_k2, diff_subln, diff_w_o, ln_ffn_g, ln_ffn_b, ffn_w_up, ffn_w_down):
    batch, seq, _ = x.shape
    x2d = x.reshape(batch * seq, D_MODEL)
    pos = positions.reshape(-1)

    mla_cos, mla_sin = _rope_tables(pos, MLA_ROPE // 2, MLA_ROPE_THETA)
    mla_c, mla_s = _lane_tables(mla_cos, mla_sin, LANES)
    diff_cos, diff_sin = _rope_tables(pos, DIFF_ROT_DIM // 2, DIFF_ROPE_THETA)
    diff_c, diff_s = _lane_tables(diff_cos, diff_sin, DIFF_HEAD_DIM)

    tm = 512
    tf = 1024
    q, k, v = _mla_proj(x2d, mla_c, mla_s, mla_w_dq[0], mla_q_norm[0], mla_w_uq[0],
                        mla_w_dkv[0], mla_kv_norm[0], mla_w_ukv[0], tm)
    o = _mla_flash(q, k, v, batch, seq, tq=512, tk=512)
    x2d = _post_attn(o, x2d, mla_w_o[0], ln_mix_g[0], ln_mix_b[0], ffn_w_up[0], ffn_w_down[0],
                     ln_ffn_g[0], ln_ffn_b[0], tm, tf)

    lambda_init = 0.8 - 0.6 * math.exp(-0.3 * 1)
    q, k, v = _diff_proj(x2d, diff_c, diff_s, diff_w_qkv[0], tm)
    o = _diff_flash(q, k, v, diff_lambda_q1[0], diff_lambda_k1[0], diff_lambda_q2[0],
                    diff_lambda_k2[0], diff_subln[0], lambda_init, batch, seq, tq=512, tk=512)
    x2d = _post_attn(o, x2d, diff_w_o[0], ln_mix_g[1], ln_mix_b[1], ffn_w_up[1], ffn_w_down[1],
                     ln_ffn_g[1], ln_ffn_b[1], tm, tf)
    return x2d.reshape(batch, seq, D_MODEL)
```

```python
import functools
import math

import jax
import jax.numpy as jnp
from jax import lax
from jax.experimental import pallas as pl
from jax.experimental.pallas import tpu as pltpu

D_MODEL = 1024
DEPTH = 2

MLA_NOPE = 128
MLA_ROPE = 64
MLA_V = 128
MLA_HEADS = 8
MLA_Q_LORA = 384
MLA_KV_LORA = 256
MLA_ROPE_THETA = 10000.0
MLA_QK_PAD = 256

DIFF_HEAD_DIM = 64
DIFF_HEADS = 8
DIFF_ROT_DIM = 16
DIFF_ROPE_THETA = 500000.0

D_FF = 4 * D_MODEL
DEEPNORM_ALPHA = (2 * DEPTH) ** 0.25
LN_EPS = 1e-5
RMS_EPS = 1e-6

LANES = 128
NEG_BIG = -1e30
LOG2_E = math.log2(math.e)
VMEM_LIMIT_BYTES = 48 * 1024 * 1024

F32 = jnp.float32
BF16 = jnp.bfloat16


def _cparams(*sem):
    return pltpu.CompilerParams(dimension_semantics=sem, vmem_limit_bytes=VMEM_LIMIT_BYTES)


def _dot(a, b):
    return jnp.dot(a, b, preferred_element_type=F32)


def _rms(x, g):
    return x * lax.rsqrt(jnp.mean(jnp.square(x), axis=-1, keepdims=True) + RMS_EPS) * g


def _layernorm(y, g, b):
    mu = jnp.mean(y, axis=-1, keepdims=True)
    d = y - mu
    var = jnp.mean(jnp.square(d), axis=-1, keepdims=True)
    return d * lax.rsqrt(var + LN_EPS) * g + b


def _rope_table_kernel(pos_ref, invf_ref, cos_ref, sin_ref):
    ang = pos_ref[...].astype(F32) * invf_ref[...]
    cos_ref[...] = jnp.cos(ang)
    sin_ref[...] = jnp.sin(ang)


def _rope_tables(positions, half, theta):
    t = positions.size
    rep = LANES // half
    inv_freq = theta ** (-jnp.arange(0, 2 * half, 2, dtype=F32) / (2 * half))
    pos_dense = jnp.repeat(positions.reshape(t // rep, rep), half, axis=1)
    invf = jnp.tile(inv_freq, rep).reshape(1, LANES)
    rows = t // rep
    cos, sin = pl.pallas_call(
        _rope_table_kernel,
        out_shape=[jax.ShapeDtypeStruct((rows, LANES), F32)] * 2,
        name="rope_table",
    )(pos_dense, invf)
    return cos.reshape(t, half), sin.reshape(t, half)


def _rotate_half_in_lanes(x, half, group, c_tab, s_tab):
    lane = lax.broadcasted_iota(jnp.int32, x.shape, 1)
    from_above = pltpu.roll(x, LANES - half, axis=1)
    from_below = pltpu.roll(x, half, axis=1)
    partner = jnp.where(lane % group < half, from_above, from_below)
    return x * c_tab + partner * s_tab


def _mla_proj_kernel(x_ref, wdq_ref, qn_ref, wuq_ref, wdkv_ref, kvn_ref, wukv_ref,
                     c_ref, s_ref, q_ref, k_ref, v_ref, *, scale):
    xb = x_ref[...].astype(BF16)
    c_tab = c_ref[...]
    s_tab = s_ref[...]
    rope = functools.partial(_rotate_half_in_lanes, half=MLA_ROPE // 2, group=LANES,
                             c_tab=c_tab, s_tab=s_tab)

    cq = _rms(_dot(xb, wdq_ref[...]), qn_ref[...]).astype(BF16)
    q = _dot(cq, wuq_ref[...])
    ckv = _dot(xb, wdkv_ref[...])
    c_kv = _rms(ckv[:, :MLA_KV_LORA], kvn_ref[...]).astype(BF16)
    k_pe = rope(ckv[:, MLA_KV_LORA:]).astype(BF16)
    kv = _dot(c_kv, wukv_ref[...])

    for h in range(MLA_HEADS):
        base = h * MLA_QK_PAD
        q_ref[:, base:base + LANES] = (q[:, base:base + LANES] * scale).astype(BF16)
        q_ref[:, base + LANES:base + 2 * LANES] = (rope(q[:, base + LANES:base + 2 * LANES]) * scale).astype(BF16)
        k_ref[:, base:base + LANES] = kv[:, base:base + LANES].astype(BF16)
        k_ref[:, base + LANES:base + 2 * LANES] = k_pe
        v_ref[:, h * MLA_V:(h + 1) * MLA_V] = kv[:, base + LANES:base + 2 * LANES].astype(BF16)


def _mla_proj(x2d, c_tab, s_tab, w_dq, q_norm, w_uq, w_dkv, kv_norm, w_ukv, tm):
    t = x2d.shape[0]
    h = MLA_HEADS
    pad = MLA_QK_PAD - MLA_NOPE - MLA_ROPE
    wuq = jnp.pad(w_uq.reshape(MLA_Q_LORA, h, MLA_NOPE + MLA_ROPE), ((0, 0), (0, 0), (0, pad)))
    wuq = wuq.reshape(MLA_Q_LORA, h * MLA_QK_PAD).astype(BF16)
    wdkv = jnp.pad(w_dkv, ((0, 0), (0, pad))).astype(BF16)
    row = lambda i: (i, 0)
    full = lambda i: (0, 0)
    return pl.pallas_call(
        functools.partial(_mla_proj_kernel, scale=LOG2_E * (MLA_NOPE + MLA_ROPE) ** -0.5),
        grid=(t // tm,),
        in_specs=[
            pl.BlockSpec((tm, D_MODEL), row),
            pl.BlockSpec((D_MODEL, MLA_Q_LORA), full),
            pl.BlockSpec((1, MLA_Q_LORA), full),
            pl.BlockSpec((MLA_Q_LORA, h * MLA_QK_PAD), full),
            pl.BlockSpec((D_MODEL, MLA_KV_LORA + LANES), full),
            pl.BlockSpec((1, MLA_KV_LORA), full),
            pl.BlockSpec((MLA_KV_LORA, h * (MLA_NOPE + MLA_V)), full),
            pl.BlockSpec((tm, LANES), row),
            pl.BlockSpec((tm, LANES), row),
        ],
        out_specs=[
            pl.BlockSpec((tm, h * MLA_QK_PAD), row),
            pl.BlockSpec((tm, h * MLA_QK_PAD), row),
            pl.BlockSpec((tm, h * MLA_V), row),
        ],
        out_shape=[
            jax.ShapeDtypeStruct((t, h * MLA_QK_PAD), BF16),
            jax.ShapeDtypeStruct((t, h * MLA_QK_PAD), BF16),
            jax.ShapeDtypeStruct((t, h * MLA_V), BF16),
        ],
        compiler_params=_cparams("parallel"),
        name="mla_proj",
    )(x2d, w_dq.astype(BF16), q_norm.reshape(1, -1), wuq, wdkv, kv_norm.reshape(1, -1),
      w_ukv.astype(BF16), c_tab, s_tab)


def _diff_proj_kernel(x_ref, w_ref, c_ref, s_ref, q_ref, k_ref, v_ref, *, scale):
    xb = x_ref[...].astype(BF16)
    c_tab = c_ref[...]
    s_tab = s_ref[...]
    rope = functools.partial(_rotate_half_in_lanes, half=DIFF_ROT_DIM // 2, group=DIFF_HEAD_DIM,
                             c_tab=c_tab, s_tab=s_tab)
    qkv = _dot(xb, w_ref[...])
    for j in range(D_MODEL // LANES):
        lo, hi = j * LANES, (j + 1) * LANES
        q_ref[:, lo:hi] = (rope(qkv[:, lo:hi]) * scale).astype(BF16)
        k_ref[:, lo:hi] = rope(qkv[:, D_MODEL + lo:D_MODEL + hi]).astype(BF16)
    v_ref[...] = qkv[:, 2 * D_MODEL:].astype(BF16)


def _diff_proj(x2d, c_tab, s_tab, w_qkv, tm):
    t = x2d.shape[0]
    row = lambda i: (i, 0)
    full = lambda i: (0, 0)
    return pl.pallas_call(
        functools.partial(_diff_proj_kernel, scale=LOG2_E * DIFF_HEAD_DIM ** -0.5),
        grid=(t // tm,),
        in_specs=[
            pl.BlockSpec((tm, D_MODEL), row),
            pl.BlockSpec((D_MODEL, 3 * D_MODEL), full),
            pl.BlockSpec((tm, LANES), row),
            pl.BlockSpec((tm, LANES), row),
        ],
        out_specs=[pl.BlockSpec((tm, D_MODEL), row)] * 3,
        out_shape=[jax.ShapeDtypeStruct((t, D_MODEL), BF16)] * 3,
        compiler_params=_cparams("parallel"),
        name="diff_proj",
    )(x2d, w_qkv.astype(BF16), c_tab, s_tab)


def _transpose_v(v_ref, vt_ref, tk):
    for j in range(vt_ref.shape[0]):
        vt_ref[j] = v_ref[j * tk:(j + 1) * tk, :].astype(F32).T.astype(vt_ref.dtype)


def _flash_loop(qs_ref, k_ref, vt_ref, s_refs, m_ref, l_ref, acc_ref, q_start, tq, tk):
    assert tk % tq == 0
    rows = qs_ref.shape[0]
    s0_ref, s1_ref = s_refs
    m_ref[...] = jnp.full(m_ref.shape, NEG_BIG, F32)
    l_ref[...] = jnp.zeros(l_ref.shape, F32)
    acc_ref[...] = jnp.zeros(acc_ref.shape, F32)

    def scores(j, s_ref):
        start = pl.multiple_of(j * tk, tk)
        k = k_ref[pl.ds(start, tk), :]
        s_ref[...] = lax.dot_general(k, qs_ref[...], (((1,), (1,)), ((), ())),
                                     preferred_element_type=F32)

    def consume(j, s_ref, masked):
        if masked:
            k_pos = j * tk + lax.broadcasted_iota(jnp.int32, (tk, rows), 0)
            q_pos = q_start + lax.broadcasted_iota(jnp.int32, (tk, rows), 1) % tq
            s_ref[...] = jnp.where(k_pos <= q_pos, s_ref[...], NEG_BIG)
        m_prev = m_ref[...]
        m_new = jnp.maximum(m_prev, jnp.max(s_ref[...], axis=0, keepdims=True))
        alpha = jnp.exp2(m_prev - m_new)
        p = jnp.exp2(s_ref[...] - m_new)
        l_ref[...] = alpha * l_ref[...] + jnp.sum(p, axis=0, keepdims=True)
        acc_ref[...] = alpha * acc_ref[...] + _dot(vt_ref[j], p.astype(BF16))
        m_ref[...] = m_new

    n_full = q_start // tk

    def pair(i, carry):
        j = 2 * i
        scores(j + 1, s1_ref)
        consume(j, s0_ref, masked=False)
        scores(j + 2, s0_ref)
        consume(j + 1, s1_ref, masked=False)
        return carry

    scores(0, s0_ref)
    lax.fori_loop(0, n_full // 2, pair, 0)

    @pl.when(n_full % 2 == 1)
    def _():
        scores(n_full, s1_ref)
        consume(n_full - 1, s0_ref, masked=False)
        consume(n_full, s1_ref, masked=True)

    @pl.when(n_full % 2 == 0)
    def _():
        consume(n_full, s0_ref, masked=True)


def _mla_flash_kernel(q_ref, k_ref, v_ref, o_ref, vt_ref, s0_ref, s1_ref, m_ref, l_ref, acc_ref,
                      *, tq, tk):
    @pl.when(pl.program_id(2) == 0)
    def _():
        _transpose_v(v_ref, vt_ref, tk)

    q_start = pl.program_id(2) * tq
    _flash_loop(q_ref, k_ref, vt_ref, (s0_ref, s1_ref), m_ref, l_ref, acc_ref, q_start, tq, tk)
    o_ref[...] = (acc_ref[...] / l_ref[...]).T.astype(o_ref.dtype)


def _diff_flash_kernel(q_ref, k_ref, v_ref, lq1_ref, lk1_ref, lq2_ref, lk2_ref, subln_ref,
                       o_ref, qs_ref, vt_ref, s0_ref, s1_ref, m_ref, l_ref, acc_ref,
                       *, tq, tk, lambda_init):
    @pl.when(pl.program_id(2) == 0)
    def _():
        _transpose_v(v_ref, vt_ref, tk)

    q_start = pl.program_id(2) * tq
    q = q_ref[...]
    lane = lax.broadcasted_iota(jnp.int32, q.shape, 1)
    zero = jnp.zeros_like(q)
    qs_ref[:tq, :] = jnp.where(lane < DIFF_HEAD_DIM, q, zero)
    qs_ref[tq:, :] = jnp.where(lane < DIFF_HEAD_DIM, zero, q)
    _flash_loop(qs_ref, k_ref, vt_ref, (s0_ref, s1_ref), m_ref, l_ref, acc_ref, q_start, tq, tk)
    lam = (jnp.exp(jnp.sum(lq1_ref[...] * lk1_ref[...]))
           - jnp.exp(jnp.sum(lq2_ref[...] * lk2_ref[...])) + lambda_init)
    o1_t = acc_ref[:, :tq] / l_ref[:, :tq]
    o2_t = acc_ref[:, tq:] / l_ref[:, tq:]
    o = _rms((o1_t - lam * o2_t).T, subln_ref[...]) * (1.0 - lambda_init)
    o_ref[...] = o.astype(o_ref.dtype)


def _flash_scratch(seq, rows, tk, dv):
    return [pltpu.VMEM((seq // tk, dv, tk), BF16),
            pltpu.VMEM((tk, rows), F32), pltpu.VMEM((tk, rows), F32),
            pltpu.VMEM((1, rows), F32), pltpu.VMEM((1, rows), F32),
            pltpu.VMEM((dv, rows), F32)]


def _flash_specs(seq, n_q, tq, dk, dv):
    q_spec = pl.BlockSpec((tq, dk), lambda b, h, i: (b * n_q + i, h))
    k_spec = pl.BlockSpec((seq, dk), lambda b, h, i: (b, h))
    v_spec = pl.BlockSpec((seq, dv), lambda b, h, i: (b, h))
    o_spec = pl.BlockSpec((tq, dv), lambda b, h, i: (b * n_q + i, h))
    return q_spec, k_spec, v_spec, o_spec


def _mla_flash(q, k, v, batch, seq, tq, tk):
    n_q = seq // tq
    q_spec, k_spec, v_spec, o_spec = _flash_specs(seq, n_q, tq, MLA_QK_PAD, MLA_V)
    return pl.pallas_call(
        functools.partial(_mla_flash_kernel, tq=tq, tk=tk),
        grid=(batch, MLA_HEADS, n_q),
        in_specs=[q_spec, k_spec, v_spec],
        out_specs=o_spec,
        out_shape=jax.ShapeDtypeStruct((batch * seq, MLA_HEADS * MLA_V), BF16),
        scratch_shapes=_flash_scratch(seq, tq, tk, MLA_V),
        compiler_params=_cparams("parallel", "parallel", "arbitrary"),
        name="mla_flash",
    )(q, k, v)


def _diff_flash(q, k, v, lq1, lk1, lq2, lk2, subln, lambda_init, batch, seq, tq, tk):
    n_q = seq // tq
    dv = 2 * DIFF_HEAD_DIM
    q_spec, k_spec, v_spec, o_spec = _flash_specs(seq, n_q, tq, dv, dv)
    vec = lambda n: pl.BlockSpec((1, n), lambda b, h, i: (0, 0))
    return pl.pallas_call(
        functools.partial(_diff_flash_kernel, tq=tq, tk=tk, lambda_init=lambda_init),
        grid=(batch, DIFF_HEADS, n_q),
        in_specs=[q_spec, k_spec, v_spec, vec(DIFF_HEAD_DIM), vec(DIFF_HEAD_DIM),
                  vec(DIFF_HEAD_DIM), vec(DIFF_HEAD_DIM), vec(dv)],
        out_specs=o_spec,
        out_shape=jax.ShapeDtypeStruct((batch * seq, DIFF_HEADS * dv), BF16),
        scratch_shapes=[pltpu.VMEM((2 * tq, dv), BF16)] + _flash_scratch(seq, 2 * tq, tk, dv),
        compiler_params=_cparams("parallel", "parallel", "arbitrary"),
        name="diff_flash",
    )(q, k, v, lq1.reshape(1, -1), lk1.reshape(1, -1), lq2.reshape(1, -1), lk2.reshape(1, -1),
      subln.reshape(1, -1))


def _post_attn_kernel(o_ref, x_ref, wo_ref, g1_ref, b1_ref, wu_ref, wd_ref, g2_ref, b2_ref,
                      y_ref, x1_ref, x1b_ref, acc_ref):
    j = pl.program_id(1)

    @pl.when(j == 0)
    def _():
        x1 = _layernorm(DEEPNORM_ALPHA * x_ref[...] + _dot(o_ref[...], wo_ref[...]),
                        g1_ref[...], b1_ref[...])
        x1_ref[...] = x1
        x1b_ref[...] = x1.astype(BF16)

    hidden = jnp.square(jnp.maximum(_dot(x1b_ref[...], wu_ref[...]), 0.0))
    part = _dot(hidden.astype(BF16), wd_ref[...])

    @pl.when(j == 0)
    def _():
        acc_ref[...] = part

    @pl.when(j > 0)
    def _():
        acc_ref[...] += part

    @pl.when(j == pl.num_programs(1) - 1)
    def _():
        y = DEEPNORM_ALPHA * x1_ref[...] + acc_ref[...]
        y_ref[...] = _layernorm(y, g2_ref[...], b2_ref[...])


def _post_attn(o, x2d, w_o, g1, b1, w_up, w_down, g2, b2, tm, tf):
    t = x2d.shape[0]
    row = lambda i, j: (i, 0)
    full = lambda i, j: (0, 0)
    vec = pl.BlockSpec((1, D_MODEL), full)
    return pl.pallas_call(
        _post_attn_kernel,
        grid=(t // tm, D_FF // tf),
        in_specs=[
            pl.BlockSpec((tm, D_MODEL), row),
            pl.BlockSpec((tm, D_MODEL), row),
            pl.BlockSpec((D_MODEL, D_MODEL), full),
            vec, vec,
            pl.BlockSpec((D_MODEL, tf), lambda i, j: (0, j)),
            pl.BlockSpec((tf, D_MODEL), lambda i, j: (j, 0)),
            vec, vec,
        ],
        out_specs=pl.BlockSpec((tm, D_MODEL), row),
        out_shape=jax.ShapeDtypeStruct((t, D_MODEL), F32),
        scratch_shapes=[pltpu.VMEM((tm, D_MODEL), F32), pltpu.VMEM((tm, D_MODEL), BF16),
                        pltpu.VMEM((tm, D_MODEL), F32)],
        compiler_params=_cparams("parallel", "arbitrary"),
        name="post_attn",
    )(o, x2d, w_o.astype(BF16), g1.reshape(1, -1), b1.reshape(1, -1), w_up.astype(BF16),
      w_down.astype(BF16), g2.reshape(1, -1), b2.reshape(1, -1))


def _lane_tables(cos, sin, group):
    t, half = cos.shape
    rest = group - 2 * half
    c = jnp.concatenate([cos, cos, jnp.ones((t, rest), F32)], axis=1)
    s = jnp.concatenate([-sin, sin, jnp.zeros((t, rest), F32)], axis=1)
    reps = LANES // group
    return jnp.tile(c, (1, reps)), jnp.tile(s, (1, reps))


def kernel(x, positions, ln_mix_g, ln_mix_b, mla_w_dq, mla_q_norm, mla_w_uq, mla_w_dkv, mla_kv_norm, mla_w_ukv, mla_w_o, diff_w_qkv, diff_lambda_q1, diff_lambda_k1, diff_lambda_q2, diff_lambda_k2, diff_subln, diff_w_o, ln_ffn_g, ln_ffn_b, ffn_w_up, ffn_w_down):
    batch, seq, _ = x.shape
    x2d = x.reshape(batch * seq, D_MODEL)
    pos = positions.reshape(-1)

    mla_cos, mla_sin = _rope_tables(pos, MLA_ROPE // 2, MLA_ROPE_THETA)
    mla_c, mla_s = _lane_tables(mla_cos, mla_sin, LANES)
    diff_cos, diff_sin = _rope_tables(pos, DIFF_ROT_DIM // 2, DIFF_ROPE_THETA)
    diff_c, diff_s = _lane_tables(diff_cos, diff_sin, DIFF_HEAD_DIM)

    tm = 512
    tf = 1024
    q, k, v = _mla_proj(x2d, mla_c, mla_s, mla_w_dq[0], mla_q_norm[0], mla_w_uq[0],
                        mla_w_dkv[0], mla_kv_norm[0], mla_w_ukv[0], tm)
    o = _mla_flash(q, k, v, batch, seq, tq=512, tk=512)
    x2d = _post_attn(o, x2d, mla_w_o[0], ln_mix_g[0], ln_mix_b[0], ffn_w_up[0], ffn_w_down[0],
                     ln_ffn_g[0], ln_ffn_b[0], tm, tf)

    lambda_init = 0.8 - 0.6 * math.exp(-0.3 * 1)
    q, k, v = _diff_proj(x2d, diff_c, diff_s, diff_w_qkv[0], tm)
    o = _diff_flash(q, k, v, diff_lambda_q1[0], diff_lambda_k1[0], diff_lambda_q2[0],
                    diff_lambda_k2[0], diff_subln[0], lambda_init, batch, seq, tq=512, tk=512)
    x2d = _post_attn(o, x2d, diff_w_o[0], ln_mix_g[1], ln_mix_b[1], ffn_w_up[1], ffn_w_down[1],
                     ln_ffn_g[1], ln_ffn_b[1], tm, tf)
    return x2d.reshape(batch, seq, D_MODEL)
```

```python
import functools
import math

import jax
import jax.numpy as jnp
from jax import lax
from jax.experimental import pallas as pl
from jax.experimental.pallas import tpu as pltpu

D_MODEL = 1024
DEPTH = 2

MLA_NOPE = 128
MLA_ROPE = 64
MLA_V = 128
MLA_HEADS = 8
MLA_Q_LORA = 384
MLA_KV_LORA = 256
MLA_ROPE_THETA = 10000.0
MLA_QK_PAD = 256

DIFF_HEAD_DIM = 64
DIFF_HEADS = 8
DIFF_ROT_DIM = 16
DIFF_ROPE_THETA = 500000.0

D_FF = 4 * D_MODEL
DEEPNORM_ALPHA = (2 * DEPTH) ** 0.25
LN_EPS = 1e-5
RMS_EPS = 1e-6

LANES = 128
NEG_BIG = -1e30
LOG2_E = math.log2(math.e)
VMEM_LIMIT_BYTES = 48 * 1024 * 1024

F32 = jnp.float32
BF16 = jnp.bfloat16


def _cparams(*sem):
    return pltpu.CompilerParams(dimension_semantics=sem, vmem_limit_bytes=VMEM_LIMIT_BYTES)


def _dot(a, b):
    return jnp.dot(a, b, preferred_element_type=F32)


def _rms(x, g):
    return x * lax.rsqrt(jnp.mean(jnp.square(x), axis=-1, keepdims=True) + RMS_EPS) * g


def _layernorm(y, g, b):
    mu = jnp.mean(y, axis=-1, keepdims=True)
    d = y - mu
    var = jnp.mean(jnp.square(d), axis=-1, keepdims=True)
    return d * lax.rsqrt(var + LN_EPS) * g + b


def _rope_table_kernel(pos_ref, invf_ref, cos_ref, sin_ref):
    ang = pos_ref[...].astype(F32) * invf_ref[...]
    cos_ref[...] = jnp.cos(ang)
    sin_ref[...] = jnp.sin(ang)


def _rope_tables(positions, half, theta):
    t = positions.size
    rep = LANES // half
    inv_freq = theta ** (-jnp.arange(0, 2 * half, 2, dtype=F32) / (2 * half))
    pos_dense = jnp.repeat(positions.reshape(t // rep, rep), half, axis=1)
    invf = jnp.tile(inv_freq, rep).reshape(1, LANES)
    rows = t // rep
    cos, sin = pl.pallas_call(
        _rope_table_kernel,
        out_shape=[jax.ShapeDtypeStruct((rows, LANES), F32)] * 2,
        name="rope_table",
    )(pos_dense, invf)
    return cos.reshape(t, half), sin.reshape(t, half)


def _rotate_half_in_lanes(x, half, group, c_tab, s_tab):
    lane = lax.broadcasted_iota(jnp.int32, x.shape, 1)
    from_above = pltpu.roll(x, LANES - half, axis=1)
    from_below = pltpu.roll(x, half, axis=1)
    partner = jnp.where(lane % group < half, from_above, from_below)
    return x * c_tab + partner * s_tab


def _mla_proj_kernel(x_ref, wdq_ref, qn_ref, wuq_ref, wdkv_ref, kvn_ref, wukv_ref,
                     c_ref, s_ref, q_ref, k_ref, v_ref, *, scale):
    xb = x_ref[...].astype(BF16)
    c_tab = c_ref[...]
    s_tab = s_ref[...]
    rope = functools.partial(_rotate_half_in_lanes, half=MLA_ROPE // 2, group=LANES,
                             c_tab=c_tab, s_tab=s_tab)

    cq = _rms(_dot(xb, wdq_ref[...]), qn_ref[...]).astype(BF16)
    q = _dot(cq, wuq_ref[...])
    ckv = _dot(xb, wdkv_ref[...])
    c_kv = _rms(ckv[:, :MLA_KV_LORA], kvn_ref[...]).astype(BF16)
    k_pe = rope(ckv[:, MLA_KV_LORA:]).astype(BF16)
    kv = _dot(c_kv, wukv_ref[...])

    for h in range(MLA_HEADS):
        base = h * MLA_QK_PAD
        q_ref[:, base:base + LANES] = (q[:, base:base + LANES] * scale).astype(BF16)
        q_ref[:, base + LANES:base + 2 * LANES] = (rope(q[:, base + LANES:base + 2 * LANES]) * scale).astype(BF16)
        k_ref[:, base:base + LANES] = kv[:, base:base + LANES].astype(BF16)
        k_ref[:, base + LANES:base + 2 * LANES] = k_pe
        v_ref[:, h * MLA_V:(h + 1) * MLA_V] = kv[:, base + LANES:base + 2 * LANES].astype(BF16)


def _mla_proj(x2d, c_tab, s_tab, w_dq, q_norm, w_uq, w_dkv, kv_norm, w_ukv, tm):
    t = x2d.shape[0]
    h = MLA_HEADS
    pad = MLA_QK_PAD - MLA_NOPE - MLA_ROPE
    wuq = jnp.pad(w_uq.reshape(MLA_Q_LORA, h, MLA_NOPE + MLA_ROPE), ((0, 0), (0, 0), (0, pad)))
    wuq = wuq.reshape(MLA_Q_LORA, h * MLA_QK_PAD).astype(BF16)
    wdkv = jnp.pad(w_dkv, ((0, 0), (0, pad))).astype(BF16)
    row = lambda i: (i, 0)
    full = lambda i: (0, 0)
    return pl.pallas_call(
        functools.partial(_mla_proj_kernel, scale=LOG2_E * (MLA_NOPE + MLA_ROPE) ** -0.5),
        grid=(t // tm,),
        in_specs=[
            pl.BlockSpec((tm, D_MODEL), row),
            pl.BlockSpec((D_MODEL, MLA_Q_LORA), full),
            pl.BlockSpec((1, MLA_Q_LORA), full),
            pl.BlockSpec((MLA_Q_LORA, h * MLA_QK_PAD), full),
            pl.BlockSpec((D_MODEL, MLA_KV_LORA + LANES), full),
            pl.BlockSpec((1, MLA_KV_LORA), full),
            pl.BlockSpec((MLA_KV_LORA, h * (MLA_NOPE + MLA_V)), full),
            pl.BlockSpec((tm, LANES), row),
            pl.BlockSpec((tm, LANES), row),
        ],
        out_specs=[
            pl.BlockSpec((tm, h * MLA_QK_PAD), row),
            pl.BlockSpec((tm, h * MLA_QK_PAD), row),
            pl.BlockSpec((tm, h * MLA_V), row),
        ],
        out_shape=[
            jax.ShapeDtypeStruct((t, h * MLA_QK_PAD), BF16),
            jax.ShapeDtypeStruct((t, h * MLA_QK_PAD), BF16),
            jax.ShapeDtypeStruct((t, h * MLA_V), BF16),
        ],
        compiler_params=_cparams("parallel"),
        name="mla_proj",
    )(x2d, w_dq.astype(BF16), q_norm.reshape(1, -1), wuq, wdkv, kv_norm.reshape(1, -1),
      w_ukv.astype(BF16), c_tab, s_tab)


def _diff_proj_kernel(x_ref, w_ref, c_ref, s_ref, q_ref, k_ref, v_ref, *, scale):
    xb = x_ref[...].astype(BF16)
    c_tab = c_ref[...]
    s_tab = s_ref[...]
    rope = functools.partial(_rotate_half_in_lanes, half=DIFF_ROT_DIM // 2, group=DIFF_HEAD_DIM,
                             c_tab=c_tab, s_tab=s_tab)
    qkv = _dot(xb, w_ref[...])
    for j in range(D_MODEL // LANES):
        lo, hi = j * LANES, (j + 1) * LANES
        q_ref[:, lo:hi] = (rope(qkv[:, lo:hi]) * scale).astype(BF16)
        k_ref[:, lo:hi] = rope(qkv[:, D_MODEL + lo:D_MODEL + hi]).astype(BF16)
    v_ref[...] = qkv[:, 2 * D_MODEL:].astype(BF16)


def _diff_proj(x2d, c_tab, s_tab, w_qkv, tm):
    t = x2d.shape[0]
    row = lambda i: (i, 0)
    full = lambda i: (0, 0)
    return pl.pallas_call(
        functools.partial(_diff_proj_kernel, scale=LOG2_E * DIFF_HEAD_DIM ** -0.5),
        grid=(t // tm,),
        in_specs=[
            pl.BlockSpec((tm, D_MODEL), row),
            pl.BlockSpec((D_MODEL, 3 * D_MODEL), full),
            pl.BlockSpec((tm, LANES), row),
            pl.BlockSpec((tm, LANES), row),
        ],
        out_specs=[pl.BlockSpec((tm, D_MODEL), row)] * 3,
        out_shape=[jax.ShapeDtypeStruct((t, D_MODEL), BF16)] * 3,
        compiler_params=_cparams("parallel"),
        name="diff_proj",
    )(x2d, w_qkv.astype(BF16), c_tab, s_tab)


def _transpose_v(v_ref, vt_ref, tk):
    for j in range(vt_ref.shape[0]):
        vt_ref[j] = v_ref[j * tk:(j + 1) * tk, :].astype(F32).T.astype(vt_ref.dtype)


def _flash_head(q_block, k_ref, vt_ref, s_refs, m_ref, l_ref, acc_ref, finish, n_q, tq, fold):
    rows = fold * tq
    steps = [(i, j) for i in range(n_q) for j in range(i + 1)]

    def scores(t):
        i, j = steps[t]
        k = k_ref[j * tq:(j + 1) * tq, :]
        s_refs[t % 2][...] = lax.dot_general(k, q_block(i, j == 0), (((1,), (1,)), ((), ())),
                                             preferred_element_type=F32)

    def consume(t):
        i, j = steps[t]
        s_ref = s_refs[t % 2]
        par = i % 2
        if j == i:
            k_pos = lax.broadcasted_iota(jnp.int32, (tq, rows), 0)
            q_pos = lax.broadcasted_iota(jnp.int32, (tq, rows), 1) % tq
            s_ref[...] = jnp.where(k_pos <= q_pos, s_ref[...], NEG_BIG)
        m_blk = jnp.max(s_ref[...], axis=0, keepdims=True)
        if j == 0:
            m_new = m_blk
            p = jnp.exp2(s_ref[...] - m_new)
            l_ref[par] = jnp.sum(p, axis=0, keepdims=True)
            acc_ref[par] = _dot(vt_ref[j], p.astype(BF16))
        else:
            m_prev = m_ref[par]
            m_new = jnp.maximum(m_prev, m_blk)
            alpha = jnp.exp2(m_prev - m_new)
            p = jnp.exp2(s_ref[...] - m_new)
            l_ref[par] = alpha * l_ref[par] + jnp.sum(p, axis=0, keepdims=True)
            acc_ref[par] = alpha * acc_ref[par] + _dot(vt_ref[j], p.astype(BF16))
        m_ref[par] = m_new
        if j == i:
            finish(i, acc_ref[par], l_ref[par])

    scores(0)
    for t in range(len(steps)):
        if t + 1 < len(steps):
            scores(t + 1)
        consume(t)


def _mla_flash_kernel(q_ref, k_ref, v_ref, o_ref, vt_ref, s0_ref, s1_ref, m_ref, l_ref, acc_ref,
                      *, n_q, tq):
    _transpose_v(v_ref, vt_ref, tq)

    def q_block(i, first):
        del first
        return q_ref[i * tq:(i + 1) * tq, :]

    def finish(i, acc, l):
        o_ref[i * tq:(i + 1) * tq, :] = (acc / l).T.astype(o_ref.dtype)

    _flash_head(q_block, k_ref, vt_ref, (s0_ref, s1_ref), m_ref, l_ref, acc_ref, finish,
                n_q, tq, fold=1)


def _diff_flash_kernel(q_ref, k_ref, v_ref, lq1_ref, lk1_ref, lq2_ref, lk2_ref, subln_ref,
                       o_ref, qs_ref, vt_ref, s0_ref, s1_ref, m_ref, l_ref, acc_ref,
                       *, n_q, tq, lambda_init):
    _transpose_v(v_ref, vt_ref, tq)
    lam = (jnp.exp(jnp.sum(lq1_ref[...] * lk1_ref[...]))
           - jnp.exp(jnp.sum(lq2_ref[...] * lk2_ref[...])) + lambda_init)
    subln = subln_ref[...]

    def q_block(i, first):
        if first:
            q = q_ref[i * tq:(i + 1) * tq, :]
            lane = lax.broadcasted_iota(jnp.int32, q.shape, 1)
            zero = jnp.zeros_like(q)
            qs_ref[i % 2, :tq, :] = jnp.where(lane < DIFF_HEAD_DIM, q, zero)
            qs_ref[i % 2, tq:, :] = jnp.where(lane < DIFF_HEAD_DIM, zero, q)
        return qs_ref[i % 2]

    def finish(i, acc, l):
        o_t = acc[:, :tq] / l[:, :tq] - lam * (acc[:, tq:] / l[:, tq:])
        o = _rms(o_t.T, subln) * (1.0 - lambda_init)
        o_ref[i * tq:(i + 1) * tq, :] = o.astype(o_ref.dtype)

    _flash_head(q_block, k_ref, vt_ref, (s0_ref, s1_ref), m_ref, l_ref, acc_ref, finish,
                n_q, tq, fold=2)


def _flash_scratch(seq, rows, tq, dv):
    return [pltpu.VMEM((seq // tq, dv, tq), BF16),
            pltpu.VMEM((tq, rows), F32), pltpu.VMEM((tq, rows), F32),
            pltpu.VMEM((2, 1, rows), F32), pltpu.VMEM((2, 1, rows), F32),
            pltpu.VMEM((2, dv, rows), F32)]


def _flash_specs(seq, dk, dv):
    head = lambda width: pl.BlockSpec((seq, width), lambda b, h: (b, h))
    return head(dk), head(dk), head(dv), head(dv)


def _mla_flash(q, k, v, batch, seq, tq):
    q_spec, k_spec, v_spec, o_spec = _flash_specs(seq, MLA_QK_PAD, MLA_V)
    return pl.pallas_call(
        functools.partial(_mla_flash_kernel, n_q=seq // tq, tq=tq),
        grid=(batch, MLA_HEADS),
        in_specs=[q_spec, k_spec, v_spec],
        out_specs=o_spec,
        out_shape=jax.ShapeDtypeStruct((batch * seq, MLA_HEADS * MLA_V), BF16),
        scratch_shapes=_flash_scratch(seq, tq, tq, MLA_V),
        compiler_params=_cparams("parallel", "parallel"),
        name="mla_flash",
    )(q, k, v)


def _diff_flash(q, k, v, lq1, lk1, lq2, lk2, subln, lambda_init, batch, seq, tq):
    dv = 2 * DIFF_HEAD_DIM
    q_spec, k_spec, v_spec, o_spec = _flash_specs(seq, dv, dv)
    vec = lambda n: pl.BlockSpec((1, n), lambda b, h: (0, 0))
    return pl.pallas_call(
        functools.partial(_diff_flash_kernel, n_q=seq // tq, tq=tq, lambda_init=lambda_init),
        grid=(batch, DIFF_HEADS),
        in_specs=[q_spec, k_spec, v_spec, vec(DIFF_HEAD_DIM), vec(DIFF_HEAD_DIM),
                  vec(DIFF_HEAD_DIM), vec(DIFF_HEAD_DIM), vec(dv)],
        out_specs=o_spec,
        out_shape=jax.ShapeDtypeStruct((batch * seq, DIFF_HEADS * dv), BF16),
        scratch_shapes=[pltpu.VMEM((2, 2 * tq, dv), BF16)] + _flash_scratch(seq, 2 * tq, tq, dv),
        compiler_params=_cparams("parallel", "parallel"),
        name="diff_flash",
    )(q, k, v, lq1.reshape(1, -1), lk1.reshape(1, -1), lq2.reshape(1, -1), lk2.reshape(1, -1),
      subln.reshape(1, -1))


def _post_attn_kernel(o_ref, x_ref, wo_ref, g1_ref, b1_ref, wu_ref, wd_ref, g2_ref, b2_ref,
                      y_ref, x1_ref, x1b_ref, acc_ref):
    j = pl.program_id(1)

    @pl.when(j == 0)
    def _():
        x1 = _layernorm(DEEPNORM_ALPHA * x_ref[...] + _dot(o_ref[...], wo_ref[...]),
                        g1_ref[...], b1_ref[...])
        x1_ref[...] = x1
        x1b_ref[...] = x1.astype(BF16)

    hidden = jnp.square(jnp.maximum(_dot(x1b_ref[...], wu_ref[...]), 0.0))
    part = _dot(hidden.astype(BF16), wd_ref[...])

    @pl.when(j == 0)
    def _():
        acc_ref[...] = part

    @pl.when(j > 0)
    def _():
        acc_ref[...] += part

    @pl.when(j == pl.num_programs(1) - 1)
    def _():
        y = DEEPNORM_ALPHA * x1_ref[...] + acc_ref[...]
        y_ref[...] = _layernorm(y, g2_ref[...], b2_ref[...])


def _post_attn(o, x2d, w_o, g1, b1, w_up, w_down, g2, b2, tm, tf):
    t = x2d.shape[0]
    row = lambda i, j: (i, 0)
    full = lambda i, j: (0, 0)
    vec = pl.BlockSpec((1, D_MODEL), full)
    return pl.pallas_call(
        _post_attn_kernel,
        grid=(t // tm, D_FF // tf),
        in_specs=[
            pl.BlockSpec((tm, D_MODEL), row),
            pl.BlockSpec((tm, D_MODEL), row),
            pl.BlockSpec((D_MODEL, D_MODEL), full),
            vec, vec,
            pl.BlockSpec((D_MODEL, tf), lambda i, j: (0, j)),
            pl.BlockSpec((tf, D_MODEL), lambda i, j: (j, 0)),
            vec, vec,
        ],
        out_specs=pl.BlockSpec((tm, D_MODEL), row),
        out_shape=jax.ShapeDtypeStruct((t, D_MODEL), F32),
        scratch_shapes=[pltpu.VMEM((tm, D_MODEL), F32), pltpu.VMEM((tm, D_MODEL), BF16),
                        pltpu.VMEM((tm, D_MODEL), F32)],
        compiler_params=_cparams("parallel", "arbitrary"),
        name="post_attn",
    )(o, x2d, w_o.astype(BF16), g1.reshape(1, -1), b1.reshape(1, -1), w_up.astype(BF16),
      w_down.astype(BF16), g2.reshape(1, -1), b2.reshape(1, -1))


def _lane_tables(cos, sin, group):
    t, half = cos.shape
    rest = group - 2 * half
    c = jnp.concatenate([cos, cos, jnp.ones((t, rest), F32)], axis=1)
    s = jnp.concatenate([-sin, sin, jnp.zeros((t, rest), F32)], axis=1)
    reps = LANES // group
    return jnp.tile(c, (1, reps)), jnp.tile(s, (1, reps))


def kernel(x, positions, ln_mix_g, ln_mix_b, mla_w_dq, mla_q_norm, mla_w_uq, mla_w_dkv, mla_kv_norm, mla_w_ukv, mla_w_o, diff_w_qkv, diff_lambda_q1, diff_lambda_k1, diff_lambda_q2, diff_lambda_k2, diff_subln, diff_w_o, ln_ffn_g, ln_ffn_b, ffn_w_up, ffn_w_down):
    batch, seq, _ = x.shape
    x2d = x.reshape(batch * seq, D_MODEL)
    pos = positions.reshape(-1)

    mla_cos, mla_sin = _rope_tables(pos, MLA_ROPE // 2, MLA_ROPE_THETA)
    mla_c, mla_s = _lane_tables(mla_cos, mla_sin, LANES)
    diff_cos, diff_sin = _rope_tables(pos, DIFF_ROT_DIM // 2, DIFF_ROPE_THETA)
    diff_c, diff_s = _lane_tables(diff_cos, diff_sin, DIFF_HEAD_DIM)

    tm = 512
    tf = 1024
    q, k, v = _mla_proj(x2d, mla_c, mla_s, mla_w_dq[0], mla_q_norm[0], mla_w_uq[0],
                        mla_w_dkv[0], mla_kv_norm[0], mla_w_ukv[0], tm)
    o = _mla_flash(q, k, v, batch, seq, tq=512)
    x2d = _post_attn(o, x2d, mla_w_o[0], ln_mix_g[0], ln_mix_b[0], ffn_w_up[0], ffn_w_down[0],
                     ln_ffn_g[0], ln_ffn_b[0], tm, tf)

    lambda_init = 0.8 - 0.6 * math.exp(-0.3 * 1)
    q, k, v = _diff_proj(x2d, diff_c, diff_s, diff_w_qkv[0], tm)
    o = _diff_flash(q, k, v, diff_lambda_q1[0], diff_lambda_k1[0], diff_lambda_q2[0],
                    diff_lambda_k2[0], diff_subln[0], lambda_init, batch, seq, tq=512)
    x2d = _post_attn(o, x2d, diff_w_o[0], ln_mix_g[1], ln_mix_b[1], ffn_w_up[1], ffn_w_down[1],
                     ln_ffn_g[1], ln_ffn_b[1], tm, tf)
    return x2d.reshape(batch, seq, D_MODEL)
```

```python
import functools
import math

import jax
import jax.numpy as jnp
from jax import lax
from jax.experimental import pallas as pl
from jax.experimental.pallas import tpu as pltpu

D_MODEL = 1024
DEPTH = 2

MLA_NOPE = 128
MLA_ROPE = 64
MLA_V = 128
MLA_HEADS = 8
MLA_Q_LORA = 384
MLA_KV_LORA = 256
MLA_ROPE_THETA = 10000.0
MLA_QK_PAD = 256

DIFF_HEAD_DIM = 64
DIFF_HEADS = 8
DIFF_ROT_DIM = 16
DIFF_ROPE_THETA = 500000.0

D_FF = 4 * D_MODEL
DEEPNORM_ALPHA = (2 * DEPTH) ** 0.25
LN_EPS = 1e-5
RMS_EPS = 1e-6

LANES = 128
NEG_BIG = -1e30
LOG2_E = math.log2(math.e)
VMEM_LIMIT_BYTES = 48 * 1024 * 1024

F32 = jnp.float32
BF16 = jnp.bfloat16


def _cparams(*sem):
    return pltpu.CompilerParams(dimension_semantics=sem, vmem_limit_bytes=VMEM_LIMIT_BYTES)


def _dot(a, b):
    return jnp.dot(a, b, preferred_element_type=F32)


def _rms(x, g):
    return x * lax.rsqrt(jnp.mean(jnp.square(x), axis=-1, keepdims=True) + RMS_EPS) * g


def _layernorm(y, g, b):
    mu = jnp.mean(y, axis=-1, keepdims=True)
    d = y - mu
    var = jnp.mean(jnp.square(d), axis=-1, keepdims=True)
    return d * lax.rsqrt(var + LN_EPS) * g + b


def _rope_table_kernel(pos_ref, invf_ref, c_ref, s_ref, *, n_tok, half, dup, group):
    rows = pos_ref.shape[0]
    width = LANES // n_tok
    lane = lax.broadcasted_iota(jnp.int32, (rows, LANES), 1)
    pos = pos_ref[...]
    pos_dense = jnp.broadcast_to(pos[:, n_tok - 1:n_tok], (rows, LANES))
    for a in reversed(range(n_tok - 1)):
        pos_dense = jnp.where(lane < (a + 1) * width, pos[:, a:a + 1], pos_dense)
    ang = pos_dense.astype(F32) * invf_ref[...]
    cos_d = jnp.cos(ang)
    sin_d = jnp.sin(ang)
    neg_sin_d = -sin_d

    def spread(first_d, second_d, a, fill):
        rolled = {}

        def at(x_d, key, shift):
            if (key, shift) not in rolled:
                rolled[key, shift] = x_d if shift == 0 else pltpu.roll(x_d, shift, axis=1)
            return rolled[key, shift]

        out = jnp.full((rows, LANES), fill, F32)
        for g0 in range(0, LANES, group):
            for h, x_d in enumerate((first_d, second_d)):
                src = a * width + (h * half if dup == 2 else 0)
                dst = g0 + h * half
                val = at(x_d, h, (dst - src) % LANES)
                out = jnp.where((lane >= dst) & (lane < dst + half), val, out)
        return out

    for a in range(n_tok):
        c_ref[a] = spread(cos_d, cos_d, a, 1.0)
        s_ref[a] = spread(neg_sin_d, sin_d, a, 0.0)


def _rope_tables(positions, half, group, dup, theta, rows_per_step=512):
    t = positions.size
    n_tok = LANES // (half * dup)
    rows = t // n_tok
    assert rows % rows_per_step == 0
    inv_freq = theta ** (-jnp.arange(0, 2 * half, 2, dtype=F32) / (2 * half))
    invf = jnp.tile(inv_freq, LANES // half).reshape(1, LANES)
    pos_t = positions.reshape(n_tok, rows).T
    out_spec = pl.BlockSpec((n_tok, rows_per_step, LANES), lambda i: (0, i, 0))
    c, s = pl.pallas_call(
        functools.partial(_rope_table_kernel, n_tok=n_tok, half=half, dup=dup, group=group),
        grid=(rows // rows_per_step,),
        in_specs=[pl.BlockSpec((rows_per_step, n_tok), lambda i: (i, 0)),
                  pl.BlockSpec((1, LANES), lambda i: (0, 0))],
        out_specs=[out_spec, out_spec],
        out_shape=[jax.ShapeDtypeStruct((n_tok, rows, LANES), F32)] * 2,
        compiler_params=_cparams("parallel"),
        name="rope_table",
    )(pos_t, invf)
    return c.reshape(t, LANES), s.reshape(t, LANES)


def _rotate_half_in_lanes(x, half, group, c_tab, s_tab):
    lane = lax.broadcasted_iota(jnp.int32, x.shape, 1)
    from_above = pltpu.roll(x, LANES - half, axis=1)
    from_below = pltpu.roll(x, half, axis=1)
    partner = jnp.where(lane % group < half, from_above, from_below)
    return x * c_tab + partner * s_tab


def _mla_proj_kernel(x_ref, wdq_ref, qn_ref, wuq_ref, wdkv_ref, kvn_ref, wukv_ref,
                     c_ref, s_ref, q_ref, k_ref, v_ref, *, scale):
    xb = x_ref[...].astype(BF16)
    c_tab = c_ref[...]
    s_tab = s_ref[...]
    rope = functools.partial(_rotate_half_in_lanes, half=MLA_ROPE // 2, group=LANES,
                             c_tab=c_tab, s_tab=s_tab)

    cq = _rms(_dot(xb, wdq_ref[...]), qn_ref[...]).astype(BF16)
    q = _dot(cq, wuq_ref[...])
    ckv = _dot(xb, wdkv_ref[...])
    c_kv = _rms(ckv[:, :MLA_KV_LORA], kvn_ref[...]).astype(BF16)
    k_pe = rope(ckv[:, MLA_KV_LORA:]).astype(BF16)
    kv = _dot(c_kv, wukv_ref[...])

    for h in range(MLA_HEADS):
        base = h * MLA_QK_PAD
        q_ref[:, base:base + LANES] = (q[:, base:base + LANES] * scale).astype(BF16)
        q_ref[:, base + LANES:base + 2 * LANES] = (rope(q[:, base + LANES:base + 2 * LANES]) * scale).astype(BF16)
        k_ref[:, base:base + LANES] = kv[:, base:base + LANES].astype(BF16)
        k_ref[:, base + LANES:base + 2 * LANES] = k_pe
        v_ref[:, h * MLA_V:(h + 1) * MLA_V] = kv[:, base + LANES:base + 2 * LANES].astype(BF16)


def _mla_proj(x2d, c_tab, s_tab, w_dq, q_norm, w_uq, w_dkv, kv_norm, w_ukv, tm):
    t = x2d.shape[0]
    h = MLA_HEADS
    pad = MLA_QK_PAD - MLA_NOPE - MLA_ROPE
    wuq = jnp.pad(w_uq.reshape(MLA_Q_LORA, h, MLA_NOPE + MLA_ROPE), ((0, 0), (0, 0), (0, pad)))
    wuq = wuq.reshape(MLA_Q_LORA, h * MLA_QK_PAD).astype(BF16)
    wdkv = jnp.pad(w_dkv, ((0, 0), (0, pad))).astype(BF16)
    row = lambda i: (i, 0)
    full = lambda i: (0, 0)
    return pl.pallas_call(
        functools.partial(_mla_proj_kernel, scale=LOG2_E * (MLA_NOPE + MLA_ROPE) ** -0.5),
        grid=(t // tm,),
        in_specs=[
            pl.BlockSpec((tm, D_MODEL), row),
            pl.BlockSpec((D_MODEL, MLA_Q_LORA), full),
            pl.BlockSpec((1, MLA_Q_LORA), full),
            pl.BlockSpec((MLA_Q_LORA, h * MLA_QK_PAD), full),
            pl.BlockSpec((D_MODEL, MLA_KV_LORA + LANES), full),
            pl.BlockSpec((1, MLA_KV_LORA), full),
            pl.BlockSpec((MLA_KV_LORA, h * (MLA_NOPE + MLA_V)), full),
            pl.BlockSpec((tm, LANES), row),
            pl.BlockSpec((tm, LANES), row),
        ],
        out_specs=[
            pl.BlockSpec((tm, h * MLA_QK_PAD), row),
            pl.BlockSpec((tm, h * MLA_QK_PAD), row),
            pl.BlockSpec((tm, h * MLA_V), row),
        ],
        out_shape=[
            jax.ShapeDtypeStruct((t, h * MLA_QK_PAD), BF16),
            jax.ShapeDtypeStruct((t, h * MLA_QK_PAD), BF16),
            jax.ShapeDtypeStruct((t, h * MLA_V), BF16),
        ],
        compiler_params=_cparams("parallel"),
        name="mla_proj",
    )(x2d, w_dq.astype(BF16), q_norm.reshape(1, -1), wuq, wdkv, kv_norm.reshape(1, -1),
      w_ukv.astype(BF16), c_tab, s_tab)


def _diff_proj_kernel(x_ref, w_ref, c_ref, s_ref, q_ref, k_ref, v_ref, *, scale):
    xb = x_ref[...].astype(BF16)
    c_tab = c_ref[...]
    s_tab = s_ref[...]
    rope = functools.partial(_rotate_half_in_lanes, half=DIFF_ROT_DIM // 2, group=DIFF_HEAD_DIM,
                             c_tab=c_tab, s_tab=s_tab)
    qkv = _dot(xb, w_ref[...])
    for j in range(D_MODEL // LANES):
        lo, hi = j * LANES, (j + 1) * LANES
        q_ref[:, lo:hi] = (rope(qkv[:, lo:hi]) * scale).astype(BF16)
        k_ref[:, lo:hi] = rope(qkv[:, D_MODEL + lo:D_MODEL + hi]).astype(BF16)
    v_ref[...] = qkv[:, 2 * D_MODEL:].astype(BF16)


def _diff_proj(x2d, c_tab, s_tab, w_qkv, tm):
    t = x2d.shape[0]
    row = lambda i: (i, 0)
    full = lambda i: (0, 0)
    return pl.pallas_call(
        functools.partial(_diff_proj_kernel, scale=LOG2_E * DIFF_HEAD_DIM ** -0.5),
        grid=(t // tm,),
        in_specs=[
            pl.BlockSpec((tm, D_MODEL), row),
            pl.BlockSpec((D_MODEL, 3 * D_MODEL), full),
            pl.BlockSpec((tm, LANES), row),
            pl.BlockSpec((tm, LANES), row),
        ],
        out_specs=[pl.BlockSpec((tm, D_MODEL), row)] * 3,
        out_shape=[jax.ShapeDtypeStruct((t, D_MODEL), BF16)] * 3,
        compiler_params=_cparams("parallel"),
        name="diff_proj",
    )(x2d, w_qkv.astype(BF16), c_tab, s_tab)


def _transpose_v(v_ref, vt_ref, tk):
    for j in range(vt_ref.shape[0]):
        vt_ref[j] = v_ref[j * tk:(j + 1) * tk, :].astype(F32).T.astype(vt_ref.dtype)


def _flash_head(q_block, k_ref, vt_ref, s_refs, m_ref, l_ref, acc_ref, finish, n_q, tq, fold):
    rows = fold * tq
    steps = [(i, j) for i in range(n_q) for j in range(i + 1)]

    def scores(t):
        i, j = steps[t]
        k = k_ref[j * tq:(j + 1) * tq, :]
        s_refs[t % 2][...] = lax.dot_general(k, q_block(i, j == 0), (((1,), (1,)), ((), ())),
                                             preferred_element_type=F32)

    def consume(t):
        i, j = steps[t]
        s_ref = s_refs[t % 2]
        par = i % 2
        if j == i:
            k_pos = lax.broadcasted_iota(jnp.int32, (tq, rows), 0)
            q_pos = lax.broadcasted_iota(jnp.int32, (tq, rows), 1) % tq
            s_ref[...] = jnp.where(k_pos <= q_pos, s_ref[...], NEG_BIG)
        m_blk = jnp.max(s_ref[...], axis=0, keepdims=True)
        if j == 0:
            m_new = m_blk
            p = jnp.exp2(s_ref[...] - m_new)
            l_ref[par] = jnp.sum(p, axis=0, keepdims=True)
            acc_ref[par] = _dot(vt_ref[j], p.astype(BF16))
        else:
            m_prev = m_ref[par]
            m_new = jnp.maximum(m_prev, m_blk)
            alpha = jnp.exp2(m_prev - m_new)
            p = jnp.exp2(s_ref[...] - m_new)
            l_ref[par] = alpha * l_ref[par] + jnp.sum(p, axis=0, keepdims=True)
            acc_ref[par] = alpha * acc_ref[par] + _dot(vt_ref[j], p.astype(BF16))
        m_ref[par] = m_new
        if j == i:
            finish(i, acc_ref[par], l_ref[par])

    scores(0)
    for t in range(len(steps)):
        if t + 1 < len(steps):
            scores(t + 1)
        consume(t)


def _mla_flash_kernel(q_ref, k_ref, v_ref, o_ref, vt_ref, s0_ref, s1_ref, m_ref, l_ref, acc_ref,
                      *, n_q, tq):
    _transpose_v(v_ref, vt_ref, tq)

    def q_block(i, first):
        del first
        return q_ref[i * tq:(i + 1) * tq, :]

    def finish(i, acc, l):
        o_ref[i * tq:(i + 1) * tq, :] = (acc / l).T.astype(o_ref.dtype)

    _flash_head(q_block, k_ref, vt_ref, (s0_ref, s1_ref), m_ref, l_ref, acc_ref, finish,
                n_q, tq, fold=1)


def _diff_flash_kernel(q_ref, k_ref, v_ref, lq1_ref, lk1_ref, lq2_ref, lk2_ref, subln_ref,
                       o_ref, qs_ref, vt_ref, s0_ref, s1_ref, m_ref, l_ref, acc_ref,
                       *, n_q, tq, lambda_init):
    _transpose_v(v_ref, vt_ref, tq)
    lam = (jnp.exp(jnp.sum(lq1_ref[...] * lk1_ref[...]))
           - jnp.exp(jnp.sum(lq2_ref[...] * lk2_ref[...])) + lambda_init)
    subln = subln_ref[...]

    def q_block(i, first):
        if first:
            q = q_ref[i * tq:(i + 1) * tq, :]
            lane = lax.broadcasted_iota(jnp.int32, q.shape, 1)
            zero = jnp.zeros_like(q)
            qs_ref[i % 2, :tq, :] = jnp.where(lane < DIFF_HEAD_DIM, q, zero)
            qs_ref[i % 2, tq:, :] = jnp.where(lane < DIFF_HEAD_DIM, zero, q)
        return qs_ref[i % 2]

    def finish(i, acc, l):
        o_t = acc[:, :tq] / l[:, :tq] - lam * (acc[:, tq:] / l[:, tq:])
        o = _rms(o_t.T, subln) * (1.0 - lambda_init)
        o_ref[i * tq:(i + 1) * tq, :] = o.astype(o_ref.dtype)

    _flash_head(q_block, k_ref, vt_ref, (s0_ref, s1_ref), m_ref, l_ref, acc_ref, finish,
                n_q, tq, fold=2)


def _flash_scratch(seq, rows, tq, dv):
    return [pltpu.VMEM((seq // tq, dv, tq), BF16),
            pltpu.VMEM((tq, rows), F32), pltpu.VMEM((tq, rows), F32),
            pltpu.VMEM((2, 1, rows), F32), pltpu.VMEM((2, 1, rows), F32),
            pltpu.VMEM((2, dv, rows), F32)]


def _flash_specs(seq, dk, dv):
    head = lambda width: pl.BlockSpec((seq, width), lambda b, h: (b, h))
    return head(dk), head(dk), head(dv), head(dv)


def _mla_flash(q, k, v, batch, seq, tq):
    q_spec, k_spec, v_spec, o_spec = _flash_specs(seq, MLA_QK_PAD, MLA_V)
    return pl.pallas_call(
        functools.partial(_mla_flash_kernel, n_q=seq // tq, tq=tq),
        grid=(batch, MLA_HEADS),
        in_specs=[q_spec, k_spec, v_spec],
        out_specs=o_spec,
        out_shape=jax.ShapeDtypeStruct((batch * seq, MLA_HEADS * MLA_V), BF16),
        scratch_shapes=_flash_scratch(seq, tq, tq, MLA_V),
        compiler_params=_cparams("parallel", "parallel"),
        name="mla_flash",
    )(q, k, v)


def _diff_flash(q, k, v, lq1, lk1, lq2, lk2, subln, lambda_init, batch, seq, tq):
    dv = 2 * DIFF_HEAD_DIM
    q_spec, k_spec, v_spec, o_spec = _flash_specs(seq, dv, dv)
    vec = lambda n: pl.BlockSpec((1, n), lambda b, h: (0, 0))
    return pl.pallas_call(
        functools.partial(_diff_flash_kernel, n_q=seq // tq, tq=tq, lambda_init=lambda_init),
        grid=(batch, DIFF_HEADS),
        in_specs=[q_spec, k_spec, v_spec, vec(DIFF_HEAD_DIM), vec(DIFF_HEAD_DIM),
                  vec(DIFF_HEAD_DIM), vec(DIFF_HEAD_DIM), vec(dv)],
        out_specs=o_spec,
        out_shape=jax.ShapeDtypeStruct((batch * seq, DIFF_HEADS * dv), BF16),
        scratch_shapes=[pltpu.VMEM((2, 2 * tq, dv), BF16)] + _flash_scratch(seq, 2 * tq, tq, dv),
        compiler_params=_cparams("parallel", "parallel"),
        name="diff_flash",
    )(q, k, v, lq1.reshape(1, -1), lk1.reshape(1, -1), lq2.reshape(1, -1), lk2.reshape(1, -1),
      subln.reshape(1, -1))


def _post_attn_kernel(o_ref, x_ref, wo_ref, g1_ref, b1_ref, wu_ref, wd_ref, g2_ref, b2_ref,
                      y_ref, x1_ref, x1b_ref, acc_ref):
    j = pl.program_id(1)

    @pl.when(j == 0)
    def _():
        x1 = _layernorm(DEEPNORM_ALPHA * x_ref[...] + _dot(o_ref[...], wo_ref[...]),
                        g1_ref[...], b1_ref[...])
        x1_ref[...] = x1
        x1b_ref[...] = x1.astype(BF16)

    hidden = jnp.square(jnp.maximum(_dot(x1b_ref[...], wu_ref[...]), 0.0))
    part = _dot(hidden.astype(BF16), wd_ref[...])

    @pl.when(j == 0)
    def _():
        acc_ref[...] = part

    @pl.when(j > 0)
    def _():
        acc_ref[...] += part

    @pl.when(j == pl.num_programs(1) - 1)
    def _():
        y = DEEPNORM_ALPHA * x1_ref[...] + acc_ref[...]
        y_ref[...] = _layernorm(y, g2_ref[...], b2_ref[...])


def _post_attn(o, x2d, w_o, g1, b1, w_up, w_down, g2, b2, tm, tf):
    t = x2d.shape[0]
    row = lambda i, j: (i, 0)
    full = lambda i, j: (0, 0)
    vec = pl.BlockSpec((1, D_MODEL), full)
    return pl.pallas_call(
        _post_attn_kernel,
        grid=(t // tm, D_FF // tf),
        in_specs=[
            pl.BlockSpec((tm, D_MODEL), row),
            pl.BlockSpec((tm, D_MODEL), row),
            pl.BlockSpec((D_MODEL, D_MODEL), full),
            vec, vec,
            pl.BlockSpec((D_MODEL, tf), lambda i, j: (0, j)),
            pl.BlockSpec((tf, D_MODEL), lambda i, j: (j, 0)),
            vec, vec,
        ],
        out_specs=pl.BlockSpec((tm, D_MODEL), row),
        out_shape=jax.ShapeDtypeStruct((t, D_MODEL), F32),
        scratch_shapes=[pltpu.VMEM((tm, D_MODEL), F32), pltpu.VMEM((tm, D_MODEL), BF16),
                        pltpu.VMEM((tm, D_MODEL), F32)],
        compiler_params=_cparams("parallel", "arbitrary"),
        name="post_attn",
    )(o, x2d, w_o.astype(BF16), g1.reshape(1, -1), b1.reshape(1, -1), w_up.astype(BF16),
      w_down.astype(BF16), g2.reshape(1, -1), b2.reshape(1, -1))


def kernel(x, positions, ln_mix_g, ln_mix_b, mla_w_dq, mla_q_norm, mla_w_uq, mla_w_dkv, mla_kv_norm, mla_w_ukv, mla_w_o, diff_w_qkv, diff_lambda_q1, diff_lambda_k1, diff_lambda_q2, diff_lambda_k2, diff_subln, diff_w_o, ln_ffn_g, ln_ffn_b, ffn_w_up, ffn_w_down):
    batch, seq, _ = x.shape
    x2d = x.reshape(batch * seq, D_MODEL)
    pos = positions.reshape(-1)

    mla_c, mla_s = _rope_tables(pos, MLA_ROPE // 2, LANES, 1, MLA_ROPE_THETA)
    diff_c, diff_s = _rope_tables(pos, DIFF_ROT_DIM // 2, DIFF_HEAD_DIM, 2, DIFF_ROPE_THETA)

    tm = 512
    tf = 1024
    q, k, v = _mla_proj(x2d, mla_c, mla_s, mla_w_dq[0], mla_q_norm[0], mla_w_uq[0],
                        mla_w_dkv[0], mla_kv_norm[0], mla_w_ukv[0], tm)
    o = _mla_flash(q, k, v, batch, seq, tq=512)
    x2d = _post_attn(o, x2d, mla_w_o[0], ln_mix_g[0], ln_mix_b[0], ffn_w_up[0], ffn_w_down[0],
                     ln_ffn_g[0], ln_ffn_b[0], tm, tf)

    lambda_init = 0.8 - 0.6 * math.exp(-0.3 * 1)
    q, k, v = _diff_proj(x2d, diff_c, diff_s, diff_w_qkv[0], tm)
    o = _diff_flash(q, k, v, diff_lambda_q1[0], diff_lambda_k1[0], diff_lambda_q2[0],
                    diff_lambda_k2[0], diff_subln[0], lambda_init, batch, seq, tq=512)
    x2d = _post_attn(o, x2d, diff_w_o[0], ln_mix_g[1], ln_mix_b[1], ffn_w_up[1], ffn_w_down[1],
                     ln_ffn_g[1], ln_ffn_b[1], tm, tf)
    return x2d.reshape(batch, seq, D_MODEL)
```

```python
import functools
import math

import jax
import jax.numpy as jnp
from jax import lax
from jax.experimental import pallas as pl
from jax.experimental.pallas import tpu as pltpu

D_MODEL = 1024
DEPTH = 2

MLA_NOPE = 128
MLA_ROPE = 64
MLA_V = 128
MLA_HEADS = 8
MLA_Q_LORA = 384
MLA_KV_LORA = 256
MLA_ROPE_THETA = 10000.0
MLA_QK_PAD = 256

DIFF_HEAD_DIM = 64
DIFF_HEADS = 8
DIFF_ROT_DIM = 16
DIFF_ROPE_THETA = 500000.0

D_FF = 4 * D_MODEL
DEEPNORM_ALPHA = (2 * DEPTH) ** 0.25
LN_EPS = 1e-5
RMS_EPS = 1e-6

LANES = 128
NEG_BIG = -1e30
LOG2_E = math.log2(math.e)
VMEM_LIMIT_BYTES = 48 * 1024 * 1024

F32 = jnp.float32
BF16 = jnp.bfloat16

PROJ_ROWS = 512
ATTN_TILE = 512
POST_ROWS = 1024
POST_CHAINS = 4
FF_CHUNK = 1024


def _cparams(*sem):
    return pltpu.CompilerParams(dimension_semantics=sem, vmem_limit_bytes=VMEM_LIMIT_BYTES)


def _dot(a, b):
    return jnp.dot(a, b, preferred_element_type=F32)


def _rms(x, g):
    return x * lax.rsqrt(jnp.mean(jnp.square(x), axis=-1, keepdims=True) + RMS_EPS) * g


def _layernorm(y, g, b):
    mu = jnp.mean(y, axis=-1, keepdims=True)
    d = y - mu
    var = jnp.mean(jnp.square(d), axis=-1, keepdims=True)
    return d * lax.rsqrt(var + LN_EPS) * g + b


def _rope_table_kernel(pos_ref, invf_ref, c_ref, s_ref, *, n_tok, half, dup, group):
    rows = pos_ref.shape[0]
    width = LANES // n_tok
    lane = lax.broadcasted_iota(jnp.int32, (rows, LANES), 1)
    pos = pos_ref[...]
    pos_dense = jnp.broadcast_to(pos[:, n_tok - 1:n_tok], (rows, LANES))
    for a in reversed(range(n_tok - 1)):
        pos_dense = jnp.where(lane < (a + 1) * width, pos[:, a:a + 1], pos_dense)
    ang = pos_dense.astype(F32) * invf_ref[...]
    cos_d = jnp.cos(ang)
    sin_d = jnp.sin(ang)
    neg_sin_d = -sin_d

    def spread(first_d, second_d, a, fill):
        rolled = {}

        def at(x_d, key, shift):
            if (key, shift) not in rolled:
                rolled[key, shift] = x_d if shift == 0 else pltpu.roll(x_d, shift, axis=1)
            return rolled[key, shift]

        out = jnp.full((rows, LANES), fill, F32)
        for g0 in range(0, LANES, group):
            for h, x_d in enumerate((first_d, second_d)):
                src = a * width + (h * half if dup == 2 else 0)
                dst = g0 + h * half
                val = at(x_d, h, (dst - src) % LANES)
                out = jnp.where((lane >= dst) & (lane < dst + half), val, out)
        return out

    for a in range(n_tok):
        c_ref[a] = spread(cos_d, cos_d, a, 1.0)
        s_ref[a] = spread(neg_sin_d, sin_d, a, 0.0)


def _rope_tables(positions, half, group, dup, theta, rows_per_step=512):
    t = positions.size
    n_tok = LANES // (half * dup)
    rows = t // n_tok
    assert rows % rows_per_step == 0
    inv_freq = theta ** (-jnp.arange(0, 2 * half, 2, dtype=F32) / (2 * half))
    invf = jnp.tile(inv_freq, LANES // half).reshape(1, LANES)
    pos_t = positions.reshape(n_tok, rows).T
    out_spec = pl.BlockSpec((n_tok, rows_per_step, LANES), lambda i: (0, i, 0))
    c, s = pl.pallas_call(
        functools.partial(_rope_table_kernel, n_tok=n_tok, half=half, dup=dup, group=group),
        grid=(rows // rows_per_step,),
        in_specs=[pl.BlockSpec((rows_per_step, n_tok), lambda i: (i, 0)),
                  pl.BlockSpec((1, LANES), lambda i: (0, 0))],
        out_specs=[out_spec, out_spec],
        out_shape=[jax.ShapeDtypeStruct((n_tok, rows, LANES), F32)] * 2,
        compiler_params=_cparams("parallel"),
        name="rope_table",
    )(pos_t, invf)
    return c.reshape(t, LANES), s.reshape(t, LANES)


def _rotate_half_in_lanes(x, half, group, c_tab, s_tab):
    lane = lax.broadcasted_iota(jnp.int32, x.shape, 1)
    from_above = pltpu.roll(x, LANES - half, axis=1)
    from_below = pltpu.roll(x, half, axis=1)
    partner = jnp.where(lane % group < half, from_above, from_below)
    return x * c_tab + partner * s_tab


def _mla_proj_kernel(x_ref, wdq_ref, qn_ref, wuq_ref, wdkv_ref, kvn_ref, wukv_ref,
                     c_ref, s_ref, q_ref, k_ref, v_ref, *, scale):
    xb = x_ref[...].astype(BF16)
    rope = functools.partial(_rotate_half_in_lanes, half=MLA_ROPE // 2, group=LANES,
                             c_tab=c_ref[...], s_tab=s_ref[...])

    cq = _rms(_dot(xb, wdq_ref[...]), qn_ref[...]).astype(BF16)
    q = _dot(cq, wuq_ref[...])
    ckv = _dot(xb, wdkv_ref[...])
    c_kv = _rms(ckv[:, :MLA_KV_LORA], kvn_ref[...]).astype(BF16)
    k_pe = rope(ckv[:, MLA_KV_LORA:]).astype(BF16)
    kv = _dot(c_kv, wukv_ref[...])

    for h in range(MLA_HEADS):
        lo, mid, hi = h * MLA_QK_PAD, h * MLA_QK_PAD + LANES, (h + 1) * MLA_QK_PAD
        q_ref[:, lo:mid] = (q[:, lo:mid] * scale).astype(BF16)
        q_ref[:, mid:hi] = (rope(q[:, mid:hi]) * scale).astype(BF16)
        k_ref[:, lo:mid] = kv[:, lo:mid].astype(BF16)
        k_ref[:, mid:hi] = k_pe
        v_ref[:, h * MLA_V:(h + 1) * MLA_V] = kv[:, mid:hi].astype(BF16)


def _mla_proj(x2d, c_tab, s_tab, w_dq, q_norm, w_uq, w_dkv, kv_norm, w_ukv, tm):
    t = x2d.shape[0]
    h = MLA_HEADS
    pad = MLA_QK_PAD - MLA_NOPE - MLA_ROPE
    wuq = jnp.pad(w_uq.reshape(MLA_Q_LORA, h, MLA_NOPE + MLA_ROPE), ((0, 0), (0, 0), (0, pad)))
    wuq = wuq.reshape(MLA_Q_LORA, h * MLA_QK_PAD).astype(BF16)
    wdkv = jnp.pad(w_dkv, ((0, 0), (0, pad))).astype(BF16)
    row = lambda i: (i, 0)
    full = lambda i: (0, 0)
    return pl.pallas_call(
        functools.partial(_mla_proj_kernel, scale=LOG2_E * (MLA_NOPE + MLA_ROPE) ** -0.5),
        grid=(t // tm,),
        in_specs=[
            pl.BlockSpec((tm, D_MODEL), row),
            pl.BlockSpec((D_MODEL, MLA_Q_LORA), full),
            pl.BlockSpec((1, MLA_Q_LORA), full),
            pl.BlockSpec((MLA_Q_LORA, h * MLA_QK_PAD), full),
            pl.BlockSpec((D_MODEL, MLA_KV_LORA + LANES), full),
            pl.BlockSpec((1, MLA_KV_LORA), full),
            pl.BlockSpec((MLA_KV_LORA, h * (MLA_NOPE + MLA_V)), full),
            pl.BlockSpec((tm, LANES), row),
            pl.BlockSpec((tm, LANES), row),
        ],
        out_specs=[
            pl.BlockSpec((tm, h * MLA_QK_PAD), row),
            pl.BlockSpec((tm, h * MLA_QK_PAD), row),
            pl.BlockSpec((tm, h * MLA_V), row),
        ],
        out_shape=[
            jax.ShapeDtypeStruct((t, h * MLA_QK_PAD), BF16),
            jax.ShapeDtypeStruct((t, h * MLA_QK_PAD), BF16),
            jax.ShapeDtypeStruct((t, h * MLA_V), BF16),
        ],
        compiler_params=_cparams("parallel"),
        name="mla_proj",
    )(x2d, w_dq.astype(BF16), q_norm.reshape(1, -1), wuq, wdkv, kv_norm.reshape(1, -1),
      w_ukv.astype(BF16), c_tab, s_tab)


def _diff_proj_kernel(x_ref, w_ref, c_ref, s_ref, q_ref, k_ref, v_ref, *, scale):
    rope = functools.partial(_rotate_half_in_lanes, half=DIFF_ROT_DIM // 2, group=DIFF_HEAD_DIM,
                             c_tab=c_ref[...], s_tab=s_ref[...])
    qkv = _dot(x_ref[...].astype(BF16), w_ref[...])
    for j in range(D_MODEL // LANES):
        lo, hi = j * LANES, (j + 1) * LANES
        q_ref[:, lo:hi] = (rope(qkv[:, lo:hi]) * scale).astype(BF16)
        k_ref[:, lo:hi] = rope(qkv[:, D_MODEL + lo:D_MODEL + hi]).astype(BF16)
    v_ref[...] = qkv[:, 2 * D_MODEL:].astype(BF16)


def _diff_proj(x2d, c_tab, s_tab, w_qkv, tm):
    t = x2d.shape[0]
    row = lambda i: (i, 0)
    full = lambda i: (0, 0)
    return pl.pallas_call(
        functools.partial(_diff_proj_kernel, scale=LOG2_E * DIFF_HEAD_DIM ** -0.5),
        grid=(t // tm,),
        in_specs=[
            pl.BlockSpec((tm, D_MODEL), row),
            pl.BlockSpec((D_MODEL, 3 * D_MODEL), full),
            pl.BlockSpec((tm, LANES), row),
            pl.BlockSpec((tm, LANES), row),
        ],
        out_specs=[pl.BlockSpec((tm, D_MODEL), row)] * 3,
        out_shape=[jax.ShapeDtypeStruct((t, D_MODEL), BF16)] * 3,
        compiler_params=_cparams("parallel"),
        name="diff_proj",
    )(x2d, w_qkv.astype(BF16), c_tab, s_tab)


def _transpose_v(v_ref, vt_ref, tk):
    for j in range(vt_ref.shape[0]):
        vt_ref[j] = v_ref[j * tk:(j + 1) * tk, :].astype(F32).T.astype(vt_ref.dtype)


def _flash_head(q_block, k_ref, vt_ref, s_refs, m_ref, l_ref, acc_ref, finish, n_q, tq, fold):
    rows = fold * tq
    steps = [(i, j) for i in range(n_q) for j in range(i + 1)]

    def scores(t):
        i, j = steps[t]
        k = k_ref[j * tq:(j + 1) * tq, :]
        s_refs[t % 2][...] = lax.dot_general(k, q_block(i, j == 0), (((1,), (1,)), ((), ())),
                                             preferred_element_type=F32)

    def consume(t):
        i, j = steps[t]
        s_ref = s_refs[t % 2]
        par = i % 2
        if j == i:
            k_pos = lax.broadcasted_iota(jnp.int32, (tq, rows), 0)
            q_pos = lax.broadcasted_iota(jnp.int32, (tq, rows), 1) % tq
            s_ref[...] = jnp.where(k_pos <= q_pos, s_ref[...], NEG_BIG)
        m_blk = jnp.max(s_ref[...], axis=0, keepdims=True)
        if j == 0:
            m_new = m_blk
            p = jnp.exp2(s_ref[...] - m_new)
            l_ref[par] = jnp.sum(p, axis=0, keepdims=True)
            acc_ref[par] = _dot(vt_ref[j], p.astype(BF16))
        else:
            m_prev = m_ref[par]
            m_new = jnp.maximum(m_prev, m_blk)
            alpha = jnp.exp2(m_prev - m_new)
            p = jnp.exp2(s_ref[...] - m_new)
            l_ref[par] = alpha * l_ref[par] + jnp.sum(p, axis=0, keepdims=True)
            acc_ref[par] = alpha * acc_ref[par] + _dot(vt_ref[j], p.astype(BF16))
        m_ref[par] = m_new
        if j == i:
            finish(i, acc_ref[par], l_ref[par])

    scores(0)
    for t in range(len(steps)):
        if t + 1 < len(steps):
            scores(t + 1)
        consume(t)


def _mla_flash_kernel(q_ref, k_ref, v_ref, o_ref, vt_ref, s0_ref, s1_ref, m_ref, l_ref, acc_ref,
                      *, n_q, tq):
    _transpose_v(v_ref, vt_ref, tq)

    def q_block(i, first):
        del first
        return q_ref[i * tq:(i + 1) * tq, :]

    def finish(i, acc, l):
        o_ref[i * tq:(i + 1) * tq, :] = (acc / l).T.astype(o_ref.dtype)

    _flash_head(q_block, k_ref, vt_ref, (s0_ref, s1_ref), m_ref, l_ref, acc_ref, finish,
                n_q, tq, fold=1)


def _diff_flash_kernel(q_ref, k_ref, v_ref, lq1_ref, lk1_ref, lq2_ref, lk2_ref, subln_ref,
                       o_ref, qs_ref, vt_ref, s0_ref, s1_ref, m_ref, l_ref, acc_ref,
                       *, n_q, tq, lambda_init):
    _transpose_v(v_ref, vt_ref, tq)
    lam = (jnp.exp(jnp.sum(lq1_ref[...] * lk1_ref[...]))
           - jnp.exp(jnp.sum(lq2_ref[...] * lk2_ref[...])) + lambda_init)
    subln = subln_ref[...]

    def q_block(i, first):
        if first:
            q = q_ref[i * tq:(i + 1) * tq, :]
            lane = lax.broadcasted_iota(jnp.int32, q.shape, 1)
            zero = jnp.zeros_like(q)
            qs_ref[i % 2, :tq, :] = jnp.where(lane < DIFF_HEAD_DIM, q, zero)
            qs_ref[i % 2, tq:, :] = jnp.where(lane < DIFF_HEAD_DIM, zero, q)
        return qs_ref[i % 2]

    def finish(i, acc, l):
        o_t = acc[:, :tq] / l[:, :tq] - lam * (acc[:, tq:] / l[:, tq:])
        o = _rms(o_t.T, subln) * (1.0 - lambda_init)
        o_ref[i * tq:(i + 1) * tq, :] = o.astype(o_ref.dtype)

    _flash_head(q_block, k_ref, vt_ref, (s0_ref, s1_ref), m_ref, l_ref, acc_ref, finish,
                n_q, tq, fold=2)


def _flash_scratch(seq, rows, tq, dv):
    return [pltpu.VMEM((seq // tq, dv, tq), BF16),
            pltpu.VMEM((tq, rows), F32), pltpu.VMEM((tq, rows), F32),
            pltpu.VMEM((2, 1, rows), F32), pltpu.VMEM((2, 1, rows), F32),
            pltpu.VMEM((2, dv, rows), F32)]


def _flash_specs(seq, dk, dv):
    head = lambda width: pl.BlockSpec((seq, width), lambda b, h: (b, h))
    return head(dk), head(dk), head(dv), head(dv)


def _mla_flash(q, k, v, batch, seq, tq):
    q_spec, k_spec, v_spec, o_spec = _flash_specs(seq, MLA_QK_PAD, MLA_V)
    return pl.pallas_call(
        functools.partial(_mla_flash_kernel, n_q=seq // tq, tq=tq),
        grid=(batch, MLA_HEADS),
        in_specs=[q_spec, k_spec, v_spec],
        out_specs=o_spec,
        out_shape=jax.ShapeDtypeStruct((batch * seq, MLA_HEADS * MLA_V), BF16),
        scratch_shapes=_flash_scratch(seq, tq, tq, MLA_V),
        compiler_params=_cparams("parallel", "parallel"),
        name="mla_flash",
    )(q, k, v)


def _diff_flash(q, k, v, lq1, lk1, lq2, lk2, subln, lambda_init, batch, seq, tq):
    dv = 2 * DIFF_HEAD_DIM
    q_spec, k_spec, v_spec, o_spec = _flash_specs(seq, dv, dv)
    vec = lambda n: pl.BlockSpec((1, n), lambda b, h: (0, 0))
    return pl.pallas_call(
        functools.partial(_diff_flash_kernel, n_q=seq // tq, tq=tq, lambda_init=lambda_init),
        grid=(batch, DIFF_HEADS),
        in_specs=[q_spec, k_spec, v_spec, vec(DIFF_HEAD_DIM), vec(DIFF_HEAD_DIM),
                  vec(DIFF_HEAD_DIM), vec(DIFF_HEAD_DIM), vec(dv)],
        out_specs=o_spec,
        out_shape=jax.ShapeDtypeStruct((batch * seq, DIFF_HEADS * dv), BF16),
        scratch_shapes=[pltpu.VMEM((2, 2 * tq, dv), BF16)] + _flash_scratch(seq, 2 * tq, tq, dv),
        compiler_params=_cparams("parallel", "parallel"),
        name="diff_flash",
    )(q, k, v, lq1.reshape(1, -1), lk1.reshape(1, -1), lq2.reshape(1, -1), lk2.reshape(1, -1),
      subln.reshape(1, -1))


def _post_attn_kernel(o_ref, x_ref, wo_ref, g1_ref, b1_ref, wu_ref, wd_ref, g2_ref, b2_ref,
                      y_ref, x1b_ref, acc_ref, *, n_chain):
    j = pl.program_id(1)
    last = pl.num_programs(1) - 1
    rc = acc_ref.shape[0] // n_chain

    def step(first, final):
        for c in range(n_chain):
            rows = pl.ds(c * rc, rc)
            if first:
                x1 = _layernorm(DEEPNORM_ALPHA * x_ref[rows, :] + _dot(o_ref[rows, :], wo_ref[...]),
                                g1_ref[...], b1_ref[...])
                xb = x1.astype(BF16)
                x1b_ref[rows, :] = xb
                base = DEEPNORM_ALPHA * x1
            else:
                xb = x1b_ref[rows, :]
                base = acc_ref[rows, :]
            hidden = jnp.square(jnp.maximum(_dot(xb, wu_ref[...]), 0.0))
            total = base + _dot(hidden.astype(BF16), wd_ref[...])
            if final:
                y_ref[rows, :] = _layernorm(total, g2_ref[...], b2_ref[...])
            else:
                acc_ref[rows, :] = total

    @pl.when(j == 0)
    def _():
        step(first=True, final=False)

    @pl.when((j > 0) & (j < last))
    def _():
        step(first=False, final=False)

    @pl.when(j == last)
    def _():
        step(first=False, final=True)


def _post_attn(o, x2d, w_o, g1, b1, w_up, w_down, g2, b2, tm, tf, n_chain):
    t = x2d.shape[0]
    row = lambda i, j: (i, 0)
    full = lambda i, j: (0, 0)
    vec = pl.BlockSpec((1, D_MODEL), full)
    assert D_FF // tf >= 2
    return pl.pallas_call(
        functools.partial(_post_attn_kernel, n_chain=n_chain),
        grid=(t // tm, D_FF // tf),
        in_specs=[
            pl.BlockSpec((tm, D_MODEL), row),
            pl.BlockSpec((tm, D_MODEL), row),
            pl.BlockSpec((D_MODEL, D_MODEL), full),
            vec, vec,
            pl.BlockSpec((D_MODEL, tf), lambda i, j: (0, j)),
            pl.BlockSpec((tf, D_MODEL), lambda i, j: (j, 0)),
            vec, vec,
        ],
        out_specs=pl.BlockSpec((tm, D_MODEL), row),
        out_shape=jax.ShapeDtypeStruct((t, D_MODEL), F32),
        scratch_shapes=[pltpu.VMEM((tm, D_MODEL), BF16), pltpu.VMEM((tm, D_MODEL), F32)],
        compiler_params=_cparams("parallel", "arbitrary"),
        name="post_attn",
    )(o, x2d, w_o.astype(BF16), g1.reshape(1, -1), b1.reshape(1, -1), w_up.astype(BF16),
      w_down.astype(BF16), g2.reshape(1, -1), b2.reshape(1, -1))


def kernel(x, positions, ln_mix_g, ln_mix_b, mla_w_dq, mla_q_norm, mla_w_uq, mla_w_dkv, mla_kv_norm, mla_w_ukv, mla_w_o, diff_w_qkv, diff_lambda_q1, diff_lambda_k1, diff_lambda_q2, diff_lambda_k2, diff_subln, diff_w_o, ln_ffn_g, ln_ffn_b, ffn_w_up, ffn_w_down):
    batch, seq, _ = x.shape
    x2d = x.reshape(batch * seq, D_MODEL)
    pos = positions.reshape(-1)

    mla_c, mla_s = _rope_tables(pos, MLA_ROPE // 2, LANES, 1, MLA_ROPE_THETA)
    diff_c, diff_s = _rope_tables(pos, DIFF_ROT_DIM // 2, DIFF_HEAD_DIM, 2, DIFF_ROPE_THETA)

    post_tiles = (POST_ROWS, FF_CHUNK, POST_CHAINS)
    q, k, v = _mla_proj(x2d, mla_c, mla_s, mla_w_dq[0], mla_q_norm[0], mla_w_uq[0],
                        mla_w_dkv[0], mla_kv_norm[0], mla_w_ukv[0], PROJ_ROWS)
    o = _mla_flash(q, k, v, batch, seq, ATTN_TILE)
    x2d = _post_attn(o, x2d, mla_w_o[0], ln_mix_g[0], ln_mix_b[0], ffn_w_up[0], ffn_w_down[0],
                     ln_ffn_g[0], ln_ffn_b[0], *post_tiles)

    lambda_init = 0.8 - 0.6 * math.exp(-0.3 * 1)
    q, k, v = _diff_proj(x2d, diff_c, diff_s, diff_w_qkv[0], PROJ_ROWS)
    o = _diff_flash(q, k, v, diff_lambda_q1[0], diff_lambda_k1[0], diff_lambda_q2[0],
                    diff_lambda_k2[0], diff_subln[0], lambda_init, batch, seq, ATTN_TILE)
    x2d = _post_attn(o, x2d, diff_w_o[0], ln_mix_g[1], ln_mix_b[1], ffn_w_up[1], ffn_w_down[1],
                     ln_ffn_g[1], ln_ffn_b[1], *post_tiles)
    return x2d.reshape(batch, seq, D_MODEL)
```

```python
import functools
import math

import jax
import jax.numpy as jnp
from jax import lax
from jax.experimental import pallas as pl
from jax.experimental.pallas import tpu as pltpu

D_MODEL = 1024
DEPTH = 2

MLA_NOPE = 128
MLA_ROPE = 64
MLA_V = 128
MLA_HEADS = 8
MLA_Q_LORA = 384
MLA_KV_LORA = 256
MLA_ROPE_THETA = 10000.0
MLA_QK_PAD = 256

DIFF_HEAD_DIM = 64
DIFF_HEADS = 8
DIFF_ROT_DIM = 16
DIFF_ROPE_THETA = 500000.0

D_FF = 4 * D_MODEL
DEEPNORM_ALPHA = (2 * DEPTH) ** 0.25
LN_EPS = 1e-5
RMS_EPS = 1e-6

LANES = 128
SUM_ROWS = 16
NEG_BIG = -1e30
LOG2_E = math.log2(math.e)
VMEM_LIMIT_BYTES = 48 * 1024 * 1024

F32 = jnp.float32
BF16 = jnp.bfloat16

PROJ_ROWS = 512
ATTN_TILE = 512
POST_ROWS = 1024
POST_CHAINS = 4
FF_CHUNK = 1024


def _cparams(*sem):
    return pltpu.CompilerParams(dimension_semantics=sem, vmem_limit_bytes=VMEM_LIMIT_BYTES)


def _dot(a, b):
    return jnp.dot(a, b, preferred_element_type=F32)


def _rms(x, g):
    return x * lax.rsqrt(jnp.mean(jnp.square(x), axis=-1, keepdims=True) + RMS_EPS) * g


def _layernorm(y, g, b):
    mu = jnp.mean(y, axis=-1, keepdims=True)
    d = y - mu
    var = jnp.mean(jnp.square(d), axis=-1, keepdims=True)
    return d * lax.rsqrt(var + LN_EPS) * g + b


def _rope_table_kernel(pos_ref, invf_ref, c_ref, s_ref, *, n_tok, half, dup, group):
    rows = pos_ref.shape[0]
    width = LANES // n_tok
    lane = lax.broadcasted_iota(jnp.int32, (rows, LANES), 1)
    pos = pos_ref[...]
    pos_dense = jnp.broadcast_to(pos[:, n_tok - 1:n_tok], (rows, LANES))
    for a in reversed(range(n_tok - 1)):
        pos_dense = jnp.where(lane < (a + 1) * width, pos[:, a:a + 1], pos_dense)
    ang = pos_dense.astype(F32) * invf_ref[...]
    cos_d = jnp.cos(ang)
    sin_d = jnp.sin(ang)
    neg_sin_d = -sin_d

    def spread(first_d, second_d, a, fill):
        rolled = {}

        def at(x_d, key, shift):
            if (key, shift) not in rolled:
                rolled[key, shift] = x_d if shift == 0 else pltpu.roll(x_d, shift, axis=1)
            return rolled[key, shift]

        out = jnp.full((rows, LANES), fill, F32)
        for g0 in range(0, LANES, group):
            for h, x_d in enumerate((first_d, second_d)):
                src = a * width + (h * half if dup == 2 else 0)
                dst = g0 + h * half
                val = at(x_d, h, (dst - src) % LANES)
                out = jnp.where((lane >= dst) & (lane < dst + half), val, out)
        return out

    for a in range(n_tok):
        c_ref[a] = spread(cos_d, cos_d, a, 1.0)
        s_ref[a] = spread(neg_sin_d, sin_d, a, 0.0)


def _rope_tables(positions, half, group, dup, theta, rows_per_step=512):
    t = positions.size
    n_tok = LANES // (half * dup)
    rows = t // n_tok
    assert rows % rows_per_step == 0
    inv_freq = theta ** (-jnp.arange(0, 2 * half, 2, dtype=F32) / (2 * half))
    invf = jnp.tile(inv_freq, LANES // half).reshape(1, LANES)
    pos_t = positions.reshape(n_tok, rows).T
    out_spec = pl.BlockSpec((n_tok, rows_per_step, LANES), lambda i: (0, i, 0))
    c, s = pl.pallas_call(
        functools.partial(_rope_table_kernel, n_tok=n_tok, half=half, dup=dup, group=group),
        grid=(rows // rows_per_step,),
        in_specs=[pl.BlockSpec((rows_per_step, n_tok), lambda i: (i, 0)),
                  pl.BlockSpec((1, LANES), lambda i: (0, 0))],
        out_specs=[out_spec, out_spec],
        out_shape=[jax.ShapeDtypeStruct((n_tok, rows, LANES), F32)] * 2,
        compiler_params=_cparams("parallel"),
        name="rope_table",
    )(pos_t, invf)
    return c.reshape(t, LANES), s.reshape(t, LANES)


def _rotate_half_in_lanes(x, half, group, c_tab, s_tab):
    lane = lax.broadcasted_iota(jnp.int32, x.shape, 1)
    from_above = pltpu.roll(x, LANES - half, axis=1)
    from_below = pltpu.roll(x, half, axis=1)
    partner = jnp.where(lane % group < half, from_above, from_below)
    return x * c_tab + partner * s_tab


def _mla_proj_kernel(x_ref, wdq_ref, qn_ref, wuq_ref, wdkv_ref, kvn_ref, wukv_ref,
                     c_ref, s_ref, q_ref, k_ref, v_ref, *, scale):
    xb = x_ref[...].astype(BF16)
    rope = functools.partial(_rotate_half_in_lanes, half=MLA_ROPE // 2, group=LANES,
                             c_tab=c_ref[...], s_tab=s_ref[...])

    cq = _rms(_dot(xb, wdq_ref[...]), qn_ref[...]).astype(BF16)
    q = _dot(cq, wuq_ref[...])
    ckv = _dot(xb, wdkv_ref[...])
    c_kv = _rms(ckv[:, :MLA_KV_LORA], kvn_ref[...]).astype(BF16)
    k_pe = rope(ckv[:, MLA_KV_LORA:]).astype(BF16)
    kv = _dot(c_kv, wukv_ref[...])

    for h in range(MLA_HEADS):
        lo, mid, hi = h * MLA_QK_PAD, h * MLA_QK_PAD + LANES, (h + 1) * MLA_QK_PAD
        q_ref[:, lo:mid] = (q[:, lo:mid] * scale).astype(BF16)
        q_ref[:, mid:hi] = (rope(q[:, mid:hi]) * scale).astype(BF16)
        k_ref[:, lo:mid] = kv[:, lo:mid].astype(BF16)
        k_ref[:, mid:hi] = k_pe
        v_ref[:, h * MLA_V:(h + 1) * MLA_V] = kv[:, mid:hi].astype(BF16)


def _mla_proj(x2d, c_tab, s_tab, w_dq, q_norm, w_uq, w_dkv, kv_norm, w_ukv, tm):
    t = x2d.shape[0]
    h = MLA_HEADS
    pad = MLA_QK_PAD - MLA_NOPE - MLA_ROPE
    wuq = jnp.pad(w_uq.reshape(MLA_Q_LORA, h, MLA_NOPE + MLA_ROPE), ((0, 0), (0, 0), (0, pad)))
    wuq = wuq.reshape(MLA_Q_LORA, h * MLA_QK_PAD).astype(BF16)
    wdkv = jnp.pad(w_dkv, ((0, 0), (0, pad))).astype(BF16)
    row = lambda i: (i, 0)
    full = lambda i: (0, 0)
    return pl.pallas_call(
        functools.partial(_mla_proj_kernel, scale=LOG2_E * (MLA_NOPE + MLA_ROPE) ** -0.5),
        grid=(t // tm,),
        in_specs=[
            pl.BlockSpec((tm, D_MODEL), row),
            pl.BlockSpec((D_MODEL, MLA_Q_LORA), full),
            pl.BlockSpec((1, MLA_Q_LORA), full),
            pl.BlockSpec((MLA_Q_LORA, h * MLA_QK_PAD), full),
            pl.BlockSpec((D_MODEL, MLA_KV_LORA + LANES), full),
            pl.BlockSpec((1, MLA_KV_LORA), full),
            pl.BlockSpec((MLA_KV_LORA, h * (MLA_NOPE + MLA_V)), full),
            pl.BlockSpec((tm, LANES), row),
            pl.BlockSpec((tm, LANES), row),
        ],
        out_specs=[
            pl.BlockSpec((tm, h * MLA_QK_PAD), row),
            pl.BlockSpec((tm, h * MLA_QK_PAD), row),
            pl.BlockSpec((tm, h * MLA_V), row),
        ],
        out_shape=[
            jax.ShapeDtypeStruct((t, h * MLA_QK_PAD), BF16),
            jax.ShapeDtypeStruct((t, h * MLA_QK_PAD), BF16),
            jax.ShapeDtypeStruct((t, h * MLA_V), BF16),
        ],
        compiler_params=_cparams("parallel"),
        name="mla_proj",
    )(x2d, w_dq.astype(BF16), q_norm.reshape(1, -1), wuq, wdkv, kv_norm.reshape(1, -1),
      w_ukv.astype(BF16), c_tab, s_tab)


def _diff_proj_kernel(x_ref, w_ref, c_ref, s_ref, q_ref, k_ref, v_ref, *, scale):
    rope = functools.partial(_rotate_half_in_lanes, half=DIFF_ROT_DIM // 2, group=DIFF_HEAD_DIM,
                             c_tab=c_ref[...], s_tab=s_ref[...])
    qkv = _dot(x_ref[...].astype(BF16), w_ref[...])
    for j in range(D_MODEL // LANES):
        lo, hi = j * LANES, (j + 1) * LANES
        q_ref[:, lo:hi] = (rope(qkv[:, lo:hi]) * scale).astype(BF16)
        k_ref[:, lo:hi] = rope(qkv[:, D_MODEL + lo:D_MODEL + hi]).astype(BF16)
    v_ref[...] = qkv[:, 2 * D_MODEL:].astype(BF16)


def _diff_proj(x2d, c_tab, s_tab, w_qkv, tm):
    t = x2d.shape[0]
    row = lambda i: (i, 0)
    full = lambda i: (0, 0)
    return pl.pallas_call(
        functools.partial(_diff_proj_kernel, scale=LOG2_E * DIFF_HEAD_DIM ** -0.5),
        grid=(t // tm,),
        in_specs=[
            pl.BlockSpec((tm, D_MODEL), row),
            pl.BlockSpec((D_MODEL, 3 * D_MODEL), full),
            pl.BlockSpec((tm, LANES), row),
            pl.BlockSpec((tm, LANES), row),
        ],
        out_specs=[pl.BlockSpec((tm, D_MODEL), row)] * 3,
        out_shape=[jax.ShapeDtypeStruct((t, D_MODEL), BF16)] * 3,
        compiler_params=_cparams("parallel"),
        name="diff_proj",
    )(x2d, w_qkv.astype(BF16), c_tab, s_tab)


def _transpose_v(v_ref, vt_ref, tk):
    dv = v_ref.shape[1]
    for j in range(vt_ref.shape[0]):
        vt_ref[j, :dv, :] = v_ref[j * tk:(j + 1) * tk, :].astype(F32).T.astype(vt_ref.dtype)
        vt_ref[j, dv:, :] = jnp.ones((SUM_ROWS, tk), vt_ref.dtype)


def _flash_head(q_block, k_ref, vt_ref, s_refs, m_ref, acc_ref, finish, n_q, tq, fold):
    rows = fold * tq
    dv = acc_ref.shape[1] - SUM_ROWS
    steps = [(i, j) for i in range(n_q) for j in range(i + 1)]

    def scores(t):
        i, j = steps[t]
        k = k_ref[j * tq:(j + 1) * tq, :]
        s_refs[t % 2][...] = _dot(k, q_block(i, j == 0))

    def consume(t):
        i, j = steps[t]
        s_ref = s_refs[t % 2]
        par = i % 2
        if j == i:
            k_pos = lax.broadcasted_iota(jnp.int32, (tq, rows), 0)
            q_pos = lax.broadcasted_iota(jnp.int32, (tq, rows), 1) % tq
            s_ref[...] = jnp.where(k_pos <= q_pos, s_ref[...], NEG_BIG)
        m_blk = jnp.max(s_ref[...], axis=0, keepdims=True)
        if j == 0:
            m_new = m_blk
            p = jnp.exp2(s_ref[...] - m_new)
            acc_ref[par] = _dot(vt_ref[j], p.astype(BF16))
        else:
            m_prev = m_ref[par]
            m_new = jnp.maximum(m_prev, m_blk)
            alpha = jnp.exp2(m_prev - m_new)
            p = jnp.exp2(s_ref[...] - m_new)
            acc_ref[par] = alpha * acc_ref[par] + _dot(vt_ref[j], p.astype(BF16))
        m_ref[par] = m_new
        if j == i:
            finish(i, acc_ref[par, :dv, :], acc_ref[par, dv:dv + 1, :])

    scores(0)
    for t in range(len(steps)):
        if t + 1 < len(steps):
            scores(t + 1)
        consume(t)


def _mla_flash_kernel(q_ref, k_ref, v_ref, o_ref, qt_ref, vt_ref, s0_ref, s1_ref, m_ref, acc_ref,
                      *, n_q, tq):
    _transpose_v(v_ref, vt_ref, tq)

    def q_block(i, first):
        if first:
            qt_ref[i % 2] = q_ref[i * tq:(i + 1) * tq, :].astype(F32).T.astype(BF16)
        return qt_ref[i % 2]

    def finish(i, acc, l):
        o_ref[i * tq:(i + 1) * tq, :] = (acc / l).T.astype(o_ref.dtype)

    _flash_head(q_block, k_ref, vt_ref, (s0_ref, s1_ref), m_ref, acc_ref, finish,
                n_q, tq, fold=1)


def _diff_flash_kernel(q_ref, k_ref, v_ref, lq1_ref, lk1_ref, lq2_ref, lk2_ref, subln_ref,
                       o_ref, qs_ref, vt_ref, s0_ref, s1_ref, m_ref, acc_ref,
                       *, n_q, tq, lambda_init):
    _transpose_v(v_ref, vt_ref, tq)
    lam = (jnp.exp(jnp.sum(lq1_ref[...] * lk1_ref[...]))
           - jnp.exp(jnp.sum(lq2_ref[...] * lk2_ref[...])) + lambda_init)
    subln = subln_ref[...]

    def q_block(i, first):
        if first:
            q = q_ref[i * tq:(i + 1) * tq, :]
            lane = lax.broadcasted_iota(jnp.int32, q.shape, 1)
            zero = jnp.zeros_like(q)
            qs_ref[i % 2, :, :tq] = jnp.where(lane < DIFF_HEAD_DIM, q, zero).astype(F32).T.astype(BF16)
            qs_ref[i % 2, :, tq:] = jnp.where(lane < DIFF_HEAD_DIM, zero, q).astype(F32).T.astype(BF16)
        return qs_ref[i % 2]

    def finish(i, acc, l):
        o_t = acc[:, :tq] / l[:, :tq] - lam * (acc[:, tq:] / l[:, tq:])
        o = _rms(o_t.T, subln) * (1.0 - lambda_init)
        o_ref[i * tq:(i + 1) * tq, :] = o.astype(o_ref.dtype)

    _flash_head(q_block, k_ref, vt_ref, (s0_ref, s1_ref), m_ref, acc_ref, finish,
                n_q, tq, fold=2)


def _flash_scratch(seq, rows, tq, dv):
    return [pltpu.VMEM((seq // tq, dv + SUM_ROWS, tq), BF16),
            pltpu.VMEM((tq, rows), F32), pltpu.VMEM((tq, rows), F32),
            pltpu.VMEM((2, 1, rows), F32),
            pltpu.VMEM((2, dv + SUM_ROWS, rows), F32)]


def _flash_specs(seq, dk, dv):
    head = lambda width: pl.BlockSpec((seq, width), lambda b, h: (b, h))
    return head(dk), head(dk), head(dv), head(dv)


def _mla_flash(q, k, v, batch, seq, tq):
    q_spec, k_spec, v_spec, o_spec = _flash_specs(seq, MLA_QK_PAD, MLA_V)
    return pl.pallas_call(
        functools.partial(_mla_flash_kernel, n_q=seq // tq, tq=tq),
        grid=(batch, MLA_HEADS),
        in_specs=[q_spec, k_spec, v_spec],
        out_specs=o_spec,
        out_shape=jax.ShapeDtypeStruct((batch * seq, MLA_HEADS * MLA_V), BF16),
        scratch_shapes=[pltpu.VMEM((2, MLA_QK_PAD, tq), BF16)] + _flash_scratch(seq, tq, tq, MLA_V),
        compiler_params=_cparams("parallel", "parallel"),
        name="mla_flash",
    )(q, k, v)


def _diff_flash(q, k, v, lq1, lk1, lq2, lk2, subln, lambda_init, batch, seq, tq):
    dv = 2 * DIFF_HEAD_DIM
    q_spec, k_spec, v_spec, o_spec = _flash_specs(seq, dv, dv)
    vec = lambda n: pl.BlockSpec((1, n), lambda b, h: (0, 0))
    return pl.pallas_call(
        functools.partial(_diff_flash_kernel, n_q=seq // tq, tq=tq, lambda_init=lambda_init),
        grid=(batch, DIFF_HEADS),
        in_specs=[q_spec, k_spec, v_spec, vec(DIFF_HEAD_DIM), vec(DIFF_HEAD_DIM),
                  vec(DIFF_HEAD_DIM), vec(DIFF_HEAD_DIM), vec(dv)],
        out_specs=o_spec,
        out_shape=jax.ShapeDtypeStruct((batch * seq, DIFF_HEADS * dv), BF16),
        scratch_shapes=[pltpu.VMEM((2, dv, 2 * tq), BF16)] + _flash_scratch(seq, 2 * tq, tq, dv),
        compiler_params=_cparams("parallel", "parallel"),
        name="diff_flash",
    )(q, k, v, lq1.reshape(1, -1), lk1.reshape(1, -1), lq2.reshape(1, -1), lk2.reshape(1, -1),
      subln.reshape(1, -1))


def _post_attn_kernel(o_ref, x_ref, wo_ref, g1_ref, b1_ref, wu_ref, wd_ref, g2_ref, b2_ref,
                      y_ref, x1b_ref, acc_ref, *, n_chain):
    j = pl.program_id(1)
    last = pl.num_programs(1) - 1
    rc = acc_ref.shape[0] // n_chain

    def step(first, final):
        xb, base, hidden = {}, {}, {}

        def rows(c):
            return pl.ds(c * rc, rc)

        def stage0(c):
            if first:
                x1 = _layernorm(DEEPNORM_ALPHA * x_ref[rows(c), :] + _dot(o_ref[rows(c), :], wo_ref[...]),
                                g1_ref[...], b1_ref[...])
                xb[c] = x1.astype(BF16)
                x1b_ref[rows(c), :] = xb[c]
                base[c] = DEEPNORM_ALPHA * x1
            else:
                xb[c] = x1b_ref[rows(c), :]

        def stage1(c):
            hidden[c] = jnp.square(jnp.maximum(_dot(xb[c], wu_ref[...]), 0.0)).astype(BF16)

        def stage2(c):
            total = (base[c] if first else acc_ref[rows(c), :]) + _dot(hidden[c], wd_ref[...])
            if final:
                y_ref[rows(c), :] = _layernorm(total, g2_ref[...], b2_ref[...])
            else:
                acc_ref[rows(c), :] = total

        stages = (stage0, stage1, stage2)
        for slot in range(n_chain + len(stages) - 1):
            for depth, stage in enumerate(stages):
                if 0 <= slot - depth < n_chain:
                    stage(slot - depth)

    @pl.when(j == 0)
    def _():
        step(first=True, final=False)

    @pl.when((j > 0) & (j < last))
    def _():
        step(first=False, final=False)

    @pl.when(j == last)
    def _():
        step(first=False, final=True)


def _post_attn(o, x2d, w_o, g1, b1, w_up, w_down, g2, b2, tm, tf, n_chain):
    t = x2d.shape[0]
    row = lambda i, j: (i, 0)
    full = lambda i, j: (0, 0)
    vec = pl.BlockSpec((1, D_MODEL), full)
    assert D_FF // tf >= 2
    return pl.pallas_call(
        functools.partial(_post_attn_kernel, n_chain=n_chain),
        grid=(t // tm, D_FF // tf),
        in_specs=[
            pl.BlockSpec((tm, D_MODEL), row),
            pl.BlockSpec((tm, D_MODEL), row),
            pl.BlockSpec((D_MODEL, D_MODEL), full),
            vec, vec,
            pl.BlockSpec((D_MODEL, tf), lambda i, j: (0, j)),
            pl.BlockSpec((tf, D_MODEL), lambda i, j: (j, 0)),
            vec, vec,
        ],
        out_specs=pl.BlockSpec((tm, D_MODEL), row),
        out_shape=jax.ShapeDtypeStruct((t, D_MODEL), F32),
        scratch_shapes=[pltpu.VMEM((tm, D_MODEL), BF16), pltpu.VMEM((tm, D_MODEL), F32)],
        compiler_params=_cparams("parallel", "arbitrary"),
        name="post_attn",
    )(o, x2d, w_o.astype(BF16), g1.reshape(1, -1), b1.reshape(1, -1), w_up.astype(BF16),
      w_down.astype(BF16), g2.reshape(1, -1), b2.reshape(1, -1))


def kernel(x, positions, ln_mix_g, ln_mix_b, mla_w_dq, mla_q_norm, mla_w_uq, mla_w_dkv, mla_kv_norm, mla_w_ukv, mla_w_o, diff_w_qkv, diff_lambda_q1, diff_lambda_k1, diff_lambda_q2, diff_lambda_k2, diff_subln, diff_w_o, ln_ffn_g, ln_ffn_b, ffn_w_up, ffn_w_down):
    batch, seq, _ = x.shape
    x2d = x.reshape(batch * seq, D_MODEL)
    pos = positions.reshape(-1)

    mla_c, mla_s = _rope_tables(pos, MLA_ROPE // 2, LANES, 1, MLA_ROPE_THETA)
    diff_c, diff_s = _rope_tables(pos, DIFF_ROT_DIM // 2, DIFF_HEAD_DIM, 2, DIFF_ROPE_THETA)

    post_tiles = (POST_ROWS, FF_CHUNK, POST_CHAINS)
    q, k, v = _mla_proj(x2d, mla_c, mla_s, mla_w_dq[0], mla_q_norm[0], mla_w_uq[0],
                        mla_w_dkv[0], mla_kv_norm[0], mla_w_ukv[0], PROJ_ROWS)
    o = _mla_flash(q, k, v, batch, seq, ATTN_TILE)
    x2d = _post_attn(o, x2d, mla_w_o[0], ln_mix_g[0], ln_mix_b[0], ffn_w_up[0], ffn_w_down[0],
                     ln_ffn_g[0], ln_ffn_b[0], *post_tiles)

    lambda_init = 0.8 - 0.6 * math.exp(-0.3 * 1)
    q, k, v = _diff_proj(x2d, diff_c, diff_s, diff_w_qkv[0], PROJ_ROWS)
    o = _diff_flash(q, k, v, diff_lambda_q1[0], diff_lambda_k1[0], diff_lambda_q2[0],
                    diff_lambda_k2[0], diff_subln[0], lambda_init, batch, seq, ATTN_TILE)
    x2d = _post_attn(o, x2d, diff_w_o[0], ln_mix_g[1], ln_mix_b[1], ffn_w_up[1], ffn_w_down[1],
                     ln_ffn_g[1], ln_ffn_b[1], *post_tiles)
    return x2d.reshape(batch, seq, D_MODEL)
```

```python
import functools
import math

import jax
import jax.numpy as jnp
from jax import lax
from jax.experimental import pallas as pl
from jax.experimental.pallas import tpu as pltpu

D_MODEL = 1024
DEPTH = 2

MLA_NOPE = 128
MLA_ROPE = 64
MLA_V = 128
MLA_HEADS = 8
MLA_Q_LORA = 384
MLA_KV_LORA = 256
MLA_ROPE_THETA = 10000.0
MLA_QK_PAD = 256

DIFF_HEAD_DIM = 64
DIFF_HEADS = 8
DIFF_ROT_DIM = 16
DIFF_ROPE_THETA = 500000.0

D_FF = 4 * D_MODEL
DEEPNORM_ALPHA = (2 * DEPTH) ** 0.25
LN_EPS = 1e-5
RMS_EPS = 1e-6

LANES = 128
SUM_ROWS = 16
NEG_BIG = -1e30
LOG2_E = math.log2(math.e)
VMEM_LIMIT_BYTES = 48 * 1024 * 1024

F32 = jnp.float32
BF16 = jnp.bfloat16

PROJ_ROWS = 512
ATTN_TILE = 512
POST_ROWS = 1024
POST_CHAINS = 4
FF_CHUNK = 1024


def _cparams(*sem):
    return pltpu.CompilerParams(dimension_semantics=sem, vmem_limit_bytes=VMEM_LIMIT_BYTES)


def _dot(a, b):
    return jnp.dot(a, b, preferred_element_type=F32)


def _rms(x, g):
    return x * lax.rsqrt(jnp.mean(jnp.square(x), axis=-1, keepdims=True) + RMS_EPS) * g


def _layernorm(y, g, b):
    mu = jnp.mean(y, axis=-1, keepdims=True)
    d = y - mu
    var = jnp.mean(jnp.square(d), axis=-1, keepdims=True)
    return d * lax.rsqrt(var + LN_EPS) * g + b


def _rope_table_kernel(pos_ref, invf_ref, c_ref, s_ref, *, n_tok, half, dup, group):
    rows = pos_ref.shape[0]
    width = LANES // n_tok
    lane = lax.broadcasted_iota(jnp.int32, (rows, LANES), 1)
    pos = pos_ref[...]
    pos_dense = jnp.broadcast_to(pos[:, n_tok - 1:n_tok], (rows, LANES))
    for a in reversed(range(n_tok - 1)):
        pos_dense = jnp.where(lane < (a + 1) * width, pos[:, a:a + 1], pos_dense)
    ang = pos_dense.astype(F32) * invf_ref[...]
    cos_d = jnp.cos(ang)
    sin_d = jnp.sin(ang)
    neg_sin_d = -sin_d

    def spread(first_d, second_d, a, fill):
        rolled = {}

        def at(x_d, key, shift):
            if (key, shift) not in rolled:
                rolled[key, shift] = x_d if shift == 0 else pltpu.roll(x_d, shift, axis=1)
            return rolled[key, shift]

        out = jnp.full((rows, LANES), fill, F32)
        for g0 in range(0, LANES, group):
            for h, x_d in enumerate((first_d, second_d)):
                src = a * width + (h * half if dup == 2 else 0)
                dst = g0 + h * half
                val = at(x_d, h, (dst - src) % LANES)
                out = jnp.where((lane >= dst) & (lane < dst + half), val, out)
        return out

    for a in range(n_tok):
        c_ref[a] = spread(cos_d, cos_d, a, 1.0)
        s_ref[a] = spread(neg_sin_d, sin_d, a, 0.0)


def _rope_tables(positions, half, group, dup, theta, rows_per_step=512):
    t = positions.size
    n_tok = LANES // (half * dup)
    rows = t // n_tok
    assert rows % rows_per_step == 0
    inv_freq = theta ** (-jnp.arange(0, 2 * half, 2, dtype=F32) / (2 * half))
    invf = jnp.tile(inv_freq, LANES // half).reshape(1, LANES)
    pos_t = positions.reshape(n_tok, rows).T
    out_spec = pl.BlockSpec((n_tok, rows_per_step, LANES), lambda i: (0, i, 0))
    c, s = pl.pallas_call(
        functools.partial(_rope_table_kernel, n_tok=n_tok, half=half, dup=dup, group=group),
        grid=(rows // rows_per_step,),
        in_specs=[pl.BlockSpec((rows_per_step, n_tok), lambda i: (i, 0)),
                  pl.BlockSpec((1, LANES), lambda i: (0, 0))],
        out_specs=[out_spec, out_spec],
        out_shape=[jax.ShapeDtypeStruct((n_tok, rows, LANES), F32)] * 2,
        compiler_params=_cparams("parallel"),
        name="rope_table",
    )(pos_t, invf)
    return c.reshape(t, LANES), s.reshape(t, LANES)


def _rotate_half_in_lanes(x, half, group, c_tab, s_tab):
    lane = lax.broadcasted_iota(jnp.int32, x.shape, 1)
    from_above = pltpu.roll(x, LANES - half, axis=1)
    from_below = pltpu.roll(x, half, axis=1)
    partner = jnp.where(lane % group < half, from_above, from_below)
    return x * c_tab + partner * s_tab


def _mla_proj_kernel(x_ref, wdown_ref, qn_ref, wuq_ref, kvn_ref, wukv_ref,
                     c_ref, s_ref, q_ref, k_ref, v_ref, *, scale):
    xb = x_ref[...].astype(BF16)
    rope = functools.partial(_rotate_half_in_lanes, half=MLA_ROPE // 2, group=LANES,
                             c_tab=c_ref[...], s_tab=s_ref[...])

    down = _dot(xb, wdown_ref[...])
    cq = _rms(down[:, :MLA_Q_LORA], qn_ref[...]).astype(BF16)
    ckv = down[:, MLA_Q_LORA:]
    c_kv = _rms(ckv[:, :MLA_KV_LORA], kvn_ref[...]).astype(BF16)
    q = _dot(cq, wuq_ref[...])
    k_pe = rope(ckv[:, MLA_KV_LORA:]).astype(BF16)
    kv = _dot(c_kv, wukv_ref[...])

    for h in range(MLA_HEADS):
        lo, mid, hi = h * MLA_QK_PAD, h * MLA_QK_PAD + LANES, (h + 1) * MLA_QK_PAD
        q_ref[:, lo:mid] = (q[:, lo:mid] * scale).astype(BF16)
        q_ref[:, mid:hi] = (rope(q[:, mid:hi]) * scale).astype(BF16)
        k_ref[:, lo:mid] = kv[:, lo:mid].astype(BF16)
        k_ref[:, mid:hi] = k_pe
        v_ref[:, h * MLA_V:(h + 1) * MLA_V] = kv[:, mid:hi].astype(BF16)


def _mla_proj(x2d, c_tab, s_tab, w_dq, q_norm, w_uq, w_dkv, kv_norm, w_ukv, tm):
    t = x2d.shape[0]
    h = MLA_HEADS
    pad = MLA_QK_PAD - MLA_NOPE - MLA_ROPE
    wuq = jnp.pad(w_uq.reshape(MLA_Q_LORA, h, MLA_NOPE + MLA_ROPE), ((0, 0), (0, 0), (0, pad)))
    wuq = wuq.reshape(MLA_Q_LORA, h * MLA_QK_PAD).astype(BF16)
    wdown = jnp.concatenate([w_dq, jnp.pad(w_dkv, ((0, 0), (0, pad)))], axis=1).astype(BF16)
    row = lambda i: (i, 0)
    full = lambda i: (0, 0)
    return pl.pallas_call(
        functools.partial(_mla_proj_kernel, scale=LOG2_E * (MLA_NOPE + MLA_ROPE) ** -0.5),
        grid=(t // tm,),
        in_specs=[
            pl.BlockSpec((tm, D_MODEL), row),
            pl.BlockSpec((D_MODEL, MLA_Q_LORA + MLA_KV_LORA + LANES), full),
            pl.BlockSpec((1, MLA_Q_LORA), full),
            pl.BlockSpec((MLA_Q_LORA, h * MLA_QK_PAD), full),
            pl.BlockSpec((1, MLA_KV_LORA), full),
            pl.BlockSpec((MLA_KV_LORA, h * (MLA_NOPE + MLA_V)), full),
            pl.BlockSpec((tm, LANES), row),
            pl.BlockSpec((tm, LANES), row),
        ],
        out_specs=[
            pl.BlockSpec((tm, h * MLA_QK_PAD), row),
            pl.BlockSpec((tm, h * MLA_QK_PAD), row),
            pl.BlockSpec((tm, h * MLA_V), row),
        ],
        out_shape=[
            jax.ShapeDtypeStruct((t, h * MLA_QK_PAD), BF16),
            jax.ShapeDtypeStruct((t, h * MLA_QK_PAD), BF16),
            jax.ShapeDtypeStruct((t, h * MLA_V), BF16),
        ],
        compiler_params=_cparams("parallel"),
        name="mla_proj",
    )(x2d, wdown, q_norm.reshape(1, -1), wuq, kv_norm.reshape(1, -1), w_ukv.astype(BF16),
      c_tab, s_tab)


def _diff_proj_kernel(x_ref, w_ref, c_ref, s_ref, q_ref, k_ref, v_ref, *, scale):
    rope = functools.partial(_rotate_half_in_lanes, half=DIFF_ROT_DIM // 2, group=DIFF_HEAD_DIM,
                             c_tab=c_ref[...], s_tab=s_ref[...])
    qkv = _dot(x_ref[...].astype(BF16), w_ref[...])
    for j in range(D_MODEL // LANES):
        lo, hi = j * LANES, (j + 1) * LANES
        q_ref[:, lo:hi] = (rope(qkv[:, lo:hi]) * scale).astype(BF16)
        k_ref[:, lo:hi] = rope(qkv[:, D_MODEL + lo:D_MODEL + hi]).astype(BF16)
    v_ref[...] = qkv[:, 2 * D_MODEL:].astype(BF16)


def _diff_proj(x2d, c_tab, s_tab, w_qkv, tm):
    t = x2d.shape[0]
    row = lambda i: (i, 0)
    full = lambda i: (0, 0)
    return pl.pallas_call(
        functools.partial(_diff_proj_kernel, scale=LOG2_E * DIFF_HEAD_DIM ** -0.5),
        grid=(t // tm,),
        in_specs=[
            pl.BlockSpec((tm, D_MODEL), row),
            pl.BlockSpec((D_MODEL, 3 * D_MODEL), full),
            pl.BlockSpec((tm, LANES), row),
            pl.BlockSpec((tm, LANES), row),
        ],
        out_specs=[pl.BlockSpec((tm, D_MODEL), row)] * 3,
        out_shape=[jax.ShapeDtypeStruct((t, D_MODEL), BF16)] * 3,
        compiler_params=_cparams("parallel"),
        name="diff_proj",
    )(x2d, w_qkv.astype(BF16), c_tab, s_tab)


def _transpose_v(v_ref, vt_ref, tk):
    dv = v_ref.shape[1]
    for j in range(vt_ref.shape[0]):
        vt_ref[j, :dv, :] = v_ref[j * tk:(j + 1) * tk, :].astype(F32).T.astype(vt_ref.dtype)
        vt_ref[j, dv:, :] = jnp.ones((SUM_ROWS, tk), vt_ref.dtype)


def _flash_head(q_block, k_ref, vt_ref, s_refs, tri_ref, m_ref, acc_ref, finish, n_q, tq, fold):
    rows = fold * tq
    dv = acc_ref.shape[1] - SUM_ROWS
    half = tq // 2
    tri_ref[...] = jnp.where(lax.broadcasted_iota(jnp.int32, (half, half), 0)
                             <= lax.broadcasted_iota(jnp.int32, (half, half), 1), 0.0, NEG_BIG)
    steps = [(i, j) for i in range(n_q) for j in range(i + 1)]

    def scores(t):
        i, j = steps[t]
        k = k_ref[j * tq:(j + 1) * tq, :]
        s_refs[t % 2][...] = _dot(k, q_block(i, j == 0))

    def update(par, cols, s, vt, first):
        m_blk = jnp.max(s, axis=0, keepdims=True)
        if first:
            m_new = m_blk
            acc_ref[par, :, cols] = _dot(vt, jnp.exp2(s - m_new).astype(BF16))
        else:
            m_prev = m_ref[par, :, cols]
            m_new = jnp.maximum(m_prev, m_blk)
            alpha = jnp.exp2(m_prev - m_new)
            acc_ref[par, :, cols] = (alpha * acc_ref[par, :, cols]
                                     + _dot(vt, jnp.exp2(s - m_new).astype(BF16)))
        m_ref[par, :, cols] = m_new

    def consume(t):
        i, j = steps[t]
        s_ref = s_refs[t % 2]
        par = i % 2
        if j < i:
            update(par, slice(0, rows), s_ref[...], vt_ref[j], first=j == 0)
            return
        for g in range(rows // half):
            cols = slice(g * half, (g + 1) * half)
            if g % 2 == 0:
                s = s_ref[:half, cols] + tri_ref[...]
                vt = vt_ref[j, :, :half]
            else:
                s = jnp.concatenate([s_ref[:half, cols], s_ref[half:, cols] + tri_ref[...]], axis=0)
                vt = vt_ref[j]
            update(par, cols, s, vt, first=i == 0)
        finish(i, acc_ref[par, :dv, :], acc_ref[par, dv:dv + 1, :])

    scores(0)
    for t in range(len(steps)):
        if t + 1 < len(steps):
            scores(t + 1)
        consume(t)


def _mla_flash_kernel(q_ref, k_ref, v_ref, o_ref, qt_ref, vt_ref, s0_ref, s1_ref, tri_ref, m_ref, acc_ref,
                      *, n_q, tq):
    _transpose_v(v_ref, vt_ref, tq)

    def q_block(i, first):
        if first:
            qt_ref[i % 2] = q_ref[i * tq:(i + 1) * tq, :].astype(F32).T.astype(BF16)
        return qt_ref[i % 2]

    def finish(i, acc, l):
        o_ref[i * tq:(i + 1) * tq, :] = (acc / l).T.astype(o_ref.dtype)

    _flash_head(q_block, k_ref, vt_ref, (s0_ref, s1_ref), tri_ref, m_ref, acc_ref, finish,
                n_q, tq, fold=1)


def _diff_flash_kernel(q_ref, k_ref, v_ref, lq1_ref, lk1_ref, lq2_ref, lk2_ref, subln_ref,
                       o_ref, qs_ref, vt_ref, s0_ref, s1_ref, tri_ref, m_ref, acc_ref,
                       *, n_q, tq, lambda_init):
    _transpose_v(v_ref, vt_ref, tq)
    lam = (jnp.exp(jnp.sum(lq1_ref[...] * lk1_ref[...]))
           - jnp.exp(jnp.sum(lq2_ref[...] * lk2_ref[...])) + lambda_init)
    subln = subln_ref[...]

    def q_block(i, first):
        if first:
            q = q_ref[i * tq:(i + 1) * tq, :]
            lane = lax.broadcasted_iota(jnp.int32, q.shape, 1)
            zero = jnp.zeros_like(q)
            qs_ref[i % 2, :, :tq] = jnp.where(lane < DIFF_HEAD_DIM, q, zero).astype(F32).T.astype(BF16)
            qs_ref[i % 2, :, tq:] = jnp.where(lane < DIFF_HEAD_DIM, zero, q).astype(F32).T.astype(BF16)
        return qs_ref[i % 2]

    def finish(i, acc, l):
        o_t = acc[:, :tq] / l[:, :tq] - lam * (acc[:, tq:] / l[:, tq:])
        o = _rms(o_t.T, subln) * (1.0 - lambda_init)
        o_ref[i * tq:(i + 1) * tq, :] = o.astype(o_ref.dtype)

    _flash_head(q_block, k_ref, vt_ref, (s0_ref, s1_ref), tri_ref, m_ref, acc_ref, finish,
                n_q, tq, fold=2)


def _flash_scratch(seq, rows, tq, dv):
    return [pltpu.VMEM((seq // tq, dv + SUM_ROWS, tq), BF16),
            pltpu.VMEM((tq, rows), F32), pltpu.VMEM((tq, rows), F32),
            pltpu.VMEM((tq // 2, tq // 2), F32),
            pltpu.VMEM((2, 1, rows), F32),
            pltpu.VMEM((2, dv + SUM_ROWS, rows), F32)]


def _flash_specs(seq, dk, dv):
    head = lambda width: pl.BlockSpec((seq, width), lambda b, h: (b, h))
    return head(dk), head(dk), head(dv), head(dv)


def _mla_flash(q, k, v, batch, seq, tq):
    q_spec, k_spec, v_spec, o_spec = _flash_specs(seq, MLA_QK_PAD, MLA_V)
    return pl.pallas_call(
        functools.partial(_mla_flash_kernel, n_q=seq // tq, tq=tq),
        grid=(batch, MLA_HEADS),
        in_specs=[q_spec, k_spec, v_spec],
        out_specs=o_spec,
        out_shape=jax.ShapeDtypeStruct((batch * seq, MLA_HEADS * MLA_V), BF16),
        scratch_shapes=[pltpu.VMEM((2, MLA_QK_PAD, tq), BF16)] + _flash_scratch(seq, tq, tq, MLA_V),
        compiler_params=_cparams("parallel", "parallel"),
        name="mla_flash",
    )(q, k, v)


def _diff_flash(q, k, v, lq1, lk1, lq2, lk2, subln, lambda_init, batch, seq, tq):
    dv = 2 * DIFF_HEAD_DIM
    q_spec, k_spec, v_spec, o_spec = _flash_specs(seq, dv, dv)
    vec = lambda n: pl.BlockSpec((1, n), lambda b, h: (0, 0))
    return pl.pallas_call(
        functools.partial(_diff_flash_kernel, n_q=seq // tq, tq=tq, lambda_init=lambda_init),
        grid=(batch, DIFF_HEADS),
        in_specs=[q_spec, k_spec, v_spec, vec(DIFF_HEAD_DIM), vec(DIFF_HEAD_DIM),
                  vec(DIFF_HEAD_DIM), vec(DIFF_HEAD_DIM), vec(dv)],
        out_specs=o_spec,
        out_shape=jax.ShapeDtypeStruct((batch * seq, DIFF_HEADS * dv), BF16),
        scratch_shapes=[pltpu.VMEM((2, dv, 2 * tq), BF16)] + _flash_scratch(seq, 2 * tq, tq, dv),
        compiler_params=_cparams("parallel", "parallel"),
        name="diff_flash",
    )(q, k, v, lq1.reshape(1, -1), lk1.reshape(1, -1), lq2.reshape(1, -1), lk2.reshape(1, -1),
      subln.reshape(1, -1))


def _post_attn_kernel(o_ref, x_ref, wo_ref, g1_ref, b1_ref, wu_ref, wd_ref, g2_ref, b2_ref,
                      y_ref, x1b_ref, acc_ref, *, n_chain):
    j = pl.program_id(1)
    last = pl.num_programs(1) - 1
    rc = acc_ref.shape[0] // n_chain

    def step(first, final):
        xb, base, hidden = {}, {}, {}

        def rows(c):
            return pl.ds(c * rc, rc)

        def stage0(c):
            if first:
                x1 = _layernorm(DEEPNORM_ALPHA * x_ref[rows(c), :] + _dot(o_ref[rows(c), :], wo_ref[...]),
                                g1_ref[...], b1_ref[...])
                xb[c] = x1.astype(BF16)
                x1b_ref[rows(c), :] = xb[c]
                base[c] = DEEPNORM_ALPHA * x1
            else:
                xb[c] = x1b_ref[rows(c), :]

        def stage1(c):
            hidden[c] = jnp.square(jnp.maximum(_dot(xb[c], wu_ref[...]), 0.0)).astype(BF16)

        def stage2(c):
            total = (base[c] if first else acc_ref[rows(c), :]) + _dot(hidden[c], wd_ref[...])
            if final:
                y_ref[rows(c), :] = _layernorm(total, g2_ref[...], b2_ref[...])
            else:
                acc_ref[rows(c), :] = total

        stages = (stage0, stage1, stage2)
        for slot in range(n_chain + len(stages) - 1):
            for depth, stage in enumerate(stages):
                if 0 <= slot - depth < n_chain:
                    stage(slot - depth)

    @pl.when(j == 0)
    def _():
        step(first=True, final=False)

    @pl.when((j > 0) & (j < last))
    def _():
        step(first=False, final=False)

    @pl.when(j == last)
    def _():
        step(first=False, final=True)


def _post_attn(o, x2d, w_o, g1, b1, w_up_all, w_down_all, layer, g2, b2, tm, tf, n_chain):
    t = x2d.shape[0]
    row = lambda i, j: (i, 0)
    full = lambda i, j: (0, 0)
    vec = pl.BlockSpec((1, D_MODEL), full)
    assert D_FF // tf >= 2
    return pl.pallas_call(
        functools.partial(_post_attn_kernel, n_chain=n_chain),
        grid=(t // tm, D_FF // tf),
        in_specs=[
            pl.BlockSpec((tm, D_MODEL), row),
            pl.BlockSpec((tm, D_MODEL), row),
            pl.BlockSpec((D_MODEL, D_MODEL), full),
            vec, vec,
            pl.BlockSpec((None, D_MODEL, tf), lambda i, j: (layer, 0, j)),
            pl.BlockSpec((None, tf, D_MODEL), lambda i, j: (layer, j, 0)),
            vec, vec,
        ],
        out_specs=pl.BlockSpec((tm, D_MODEL), row),
        out_shape=jax.ShapeDtypeStruct((t, D_MODEL), F32),
        scratch_shapes=[pltpu.VMEM((tm, D_MODEL), BF16), pltpu.VMEM((tm, D_MODEL), F32)],
        compiler_params=_cparams("parallel", "arbitrary"),
        name="post_attn",
    )(o, x2d, w_o.astype(BF16), g1.reshape(1, -1), b1.reshape(1, -1), w_up_all, w_down_all,
      g2.reshape(1, -1), b2.reshape(1, -1))


def kernel(x, positions, ln_mix_g, ln_mix_b, mla_w_dq, mla_q_norm, mla_w_uq, mla_w_dkv, mla_kv_norm, mla_w_ukv, mla_w_o, diff_w_qkv, diff_lambda_q1, diff_lambda_k1, diff_lambda_q2, diff_lambda_k2, diff_subln, diff_w_o, ln_ffn_g, ln_ffn_b, ffn_w_up, ffn_w_down):
    batch, seq, _ = x.shape
    x2d = x.reshape(batch * seq, D_MODEL)
    pos = positions.reshape(-1)

    mla_c, mla_s = _rope_tables(pos, MLA_ROPE // 2, LANES, 1, MLA_ROPE_THETA)
    diff_c, diff_s = _rope_tables(pos, DIFF_ROT_DIM // 2, DIFF_HEAD_DIM, 2, DIFF_ROPE_THETA)

    post_tiles = (POST_ROWS, FF_CHUNK, POST_CHAINS)
    w_up_all = ffn_w_up.astype(BF16)
    w_down_all = ffn_w_down.astype(BF16)
    q, k, v = _mla_proj(x2d, mla_c, mla_s, mla_w_dq[0], mla_q_norm[0], mla_w_uq[0],
                        mla_w_dkv[0], mla_kv_norm[0], mla_w_ukv[0], PROJ_ROWS)
    o = _mla_flash(q, k, v, batch, seq, ATTN_TILE)
    x2d = _post_attn(o, x2d, mla_w_o[0], ln_mix_g[0], ln_mix_b[0], w_up_all, w_down_all, 0,
                     ln_ffn_g[0], ln_ffn_b[0], *post_tiles)

    lambda_init = 0.8 - 0.6 * math.exp(-0.3 * 1)
    q, k, v = _diff_proj(x2d, diff_c, diff_s, diff_w_qkv[0], PROJ_ROWS)
    o = _diff_flash(q, k, v, diff_lambda_q1[0], diff_lambda_k1[0], diff_lambda_q2[0],
                    diff_lambda_k2[0], diff_subln[0], lambda_init, batch, seq, ATTN_TILE)
    x2d = _post_attn(o, x2d, diff_w_o[0], ln_mix_g[1], ln_mix_b[1], w_up_all, w_down_all, 1,
                     ln_ffn_g[1], ln_ffn_b[1], *post_tiles)
    return x2d.reshape(batch, seq, D_MODEL)
```

```python
import functools
import math

import jax
import jax.numpy as jnp
from jax import lax
from jax.experimental import pallas as pl
from jax.experimental.pallas import tpu as pltpu

D_MODEL = 1024
DEPTH = 2

MLA_NOPE = 128
MLA_ROPE = 64
MLA_V = 128
MLA_HEADS = 8
MLA_Q_LORA = 384
MLA_KV_LORA = 256
MLA_ROPE_THETA = 10000.0
MLA_QK_PAD = 256

DIFF_HEAD_DIM = 64
DIFF_HEADS = 8
DIFF_ROT_DIM = 16
DIFF_ROPE_THETA = 500000.0

D_FF = 4 * D_MODEL
DEEPNORM_ALPHA = (2 * DEPTH) ** 0.25
LN_EPS = 1e-5
RMS_EPS = 1e-6

LANES = 128
SUM_ROWS = 16
NEG_BIG = -1e30
LOG2_E = math.log2(math.e)
VMEM_LIMIT_BYTES = 48 * 1024 * 1024

F32 = jnp.float32
BF16 = jnp.bfloat16

TABLE_ROWS = 512
PROJ_ROWS = 1024
ATTN_TILE = 512
MLA_HEADS_PER_STEP = 2
POST_ROWS = 1024
POST_CHAINS = 4
FF_CHUNK = 1024


def _cparams(*sem):
    return pltpu.CompilerParams(dimension_semantics=sem, vmem_limit_bytes=VMEM_LIMIT_BYTES)


def _dot(a, b):
    return jnp.dot(a, b, preferred_element_type=F32)


def _rms(x, g):
    return x * lax.rsqrt(jnp.mean(jnp.square(x), axis=-1, keepdims=True) + RMS_EPS) * g


def _layernorm(y, g, b):
    mu = jnp.mean(y, axis=-1, keepdims=True)
    d = y - mu
    var = jnp.mean(jnp.square(d), axis=-1, keepdims=True)
    return d * lax.rsqrt(var + LN_EPS) * g + b


def _rope_table_kernel(pos_ref, invf_ref, c_ref, s_ref, *, n_tok, half, dup, group):
    rows = pos_ref.shape[0]
    width = LANES // n_tok
    lane = lax.broadcasted_iota(jnp.int32, (rows, LANES), 1)
    pos = pos_ref[...]
    pos_dense = jnp.broadcast_to(pos[:, n_tok - 1:n_tok], (rows, LANES))
    for a in reversed(range(n_tok - 1)):
        pos_dense = jnp.where(lane < (a + 1) * width, pos[:, a:a + 1], pos_dense)
    ang = pos_dense.astype(F32) * invf_ref[...]
    cos_d = jnp.cos(ang)
    sin_d = jnp.sin(ang)
    neg_sin_d = -sin_d

    def spread(first_d, second_d, a, fill):
        rolled = {}

        def at(x_d, key, shift):
            if (key, shift) not in rolled:
                rolled[key, shift] = x_d if shift == 0 else pltpu.roll(x_d, shift, axis=1)
            return rolled[key, shift]

        out = jnp.full((rows, LANES), fill, F32)
        for g0 in range(0, LANES, group):
            for h, x_d in enumerate((first_d, second_d)):
                src = a * width + (h * half if dup == 2 else 0)
                dst = g0 + h * half
                val = at(x_d, h, (dst - src) % LANES)
                out = jnp.where((lane >= dst) & (lane < dst + half), val, out)
        return out

    for a in range(n_tok):
        c_ref[a] = spread(cos_d, cos_d, a, 1.0)
        s_ref[a] = spread(neg_sin_d, sin_d, a, 0.0)


def _rope_tables(positions, half, group, dup, theta, rows_per_step=TABLE_ROWS):
    t = positions.size
    n_tok = LANES // (half * dup)
    rows = t // n_tok
    assert rows % rows_per_step == 0
    inv_freq = theta ** (-jnp.arange(0, 2 * half, 2, dtype=F32) / (2 * half))
    invf = jnp.tile(inv_freq, LANES // half).reshape(1, LANES)
    pos_t = positions.reshape(n_tok, rows).T
    out_spec = pl.BlockSpec((n_tok, rows_per_step, LANES), lambda i: (0, i, 0))
    c, s = pl.pallas_call(
        functools.partial(_rope_table_kernel, n_tok=n_tok, half=half, dup=dup, group=group),
        grid=(rows // rows_per_step,),
        in_specs=[pl.BlockSpec((rows_per_step, n_tok), lambda i: (i, 0)),
                  pl.BlockSpec((1, LANES), lambda i: (0, 0))],
        out_specs=[out_spec, out_spec],
        out_shape=[jax.ShapeDtypeStruct((n_tok, rows, LANES), F32)] * 2,
        compiler_params=_cparams("parallel"),
        name="rope_table",
    )(pos_t, invf)
    return c.reshape(t, LANES), s.reshape(t, LANES)


def _rotate_half_in_lanes(x, half, group, c_tab, s_tab):
    lane = lax.broadcasted_iota(jnp.int32, x.shape, 1)
    from_above = pltpu.roll(x, LANES - half, axis=1)
    from_below = pltpu.roll(x, half, axis=1)
    partner = jnp.where(lane % group < half, from_above, from_below)
    return x * c_tab + partner * s_tab


def _mla_proj_kernel(x_ref, wdown_ref, qn_ref, wuq_ref, kvn_ref, wukv_ref,
                     c_ref, s_ref, q_ref, k_ref, v_ref, *, scale):
    xb = x_ref[...].astype(BF16)
    rope = functools.partial(_rotate_half_in_lanes, half=MLA_ROPE // 2, group=LANES,
                             c_tab=c_ref[...], s_tab=s_ref[...])

    down = _dot(xb, wdown_ref[...])
    cq = _rms(down[:, :MLA_Q_LORA], qn_ref[...]).astype(BF16)
    ckv = down[:, MLA_Q_LORA:]
    c_kv = _rms(ckv[:, :MLA_KV_LORA], kvn_ref[...]).astype(BF16)
    q = _dot(cq, wuq_ref[...])
    k_pe = rope(ckv[:, MLA_KV_LORA:]).astype(BF16)
    kv = _dot(c_kv, wukv_ref[...])

    for h in range(MLA_HEADS):
        lo, mid, hi = h * MLA_QK_PAD, h * MLA_QK_PAD + LANES, (h + 1) * MLA_QK_PAD
        q_ref[:, lo:mid] = (q[:, lo:mid] * scale).astype(BF16)
        q_ref[:, mid:hi] = (rope(q[:, mid:hi]) * scale).astype(BF16)
        k_ref[:, lo:mid] = kv[:, lo:mid].astype(BF16)
        k_ref[:, mid:hi] = k_pe
        v_ref[:, h * MLA_V:(h + 1) * MLA_V] = kv[:, mid:hi].astype(BF16)


def _mla_proj(x2d, c_tab, s_tab, w_dq, q_norm, w_uq, w_dkv, kv_norm, w_ukv, tm):
    t = x2d.shape[0]
    h = MLA_HEADS
    pad = MLA_QK_PAD - MLA_NOPE - MLA_ROPE
    wuq = jnp.pad(w_uq.reshape(MLA_Q_LORA, h, MLA_NOPE + MLA_ROPE), ((0, 0), (0, 0), (0, pad)))
    wuq = wuq.reshape(MLA_Q_LORA, h * MLA_QK_PAD).astype(BF16)
    wdown = jnp.concatenate([w_dq, jnp.pad(w_dkv, ((0, 0), (0, pad)))], axis=1).astype(BF16)
    row = lambda i: (i, 0)
    full = lambda i: (0, 0)
    return pl.pallas_call(
        functools.partial(_mla_proj_kernel, scale=LOG2_E * (MLA_NOPE + MLA_ROPE) ** -0.5),
        grid=(t // tm,),
        in_specs=[
            pl.BlockSpec((tm, D_MODEL), row),
            pl.BlockSpec((D_MODEL, MLA_Q_LORA + MLA_KV_LORA + LANES), full),
            pl.BlockSpec((1, MLA_Q_LORA), full),
            pl.BlockSpec((MLA_Q_LORA, h * MLA_QK_PAD), full),
            pl.BlockSpec((1, MLA_KV_LORA), full),
            pl.BlockSpec((MLA_KV_LORA, h * (MLA_NOPE + MLA_V)), full),
            pl.BlockSpec((tm, LANES), row),
            pl.BlockSpec((tm, LANES), row),
        ],
        out_specs=[
            pl.BlockSpec((tm, h * MLA_QK_PAD), row),
            pl.BlockSpec((tm, h * MLA_QK_PAD), row),
            pl.BlockSpec((tm, h * MLA_V), row),
        ],
        out_shape=[
            jax.ShapeDtypeStruct((t, h * MLA_QK_PAD), BF16),
            jax.ShapeDtypeStruct((t, h * MLA_QK_PAD), BF16),
            jax.ShapeDtypeStruct((t, h * MLA_V), BF16),
        ],
        compiler_params=_cparams("parallel"),
        name="mla_proj",
    )(x2d, wdown, q_norm.reshape(1, -1), wuq, kv_norm.reshape(1, -1), w_ukv.astype(BF16),
      c_tab, s_tab)


def _diff_proj_kernel(x_ref, w_ref, c_ref, s_ref, q_ref, k_ref, v_ref, *, scale):
    rope = functools.partial(_rotate_half_in_lanes, half=DIFF_ROT_DIM // 2, group=DIFF_HEAD_DIM,
                             c_tab=c_ref[...], s_tab=s_ref[...])
    qkv = _dot(x_ref[...].astype(BF16), w_ref[...])
    for j in range(D_MODEL // LANES):
        lo, hi = j * LANES, (j + 1) * LANES
        q_ref[:, lo:hi] = (rope(qkv[:, lo:hi]) * scale).astype(BF16)
        k_ref[:, lo:hi] = rope(qkv[:, D_MODEL + lo:D_MODEL + hi]).astype(BF16)
    v_ref[...] = qkv[:, 2 * D_MODEL:].astype(BF16)


def _diff_proj(x2d, c_tab, s_tab, w_qkv, tm):
    t = x2d.shape[0]
    row = lambda i: (i, 0)
    full = lambda i: (0, 0)
    return pl.pallas_call(
        functools.partial(_diff_proj_kernel, scale=LOG2_E * DIFF_HEAD_DIM ** -0.5),
        grid=(t // tm,),
        in_specs=[
            pl.BlockSpec((tm, D_MODEL), row),
            pl.BlockSpec((D_MODEL, 3 * D_MODEL), full),
            pl.BlockSpec((tm, LANES), row),
            pl.BlockSpec((tm, LANES), row),
        ],
        out_specs=[pl.BlockSpec((tm, D_MODEL), row)] * 3,
        out_shape=[jax.ShapeDtypeStruct((t, D_MODEL), BF16)] * 3,
        compiler_params=_cparams("parallel"),
        name="diff_proj",
    )(x2d, w_qkv.astype(BF16), c_tab, s_tab)


def _transpose_v(v_ref, vt_ref, tk):
    dv = v_ref.shape[1]
    for j in range(vt_ref.shape[0]):
        vt_ref[j, :dv, :] = v_ref[j * tk:(j + 1) * tk, :].astype(F32).T.astype(vt_ref.dtype)
        vt_ref[j, dv:, :] = jnp.ones((SUM_ROWS, tk), vt_ref.dtype)


def _flash_head(q_block, k_ref, vt_ref, s_refs, tri_ref, m_ref, acc_ref, finish, n_q, tq):
    rows = tq
    dv = acc_ref.shape[1] - SUM_ROWS
    half = tq // 2
    tri_ref[...] = jnp.where(lax.broadcasted_iota(jnp.int32, (half, half), 0)
                             <= lax.broadcasted_iota(jnp.int32, (half, half), 1), 0.0, NEG_BIG)
    steps = [(i, j) for i in range(n_q) for j in range(i + 1)]

    def scores(t):
        i, j = steps[t]
        k = k_ref[j * tq:(j + 1) * tq, :]
        s_refs[t % 2][...] = _dot(k, q_block(i, j == 0))

    def update(par, cols, s, vt, first):
        m_blk = jnp.max(s, axis=0, keepdims=True)
        if first:
            m_new = m_blk
            acc_ref[par, :, cols] = _dot(vt, jnp.exp2(s - m_new).astype(BF16))
        else:
            m_prev = m_ref[par, :, cols]
            m_new = jnp.maximum(m_prev, m_blk)
            alpha = jnp.exp2(m_prev - m_new)
            acc_ref[par, :, cols] = (alpha * acc_ref[par, :, cols]
                                     + _dot(vt, jnp.exp2(s - m_new).astype(BF16)))
        m_ref[par, :, cols] = m_new

    def consume(t):
        i, j = steps[t]
        s_ref = s_refs[t % 2]
        par = i % 2
        if j < i:
            update(par, slice(0, rows), s_ref[...], vt_ref[j], first=j == 0)
            return
        for g in range(rows // half):
            cols = slice(g * half, (g + 1) * half)
            if g % 2 == 0:
                s = s_ref[:half, cols] + tri_ref[...]
                vt = vt_ref[j, :, :half]
            else:
                s = jnp.concatenate([s_ref[:half, cols], s_ref[half:, cols] + tri_ref[...]], axis=0)
                vt = vt_ref[j]
            update(par, cols, s, vt, first=i == 0)
        finish(i, acc_ref[par, :dv, :], acc_ref[par, dv:dv + 1, :])

    return len(steps), scores, consume


def _run_interleaved(programs):
    n_steps = programs[0][0]
    for _, scores, _ in programs:
        scores(0)
    for t in range(n_steps):
        for _, scores, consume in programs:
            if t + 1 < n_steps:
                scores(t + 1)
            consume(t)


def _mla_flash_kernel(q_ref, k_ref, v_ref, o_ref, *scratch, n_q, tq, n_heads):
    per_head = len(scratch) // n_heads
    programs = []
    for hh in range(n_heads):
        qt_ref, vt_ref, s0_ref, s1_ref, tri_ref, m_ref, acc_ref = scratch[hh * per_head:(hh + 1) * per_head]
        q_h = q_ref.at[:, hh * MLA_QK_PAD:(hh + 1) * MLA_QK_PAD]
        k_h = k_ref.at[:, hh * MLA_QK_PAD:(hh + 1) * MLA_QK_PAD]
        v_h = v_ref.at[:, hh * MLA_V:(hh + 1) * MLA_V]
        o_h = o_ref.at[:, hh * MLA_V:(hh + 1) * MLA_V]
        _transpose_v(v_h, vt_ref, tq)

        def q_block(i, first, q_h=q_h, qt_ref=qt_ref):
            if first:
                qt_ref[i % 2] = q_h[i * tq:(i + 1) * tq, :].astype(F32).T.astype(BF16)
            return qt_ref[i % 2]

        def finish(i, acc, l, o_h=o_h):
            o_h[i * tq:(i + 1) * tq, :] = (acc / l).T.astype(o_ref.dtype)

        programs.append(_flash_head(q_block, k_h, vt_ref, (s0_ref, s1_ref), tri_ref, m_ref, acc_ref,
                                    finish, n_q, tq))
    _run_interleaved(programs)


def _diff_flash_kernel(q_ref, k_ref, v_ref, lq1_ref, lk1_ref, lq2_ref, lk2_ref, subln_ref,
                       o_ref, vt_ref, *scratch, n_q, tq, lambda_init):
    _transpose_v(v_ref, vt_ref, tq)
    lam = (jnp.exp(jnp.sum(lq1_ref[...] * lk1_ref[...]))
           - jnp.exp(jnp.sum(lq2_ref[...] * lk2_ref[...])) + lambda_init)
    subln = subln_ref[...]
    per_map = len(scratch) // 2
    maps = [scratch[c * per_map:(c + 1) * per_map] for c in range(2)]
    acc1_ref = maps[0][-1]
    dv = v_ref.shape[1]

    programs = []
    for c, (qt_ref, s0_ref, s1_ref, tri_ref, m_ref, acc_ref) in enumerate(maps):
        def q_block(i, first, c=c, qt_ref=qt_ref):
            if first:
                q = q_ref[i * tq:(i + 1) * tq, :]
                lane = lax.broadcasted_iota(jnp.int32, q.shape, 1)
                keep = (lane < DIFF_HEAD_DIM) if c == 0 else (lane >= DIFF_HEAD_DIM)
                qt_ref[i % 2] = jnp.where(keep, q, jnp.zeros_like(q)).astype(F32).T.astype(BF16)
            return qt_ref[i % 2]

        def finish(i, acc, l, c=c):
            if c == 0:
                return
            par = i % 2
            o1_t = acc1_ref[par, :dv, :] / acc1_ref[par, dv:dv + 1, :]
            o_t = o1_t - lam * (acc / l)
            o = _rms(o_t.T, subln) * (1.0 - lambda_init)
            o_ref[i * tq:(i + 1) * tq, :] = o.astype(o_ref.dtype)

        programs.append(_flash_head(q_block, k_ref, vt_ref, (s0_ref, s1_ref), tri_ref, m_ref,
                                    acc_ref, finish, n_q, tq))
    _run_interleaved(programs)


def _flash_scratch(seq, tq, dv):
    return [pltpu.VMEM((seq // tq, dv + SUM_ROWS, tq), BF16),
            pltpu.VMEM((tq, tq), F32), pltpu.VMEM((tq, tq), F32),
            pltpu.VMEM((tq // 2, tq // 2), F32),
            pltpu.VMEM((2, 1, tq), F32),
            pltpu.VMEM((2, dv + SUM_ROWS, tq), F32)]


def _flash_specs(seq, dk, dv):
    head = lambda width: pl.BlockSpec((seq, width), lambda b, h: (b, h))
    return head(dk), head(dk), head(dv), head(dv)


def _mla_flash(q, k, v, batch, seq, tq, n_heads):
    heads = lambda width: pl.BlockSpec((seq, n_heads * width), lambda b, h: (b, h))
    scratch = [pltpu.VMEM((2, MLA_QK_PAD, tq), BF16)] + _flash_scratch(seq, tq, MLA_V)
    return pl.pallas_call(
        functools.partial(_mla_flash_kernel, n_q=seq // tq, tq=tq, n_heads=n_heads),
        grid=(batch, MLA_HEADS // n_heads),
        in_specs=[heads(MLA_QK_PAD), heads(MLA_QK_PAD), heads(MLA_V)],
        out_specs=heads(MLA_V),
        out_shape=jax.ShapeDtypeStruct((batch * seq, MLA_HEADS * MLA_V), BF16),
        scratch_shapes=scratch * n_heads,
        compiler_params=_cparams("parallel", "parallel"),
        name="mla_flash",
    )(q, k, v)


def _diff_flash(q, k, v, lq1, lk1, lq2, lk2, subln, lambda_init, batch, seq, tq):
    dv = 2 * DIFF_HEAD_DIM
    q_spec, k_spec, v_spec, o_spec = _flash_specs(seq, dv, dv)
    vec = lambda n: pl.BlockSpec((1, n), lambda b, h: (0, 0))
    return pl.pallas_call(
        functools.partial(_diff_flash_kernel, n_q=seq // tq, tq=tq, lambda_init=lambda_init),
        grid=(batch, DIFF_HEADS),
        in_specs=[q_spec, k_spec, v_spec, vec(DIFF_HEAD_DIM), vec(DIFF_HEAD_DIM),
                  vec(DIFF_HEAD_DIM), vec(DIFF_HEAD_DIM), vec(dv)],
        out_specs=o_spec,
        out_shape=jax.ShapeDtypeStruct((batch * seq, DIFF_HEADS * dv), BF16),
        scratch_shapes=_flash_scratch(seq, tq, dv)[:1]
        + 2 * ([pltpu.VMEM((2, dv, tq), BF16)] + _flash_scratch(seq, tq, dv)[1:]),
        compiler_params=_cparams("parallel", "parallel"),
        name="diff_flash",
    )(q, k, v, lq1.reshape(1, -1), lk1.reshape(1, -1), lq2.reshape(1, -1), lk2.reshape(1, -1),
      subln.reshape(1, -1))


def _post_attn_kernel(o_ref, x_ref, wo_ref, g1_ref, b1_ref, wu_ref, wd_ref, g2_ref, b2_ref,
                      y_ref, x1b_ref, acc_ref, *, n_chain):
    j = pl.program_id(1)
    last = pl.num_programs(1) - 1
    rc = acc_ref.shape[0] // n_chain

    def step(first, final):
        xb, base, hidden = {}, {}, {}

        def rows(c):
            return pl.ds(c * rc, rc)

        def stage0(c):
            if first:
                x1 = _layernorm(DEEPNORM_ALPHA * x_ref[rows(c), :] + _dot(o_ref[rows(c), :], wo_ref[...]),
                                g1_ref[...], b1_ref[...])
                xb[c] = x1.astype(BF16)
                x1b_ref[rows(c), :] = xb[c]
                base[c] = DEEPNORM_ALPHA * x1
            else:
                xb[c] = x1b_ref[rows(c), :]

        def stage1(c):
            hidden[c] = jnp.square(jnp.maximum(_dot(xb[c], wu_ref[...]), 0.0)).astype(BF16)

        def stage2(c):
            total = (base[c] if first else acc_ref[rows(c), :]) + _dot(hidden[c], wd_ref[...])
            if final:
                y_ref[rows(c), :] = _layernorm(total, g2_ref[...], b2_ref[...])
            else:
                acc_ref[rows(c), :] = total

        stages = (stage0, stage1, stage2)
        for slot in range(n_chain + len(stages) - 1):
            for depth, stage in enumerate(stages):
                if 0 <= slot - depth < n_chain:
                    stage(slot - depth)

    @pl.when(j == 0)
    def _():
        step(first=True, final=False)

    @pl.when((j > 0) & (j < last))
    def _():
        step(first=False, final=False)

    @pl.when(j == last)
    def _():
        step(first=False, final=True)


def _post_attn(o, x2d, w_o, g1, b1, w_up_all, w_down_all, layer, g2, b2, tm, tf, n_chain):
    t = x2d.shape[0]
    row = lambda i, j: (i, 0)
    full = lambda i, j: (0, 0)
    vec = pl.BlockSpec((1, D_MODEL), full)
    assert D_FF // tf >= 2
    return pl.pallas_call(
        functools.partial(_post_attn_kernel, n_chain=n_chain),
        grid=(t // tm, D_FF // tf),
        in_specs=[
            pl.BlockSpec((tm, D_MODEL), row),
            pl.BlockSpec((tm, D_MODEL), row),
            pl.BlockSpec((D_MODEL, D_MODEL), full),
            vec, vec,
            pl.BlockSpec((None, D_MODEL, tf), lambda i, j: (layer, 0, j)),
            pl.BlockSpec((None, tf, D_MODEL), lambda i, j: (layer, j, 0)),
            vec, vec,
        ],
        out_specs=pl.BlockSpec((tm, D_MODEL), row),
        out_shape=jax.ShapeDtypeStruct((t, D_MODEL), F32),
        scratch_shapes=[pltpu.VMEM((tm, D_MODEL), BF16), pltpu.VMEM((tm, D_MODEL), F32)],
        compiler_params=_cparams("parallel", "arbitrary"),
        name="post_attn",
    )(o, x2d, w_o.astype(BF16), g1.reshape(1, -1), b1.reshape(1, -1), w_up_all, w_down_all,
      g2.reshape(1, -1), b2.reshape(1, -1))


def kernel(x, positions, ln_mix_g, ln_mix_b, mla_w_dq, mla_q_norm, mla_w_uq, mla_w_dkv, mla_kv_norm, mla_w_ukv, mla_w_o, diff_w_qkv, diff_lambda_q1, diff_lambda_k1, diff_lambda_q2, diff_lambda_k2, diff_subln, diff_w_o, ln_ffn_g, ln_ffn_b, ffn_w_up, ffn_w_down):
    batch, seq, d_model = x.shape
    assert d_model == D_MODEL and seq % ATTN_TILE == 0
    assert (batch * seq) % POST_ROWS == 0 and (batch * seq) % PROJ_ROWS == 0
    x2d = x.reshape(batch * seq, D_MODEL)
    pos = positions.reshape(-1)

    mla_c, mla_s = _rope_tables(pos, MLA_ROPE // 2, LANES, 1, MLA_ROPE_THETA)
    diff_c, diff_s = _rope_tables(pos, DIFF_ROT_DIM // 2, DIFF_HEAD_DIM, 2, DIFF_ROPE_THETA)

    post_tiles = (POST_ROWS, FF_CHUNK, POST_CHAINS)
    w_up_all = ffn_w_up.astype(BF16)
    w_down_all = ffn_w_down.astype(BF16)
    q, k, v = _mla_proj(x2d, mla_c, mla_s, mla_w_dq[0], mla_q_norm[0], mla_w_uq[0],
                        mla_w_dkv[0], mla_kv_norm[0], mla_w_ukv[0], PROJ_ROWS)
    o = _mla_flash(q, k, v, batch, seq, ATTN_TILE, MLA_HEADS_PER_STEP)
    x2d = _post_attn(o, x2d, mla_w_o[0], ln_mix_g[0], ln_mix_b[0], w_up_all, w_down_all, 0,
                     ln_ffn_g[0], ln_ffn_b[0], *post_tiles)

    lambda_init = 0.8 - 0.6 * math.exp(-0.3 * 1)
    q, k, v = _diff_proj(x2d, diff_c, diff_s, diff_w_qkv[0], PROJ_ROWS)
    o = _diff_flash(q, k, v, diff_lambda_q1[0], diff_lambda_k1[0], diff_lambda_q2[0],
                    diff_lambda_k2[0], diff_subln[0], lambda_init, batch, seq, ATTN_TILE)
    x2d = _post_attn(o, x2d, diff_w_o[0], ln_mix_g[1], ln_mix_b[1], w_up_all, w_down_all, 1,
                     ln_ffn_g[1], ln_ffn_b[1], *post_tiles)
    return x2d.reshape(batch, seq, D_MODEL)
```

```python
import functools
import math

import jax
import jax.numpy as jnp
from jax import lax
from jax.experimental import pallas as pl
from jax.experimental.pallas import tpu as pltpu

D_MODEL = 1024
DEPTH = 2

MLA_NOPE = 128
MLA_ROPE = 64
MLA_V = 128
MLA_HEADS = 8
MLA_Q_LORA = 384
MLA_KV_LORA = 256
MLA_ROPE_THETA = 10000.0
MLA_QK_PAD = 256

DIFF_HEAD_DIM = 64
DIFF_HEADS = 8
DIFF_ROT_DIM = 16
DIFF_ROPE_THETA = 500000.0

D_FF = 4 * D_MODEL
DEEPNORM_ALPHA = (2 * DEPTH) ** 0.25
LN_EPS = 1e-5
RMS_EPS = 1e-6

LANES = 128
SUM_ROWS = 16
NEG_BIG = -1e30
LOG2_E = math.log2(math.e)
VMEM_LIMIT_BYTES = 48 * 1024 * 1024

F32 = jnp.float32
BF16 = jnp.bfloat16

TABLE_ROWS = 512
PROJ_ROWS = 1024
ATTN_TILE = 512
SCORE_BUFFERS = 3
POST_ROWS = 1024
POST_CHAINS = 4
FF_CHUNK = 1024


def _cparams(*sem):
    return pltpu.CompilerParams(dimension_semantics=sem, vmem_limit_bytes=VMEM_LIMIT_BYTES)


def _dot(a, b):
    return jnp.dot(a, b, preferred_element_type=F32)


def _rms(x, g):
    return x * lax.rsqrt(jnp.mean(jnp.square(x), axis=-1, keepdims=True) + RMS_EPS) * g


def _layernorm(y, g, b):
    mu = jnp.mean(y, axis=-1, keepdims=True)
    d = y - mu
    var = jnp.mean(jnp.square(d), axis=-1, keepdims=True)
    return d * lax.rsqrt(var + LN_EPS) * g + b


def _rope_table_kernel(pos_ref, invf_ref, c_ref, s_ref, *, n_tok, half, dup, group):
    rows = pos_ref.shape[0]
    width = LANES // n_tok
    lane = lax.broadcasted_iota(jnp.int32, (rows, LANES), 1)
    pos = pos_ref[...]
    pos_dense = jnp.broadcast_to(pos[:, n_tok - 1:n_tok], (rows, LANES))
    for a in reversed(range(n_tok - 1)):
        pos_dense = jnp.where(lane < (a + 1) * width, pos[:, a:a + 1], pos_dense)
    ang = pos_dense.astype(F32) * invf_ref[...]
    cos_d = jnp.cos(ang)
    sin_d = jnp.sin(ang)
    neg_sin_d = -sin_d

    def spread(first_d, second_d, a, fill):
        rolled = {}

        def at(x_d, key, shift):
            if (key, shift) not in rolled:
                rolled[key, shift] = x_d if shift == 0 else pltpu.roll(x_d, shift, axis=1)
            return rolled[key, shift]

        out = jnp.full((rows, LANES), fill, F32)
        for g0 in range(0, LANES, group):
            for h, x_d in enumerate((first_d, second_d)):
                src = a * width + (h * half if dup == 2 else 0)
                dst = g0 + h * half
                val = at(x_d, h, (dst - src) % LANES)
                out = jnp.where((lane >= dst) & (lane < dst + half), val, out)
        return out

    for a in range(n_tok):
        c_ref[a] = spread(cos_d, cos_d, a, 1.0)
        s_ref[a] = spread(neg_sin_d, sin_d, a, 0.0)


def _rope_tables(positions, half, group, dup, theta, rows_per_step=TABLE_ROWS):
    t = positions.size
    n_tok = LANES // (half * dup)
    rows = t // n_tok
    assert rows % rows_per_step == 0
    inv_freq = theta ** (-jnp.arange(0, 2 * half, 2, dtype=F32) / (2 * half))
    invf = jnp.tile(inv_freq, LANES // half).reshape(1, LANES)
    pos_t = positions.reshape(n_tok, rows).T
    out_spec = pl.BlockSpec((n_tok, rows_per_step, LANES), lambda i: (0, i, 0))
    c, s = pl.pallas_call(
        functools.partial(_rope_table_kernel, n_tok=n_tok, half=half, dup=dup, group=group),
        grid=(rows // rows_per_step,),
        in_specs=[pl.BlockSpec((rows_per_step, n_tok), lambda i: (i, 0)),
                  pl.BlockSpec((1, LANES), lambda i: (0, 0))],
        out_specs=[out_spec, out_spec],
        out_shape=[jax.ShapeDtypeStruct((n_tok, rows, LANES), F32)] * 2,
        compiler_params=_cparams("parallel"),
        name="rope_table",
    )(pos_t, invf)
    return c.reshape(t, LANES), s.reshape(t, LANES)


def _rotate_half_in_lanes(x, half, group, c_tab, s_tab):
    lane = lax.broadcasted_iota(jnp.int32, x.shape, 1)
    from_above = pltpu.roll(x, LANES - half, axis=1)
    from_below = pltpu.roll(x, half, axis=1)
    partner = jnp.where(lane % group < half, from_above, from_below)
    return x * c_tab + partner * s_tab


def _mla_proj_kernel(x_ref, wdown_ref, qn_ref, wuq_ref, kvn_ref, wukv_ref,
                     c_ref, s_ref, q_ref, k_ref, v_ref, *, scale):
    xb = x_ref[...].astype(BF16)
    rope = functools.partial(_rotate_half_in_lanes, half=MLA_ROPE // 2, group=LANES,
                             c_tab=c_ref[...], s_tab=s_ref[...])

    down = _dot(xb, wdown_ref[...])
    cq = _rms(down[:, :MLA_Q_LORA], qn_ref[...]).astype(BF16)
    ckv = down[:, MLA_Q_LORA:]
    c_kv = _rms(ckv[:, :MLA_KV_LORA], kvn_ref[...]).astype(BF16)
    q = _dot(cq, wuq_ref[...])
    k_pe = rope(ckv[:, MLA_KV_LORA:]).astype(BF16)
    kv = _dot(c_kv, wukv_ref[...])

    for h in range(MLA_HEADS):
        lo, mid, hi = h * MLA_QK_PAD, h * MLA_QK_PAD + LANES, (h + 1) * MLA_QK_PAD
        q_ref[:, lo:mid] = (q[:, lo:mid] * scale).astype(BF16)
        q_ref[:, mid:hi] = (rope(q[:, mid:hi]) * scale).astype(BF16)
        k_ref[:, lo:mid] = kv[:, lo:mid].astype(BF16)
        k_ref[:, mid:hi] = k_pe
        v_ref[:, h * MLA_V:(h + 1) * MLA_V] = kv[:, mid:hi].astype(BF16)


def _mla_proj(x2d, c_tab, s_tab, w_dq, q_norm, w_uq, w_dkv, kv_norm, w_ukv, tm):
    t = x2d.shape[0]
    h = MLA_HEADS
    pad = MLA_QK_PAD - MLA_NOPE - MLA_ROPE
    wuq = jnp.pad(w_uq.reshape(MLA_Q_LORA, h, MLA_NOPE + MLA_ROPE), ((0, 0), (0, 0), (0, pad)))
    wuq = wuq.reshape(MLA_Q_LORA, h * MLA_QK_PAD).astype(BF16)
    wdown = jnp.concatenate([w_dq, jnp.pad(w_dkv, ((0, 0), (0, pad)))], axis=1).astype(BF16)
    row = lambda i: (i, 0)
    full = lambda i: (0, 0)
    return pl.pallas_call(
        functools.partial(_mla_proj_kernel, scale=LOG2_E * (MLA_NOPE + MLA_ROPE) ** -0.5),
        grid=(t // tm,),
        in_specs=[
            pl.BlockSpec((tm, D_MODEL), row),
            pl.BlockSpec((D_MODEL, MLA_Q_LORA + MLA_KV_LORA + LANES), full),
            pl.BlockSpec((1, MLA_Q_LORA), full),
            pl.BlockSpec((MLA_Q_LORA, h * MLA_QK_PAD), full),
            pl.BlockSpec((1, MLA_KV_LORA), full),
            pl.BlockSpec((MLA_KV_LORA, h * (MLA_NOPE + MLA_V)), full),
            pl.BlockSpec((tm, LANES), row),
            pl.BlockSpec((tm, LANES), row),
        ],
        out_specs=[
            pl.BlockSpec((tm, h * MLA_QK_PAD), row),
            pl.BlockSpec((tm, h * MLA_QK_PAD), row),
            pl.BlockSpec((tm, h * MLA_V), row),
        ],
        out_shape=[
            jax.ShapeDtypeStruct((t, h * MLA_QK_PAD), BF16),
            jax.ShapeDtypeStruct((t, h * MLA_QK_PAD), BF16),
            jax.ShapeDtypeStruct((t, h * MLA_V), BF16),
        ],
        compiler_params=_cparams("parallel"),
        name="mla_proj",
    )(x2d, wdown, q_norm.reshape(1, -1), wuq, kv_norm.reshape(1, -1), w_ukv.astype(BF16),
      c_tab, s_tab)


def _diff_proj_kernel(x_ref, w_ref, c_ref, s_ref, q_ref, k_ref, v_ref, *, scale):
    rope = functools.partial(_rotate_half_in_lanes, half=DIFF_ROT_DIM // 2, group=DIFF_HEAD_DIM,
                             c_tab=c_ref[...], s_tab=s_ref[...])
    qkv = _dot(x_ref[...].astype(BF16), w_ref[...])
    for j in range(D_MODEL // LANES):
        lo, hi = j * LANES, (j + 1) * LANES
        q_ref[:, lo:hi] = (rope(qkv[:, lo:hi]) * scale).astype(BF16)
        k_ref[:, lo:hi] = rope(qkv[:, D_MODEL + lo:D_MODEL + hi]).astype(BF16)
    v_ref[...] = qkv[:, 2 * D_MODEL:].astype(BF16)


def _diff_proj(x2d, c_tab, s_tab, w_qkv, tm):
    t = x2d.shape[0]
    row = lambda i: (i, 0)
    full = lambda i: (0, 0)
    return pl.pallas_call(
        functools.partial(_diff_proj_kernel, scale=LOG2_E * DIFF_HEAD_DIM ** -0.5),
        grid=(t // tm,),
        in_specs=[
            pl.BlockSpec((tm, D_MODEL), row),
            pl.BlockSpec((D_MODEL, 3 * D_MODEL), full),
            pl.BlockSpec((tm, LANES), row),
            pl.BlockSpec((tm, LANES), row),
        ],
        out_specs=[pl.BlockSpec((tm, D_MODEL), row)] * 3,
        out_shape=[jax.ShapeDtypeStruct((t, D_MODEL), BF16)] * 3,
        compiler_params=_cparams("parallel"),
        name="diff_proj",
    )(x2d, w_qkv.astype(BF16), c_tab, s_tab)


def _transpose_v(v_ref, vt_ref, tk):
    dv = v_ref.shape[1]
    for j in range(vt_ref.shape[0]):
        vt_ref[j, :dv, :] = v_ref[j * tk:(j + 1) * tk, :].astype(F32).T.astype(vt_ref.dtype)
        vt_ref[j, dv:, :] = jnp.ones((SUM_ROWS, tk), vt_ref.dtype)


def _flash_head(q_block, k_ref, vt_ref, s_ref, tri_ref, m_ref, acc_ref, finish, n_q, tq):
    rows = tq
    n_buf = s_ref.shape[0]
    dv = acc_ref.shape[1] - SUM_ROWS
    half = tq // 2
    tri_ref[...] = jnp.where(lax.broadcasted_iota(jnp.int32, (half, half), 0)
                             <= lax.broadcasted_iota(jnp.int32, (half, half), 1), 0.0, NEG_BIG)
    steps = [(i, j) for i in range(n_q) for j in range(i + 1)]

    def scores(t):
        i, j = steps[t]
        k = k_ref[j * tq:(j + 1) * tq, :]
        s_ref[t % n_buf] = _dot(k, q_block(i, j == 0))

    def update(par, cols, s, vt, first):
        m_blk = jnp.max(s, axis=0, keepdims=True)
        if first:
            m_new = m_blk
            acc_ref[par, :, cols] = _dot(vt, jnp.exp2(s - m_new).astype(BF16))
        else:
            m_prev = m_ref[par, :, cols]
            m_new = jnp.maximum(m_prev, m_blk)
            alpha = jnp.exp2(m_prev - m_new)
            acc_ref[par, :, cols] = (alpha * acc_ref[par, :, cols]
                                     + _dot(vt, jnp.exp2(s - m_new).astype(BF16)))
        m_ref[par, :, cols] = m_new

    def consume(t):
        i, j = steps[t]
        s_t = s_ref.at[t % n_buf]
        par = i % 2
        if j < i:
            update(par, slice(0, rows), s_t[...], vt_ref[j], first=j == 0)
            return
        for g in range(rows // half):
            cols = slice(g * half, (g + 1) * half)
            if g % 2 == 0:
                s = s_t[:half, cols] + tri_ref[...]
                vt = vt_ref[j, :, :half]
            else:
                s = jnp.concatenate([s_t[:half, cols], s_t[half:, cols] + tri_ref[...]], axis=0)
                vt = vt_ref[j]
            update(par, cols, s, vt, first=i == 0)
        finish(i, acc_ref[par, :dv, :], acc_ref[par, dv:dv + 1, :])

    return len(steps), n_buf - 1, scores, consume


def _run_interleaved(programs):
    n_steps, ahead = programs[0][:2]
    for _, _, scores, _ in programs:
        for t in range(min(ahead, n_steps)):
            scores(t)
    for t in range(n_steps):
        for _, _, scores, consume in programs:
            if t + ahead < n_steps:
                scores(t + ahead)
            consume(t)


def _mla_flash_kernel(q_ref, k_ref, v_ref, o_ref, qt_ref, vt_ref, s_ref, tri_ref, m_ref, acc_ref,
                      *, n_q, tq):
    _transpose_v(v_ref, vt_ref, tq)

    def q_block(i, first):
        if first:
            qt_ref[i % 2] = q_ref[i * tq:(i + 1) * tq, :].astype(F32).T.astype(BF16)
        return qt_ref[i % 2]

    def finish(i, acc, l):
        o_ref[i * tq:(i + 1) * tq, :] = (acc / l).T.astype(o_ref.dtype)

    _run_interleaved([_flash_head(q_block, k_ref, vt_ref, s_ref, tri_ref, m_ref, acc_ref,
                                  finish, n_q, tq)])


def _diff_flash_kernel(q_ref, k_ref, v_ref, lq1_ref, lk1_ref, lq2_ref, lk2_ref, subln_ref,
                       o_ref, vt_ref, *scratch, n_q, tq, lambda_init):
    _transpose_v(v_ref, vt_ref, tq)
    lam = (jnp.exp(jnp.sum(lq1_ref[...] * lk1_ref[...]))
           - jnp.exp(jnp.sum(lq2_ref[...] * lk2_ref[...])) + lambda_init)
    subln = subln_ref[...]
    per_map = len(scratch) // 2
    maps = [scratch[c * per_map:(c + 1) * per_map] for c in range(2)]
    acc1_ref = maps[0][-1]
    dv = v_ref.shape[1]

    programs = []
    for c, (qt_ref, s_ref, tri_ref, m_ref, acc_ref) in enumerate(maps):
        def q_block(i, first, c=c, qt_ref=qt_ref):
            if first:
                q = q_ref[i * tq:(i + 1) * tq, :]
                lane = lax.broadcasted_iota(jnp.int32, q.shape, 1)
                keep = (lane < DIFF_HEAD_DIM) if c == 0 else (lane >= DIFF_HEAD_DIM)
                qt_ref[i % 2] = jnp.where(keep, q, jnp.zeros_like(q)).astype(F32).T.astype(BF16)
            return qt_ref[i % 2]

        def finish(i, acc, l, c=c):
            if c == 0:
                return
            par = i % 2
            o1_t = acc1_ref[par, :dv, :] / acc1_ref[par, dv:dv + 1, :]
            o_t = o1_t - lam * (acc / l)
            o = _rms(o_t.T, subln) * (1.0 - lambda_init)
            o_ref[i * tq:(i + 1) * tq, :] = o.astype(o_ref.dtype)

        programs.append(_flash_head(q_block, k_ref, vt_ref, s_ref, tri_ref, m_ref, acc_ref,
                                    finish, n_q, tq))
    _run_interleaved(programs)


def _flash_scratch(seq, tq, dv):
    return [pltpu.VMEM((seq // tq, dv + SUM_ROWS, tq), BF16),
            pltpu.VMEM((SCORE_BUFFERS, tq, tq), F32),
            pltpu.VMEM((tq // 2, tq // 2), F32),
            pltpu.VMEM((2, 1, tq), F32),
            pltpu.VMEM((2, dv + SUM_ROWS, tq), F32)]


def _flash_specs(seq, dk, dv):
    head = lambda width: pl.BlockSpec((seq, width), lambda b, h: (b, h))
    return head(dk), head(dk), head(dv), head(dv)


def _mla_flash(q, k, v, batch, seq, tq):
    q_spec, k_spec, v_spec, o_spec = _flash_specs(seq, MLA_QK_PAD, MLA_V)
    return pl.pallas_call(
        functools.partial(_mla_flash_kernel, n_q=seq // tq, tq=tq),
        grid=(batch, MLA_HEADS),
        in_specs=[q_spec, k_spec, v_spec],
        out_specs=o_spec,
        out_shape=jax.ShapeDtypeStruct((batch * seq, MLA_HEADS * MLA_V), BF16),
        scratch_shapes=[pltpu.VMEM((2, MLA_QK_PAD, tq), BF16)] + _flash_scratch(seq, tq, MLA_V),
        compiler_params=_cparams("parallel", "parallel"),
        name="mla_flash",
    )(q, k, v)


def _diff_flash(q, k, v, lq1, lk1, lq2, lk2, subln, lambda_init, batch, seq, tq):
    dv = 2 * DIFF_HEAD_DIM
    q_spec, k_spec, v_spec, o_spec = _flash_specs(seq, dv, dv)
    vec = lambda n: pl.BlockSpec((1, n), lambda b, h: (0, 0))
    return pl.pallas_call(
        functools.partial(_diff_flash_kernel, n_q=seq // tq, tq=tq, lambda_init=lambda_init),
        grid=(batch, DIFF_HEADS),
        in_specs=[q_spec, k_spec, v_spec, vec(DIFF_HEAD_DIM), vec(DIFF_HEAD_DIM),
                  vec(DIFF_HEAD_DIM), vec(DIFF_HEAD_DIM), vec(dv)],
        out_specs=o_spec,
        out_shape=jax.ShapeDtypeStruct((batch * seq, DIFF_HEADS * dv), BF16),
        scratch_shapes=_flash_scratch(seq, tq, dv)[:1]
        + 2 * ([pltpu.VMEM((2, dv, tq), BF16)] + _flash_scratch(seq, tq, dv)[1:]),
        compiler_params=_cparams("parallel", "parallel"),
        name="diff_flash",
    )(q, k, v, lq1.reshape(1, -1), lk1.reshape(1, -1), lq2.reshape(1, -1), lk2.reshape(1, -1),
      subln.reshape(1, -1))


def _post_attn_kernel(o_ref, x_ref, wo_ref, g1_ref, b1_ref, wu_ref, wd_ref, g2_ref, b2_ref,
                      y_ref, x1b_ref, acc_ref, *, n_chain):
    j = pl.program_id(1)
    last = pl.num_programs(1) - 1
    rc = acc_ref.shape[0] // n_chain

    def step(first, final):
        xb, base, hidden = {}, {}, {}

        def rows(c):
            return pl.ds(c * rc, rc)

        def stage0(c):
            if first:
                x1 = _layernorm(DEEPNORM_ALPHA * x_ref[rows(c), :] + _dot(o_ref[rows(c), :], wo_ref[...]),
                                g1_ref[...], b1_ref[...])
                xb[c] = x1.astype(BF16)
                x1b_ref[rows(c), :] = xb[c]
                base[c] = DEEPNORM_ALPHA * x1
            else:
                xb[c] = x1b_ref[rows(c), :]

        def stage1(c):
            hidden[c] = jnp.square(jnp.maximum(_dot(xb[c], wu_ref[...]), 0.0)).astype(BF16)

        def stage2(c):
            total = (base[c] if first else acc_ref[rows(c), :]) + _dot(hidden[c], wd_ref[...])
            if final:
                y_ref[rows(c), :] = _layernorm(total, g2_ref[...], b2_ref[...])
            else:
                acc_ref[rows(c), :] = total

        stages = (stage0, stage1, stage2)
        for slot in range(n_chain + len(stages) - 1):
            for depth, stage in enumerate(stages):
                if 0 <= slot - depth < n_chain:
                    stage(slot - depth)

    @pl.when(j == 0)
    def _():
        step(first=True, final=False)

    @pl.when((j > 0) & (j < last))
    def _():
        step(first=False, final=False)

    @pl.when(j == last)
    def _():
        step(first=False, final=True)


def _post_attn(o, x2d, w_o, g1, b1, w_up_all, w_down_all, layer, g2, b2, tm, tf, n_chain):
    t = x2d.shape[0]
    row = lambda i, j: (i, 0)
    full = lambda i, j: (0, 0)
    vec = pl.BlockSpec((1, D_MODEL), full)
    assert D_FF // tf >= 2
    return pl.pallas_call(
        functools.partial(_post_attn_kernel, n_chain=n_chain),
        grid=(t // tm, D_FF // tf),
        in_specs=[
            pl.BlockSpec((tm, D_MODEL), row),
            pl.BlockSpec((tm, D_MODEL), row),
            pl.BlockSpec((D_MODEL, D_MODEL), full),
            vec, vec,
            pl.BlockSpec((None, D_MODEL, tf), lambda i, j: (layer, 0, j)),
            pl.BlockSpec((None, tf, D_MODEL), lambda i, j: (layer, j, 0)),
            vec, vec,
        ],
        out_specs=pl.BlockSpec((tm, D_MODEL), row),
        out_shape=jax.ShapeDtypeStruct((t, D_MODEL), F32),
        scratch_shapes=[pltpu.VMEM((tm, D_MODEL), BF16), pltpu.VMEM((tm, D_MODEL), F32)],
        compiler_params=_cparams("parallel", "arbitrary"),
        name="post_attn",
    )(o, x2d, w_o.astype(BF16), g1.reshape(1, -1), b1.reshape(1, -1), w_up_all, w_down_all,
      g2.reshape(1, -1), b2.reshape(1, -1))


def kernel(x, positions, ln_mix_g, ln_mix_b, mla_w_dq, mla_q_norm, mla_w_uq, mla_w_dkv, mla_kv_norm, mla_w_ukv, mla_w_o, diff_w_qkv, diff_lambda_q1, diff_lambda_k1, diff_lambda_q2, diff_lambda_k2, diff_subln, diff_w_o, ln_ffn_g, ln_ffn_b, ffn_w_up, ffn_w_down):
    batch, seq, d_model = x.shape
    assert d_model == D_MODEL and seq % ATTN_TILE == 0
    assert (batch * seq) % POST_ROWS == 0 and (batch * seq) % PROJ_ROWS == 0
    x2d = x.reshape(batch * seq, D_MODEL)
    pos = positions.reshape(-1)

    mla_c, mla_s = _rope_tables(pos, MLA_ROPE // 2, LANES, 1, MLA_ROPE_THETA)
    diff_c, diff_s = _rope_tables(pos, DIFF_ROT_DIM // 2, DIFF_HEAD_DIM, 2, DIFF_ROPE_THETA)

    post_tiles = (POST_ROWS, FF_CHUNK, POST_CHAINS)
    w_up_all = ffn_w_up.astype(BF16)
    w_down_all = ffn_w_down.astype(BF16)
    q, k, v = _mla_proj(x2d, mla_c, mla_s, mla_w_dq[0], mla_q_norm[0], mla_w_uq[0],
                        mla_w_dkv[0], mla_kv_norm[0], mla_w_ukv[0], PROJ_ROWS)
    o = _mla_flash(q, k, v, batch, seq, ATTN_TILE)
    x2d = _post_attn(o, x2d, mla_w_o[0], ln_mix_g[0], ln_mix_b[0], w_up_all, w_down_all, 0,
                     ln_ffn_g[0], ln_ffn_b[0], *post_tiles)

    lambda_init = 0.8 - 0.6 * math.exp(-0.3 * 1)
    q, k, v = _diff_proj(x2d, diff_c, diff_s, diff_w_qkv[0], PROJ_ROWS)
    o = _diff_flash(q, k, v, diff_lambda_q1[0], diff_lambda_k1[0], diff_lambda_q2[0],
                    diff_lambda_k2[0], diff_subln[0], lambda_init, batch, seq, ATTN_TILE)
    x2d = _post_attn(o, x2d, diff_w_o[0], ln_mix_g[1], ln_mix_b[1], w_up_all, w_down_all, 1,
                     ln_ffn_g[1], ln_ffn_b[1], *post_tiles)
    return x2d.reshape(batch, seq, D_MODEL)
```

```python
import functools
import math

import jax
import jax.numpy as jnp
from jax import lax
from jax.experimental import pallas as pl
from jax.experimental.pallas import tpu as pltpu

D_MODEL = 1024
DEPTH = 2

MLA_NOPE = 128
MLA_ROPE = 64
MLA_V = 128
MLA_HEADS = 8
MLA_Q_LORA = 384
MLA_KV_LORA = 256
MLA_ROPE_THETA = 10000.0
MLA_QK_PAD = 256

DIFF_HEAD_DIM = 64
DIFF_HEADS = 8
DIFF_ROT_DIM = 16
DIFF_ROPE_THETA = 500000.0

D_FF = 4 * D_MODEL
DEEPNORM_ALPHA = (2 * DEPTH) ** 0.25
LN_EPS = 1e-5
RMS_EPS = 1e-6

LANES = 128
SUM_ROWS = 16
NEG_BIG = -1e30
LOG2_E = math.log2(math.e)
VMEM_LIMIT_BYTES = 48 * 1024 * 1024

F32 = jnp.float32
BF16 = jnp.bfloat16

TABLE_ROWS = 512
PROJ_ROWS = 1024
ATTN_TILE = 512
MLA_SCORE_BUFFERS = 3
DIFF_SCORE_BUFFERS = 2
POST_ROWS = 1024
POST_CHAINS = 4
FF_CHUNK = 1024


def _cparams(*sem):
    return pltpu.CompilerParams(dimension_semantics=sem, vmem_limit_bytes=VMEM_LIMIT_BYTES)


def _dot(a, b):
    return jnp.dot(a, b, preferred_element_type=F32)


def _rms(x, g):
    return x * lax.rsqrt(jnp.mean(jnp.square(x), axis=-1, keepdims=True) + RMS_EPS) * g


def _layernorm(y, g, b):
    mu = jnp.mean(y, axis=-1, keepdims=True)
    d = y - mu
    var = jnp.mean(jnp.square(d), axis=-1, keepdims=True)
    return d * lax.rsqrt(var + LN_EPS) * g + b


def _rope_table_kernel(pos_ref, invf_ref, c_ref, s_ref, *, n_tok, half, dup, group):
    rows = pos_ref.shape[0]
    width = LANES // n_tok
    lane = lax.broadcasted_iota(jnp.int32, (rows, LANES), 1)
    pos = pos_ref[...]
    pos_dense = jnp.broadcast_to(pos[:, n_tok - 1:n_tok], (rows, LANES))
    for a in reversed(range(n_tok - 1)):
        pos_dense = jnp.where(lane < (a + 1) * width, pos[:, a:a + 1], pos_dense)
    ang = pos_dense.astype(F32) * invf_ref[...]
    cos_d = jnp.cos(ang)
    sin_d = jnp.sin(ang)
    neg_sin_d = -sin_d

    def spread(first_d, second_d, a, fill):
        rolled = {}

        def at(x_d, key, shift):
            if (key, shift) not in rolled:
                rolled[key, shift] = x_d if shift == 0 else pltpu.roll(x_d, shift, axis=1)
            return rolled[key, shift]

        out = jnp.full((rows, LANES), fill, F32)
        for g0 in range(0, LANES, group):
            for h, x_d in enumerate((first_d, second_d)):
                src = a * width + (h * half if dup == 2 else 0)
                dst = g0 + h * half
                val = at(x_d, h, (dst - src) % LANES)
                out = jnp.where((lane >= dst) & (lane < dst + half), val, out)
        return out

    for a in range(n_tok):
        c_ref[a] = spread(cos_d, cos_d, a, 1.0)
        s_ref[a] = spread(neg_sin_d, sin_d, a, 0.0)


def _rope_tables(positions, half, group, dup, theta, rows_per_step=TABLE_ROWS):
    t = positions.size
    n_tok = LANES // (half * dup)
    rows = t // n_tok
    assert rows % rows_per_step == 0
    inv_freq = theta ** (-jnp.arange(0, 2 * half, 2, dtype=F32) / (2 * half))
    invf = jnp.tile(inv_freq, LANES // half).reshape(1, LANES)
    pos_t = positions.reshape(n_tok, rows).T
    out_spec = pl.BlockSpec((n_tok, rows_per_step, LANES), lambda i: (0, i, 0))
    c, s = pl.pallas_call(
        functools.partial(_rope_table_kernel, n_tok=n_tok, half=half, dup=dup, group=group),
        grid=(rows // rows_per_step,),
        in_specs=[pl.BlockSpec((rows_per_step, n_tok), lambda i: (i, 0)),
                  pl.BlockSpec((1, LANES), lambda i: (0, 0))],
        out_specs=[out_spec, out_spec],
        out_shape=[jax.ShapeDtypeStruct((n_tok, rows, LANES), F32)] * 2,
        compiler_params=_cparams("parallel"),
        name="rope_table",
    )(pos_t, invf)
    return c.reshape(t, LANES), s.reshape(t, LANES)


def _rotate_half_in_lanes(x, half, group, c_tab, s_tab):
    lane = lax.broadcasted_iota(jnp.int32, x.shape, 1)
    from_above = pltpu.roll(x, LANES - half, axis=1)
    from_below = pltpu.roll(x, half, axis=1)
    partner = jnp.where(lane % group < half, from_above, from_below)
    return x * c_tab + partner * s_tab


def _mla_proj_kernel(x_ref, wdown_ref, qn_ref, wuq_ref, kvn_ref, wukv_ref,
                     c_ref, s_ref, q_ref, k_ref, v_ref, *, scale):
    xb = x_ref[...].astype(BF16)
    rope = functools.partial(_rotate_half_in_lanes, half=MLA_ROPE // 2, group=LANES,
                             c_tab=c_ref[...], s_tab=s_ref[...])

    down = _dot(xb, wdown_ref[...])
    cq = _rms(down[:, :MLA_Q_LORA], qn_ref[...]).astype(BF16)
    ckv = down[:, MLA_Q_LORA:]
    c_kv = _rms(ckv[:, :MLA_KV_LORA], kvn_ref[...]).astype(BF16)
    q = _dot(cq, wuq_ref[...])
    k_pe = rope(ckv[:, MLA_KV_LORA:]).astype(BF16)
    kv = _dot(c_kv, wukv_ref[...])

    for h in range(MLA_HEADS):
        lo, mid, hi = h * MLA_QK_PAD, h * MLA_QK_PAD + LANES, (h + 1) * MLA_QK_PAD
        q_ref[:, lo:mid] = (q[:, lo:mid] * scale).astype(BF16)
        q_ref[:, mid:hi] = (rope(q[:, mid:hi]) * scale).astype(BF16)
        k_ref[:, lo:mid] = kv[:, lo:mid].astype(BF16)
        k_ref[:, mid:hi] = k_pe
        v_ref[:, h * MLA_V:(h + 1) * MLA_V] = kv[:, mid:hi].astype(BF16)


def _mla_proj(x2d, c_tab, s_tab, w_dq, q_norm, w_uq, w_dkv, kv_norm, w_ukv, tm):
    t = x2d.shape[0]
    h = MLA_HEADS
    pad = MLA_QK_PAD - MLA_NOPE - MLA_ROPE
    wuq = jnp.pad(w_uq.reshape(MLA_Q_LORA, h, MLA_NOPE + MLA_ROPE), ((0, 0), (0, 0), (0, pad)))
    wuq = wuq.reshape(MLA_Q_LORA, h * MLA_QK_PAD).astype(BF16)
    wdown = jnp.concatenate([w_dq, jnp.pad(w_dkv, ((0, 0), (0, pad)))], axis=1).astype(BF16)
    row = lambda i: (i, 0)
    full = lambda i: (0, 0)
    return pl.pallas_call(
        functools.partial(_mla_proj_kernel, scale=LOG2_E * (MLA_NOPE + MLA_ROPE) ** -0.5),
        grid=(t // tm,),
        in_specs=[
            pl.BlockSpec((tm, D_MODEL), row),
            pl.BlockSpec((D_MODEL, MLA_Q_LORA + MLA_KV_LORA + LANES), full),
            pl.BlockSpec((1, MLA_Q_LORA), full),
            pl.BlockSpec((MLA_Q_LORA, h * MLA_QK_PAD), full),
            pl.BlockSpec((1, MLA_KV_LORA), full),
            pl.BlockSpec((MLA_KV_LORA, h * (MLA_NOPE + MLA_V)), full),
            pl.BlockSpec((tm, LANES), row),
            pl.BlockSpec((tm, LANES), row),
        ],
        out_specs=[
            pl.BlockSpec((tm, h * MLA_QK_PAD), row),
            pl.BlockSpec((tm, h * MLA_QK_PAD), row),
            pl.BlockSpec((tm, h * MLA_V), row),
        ],
        out_shape=[
            jax.ShapeDtypeStruct((t, h * MLA_QK_PAD), BF16),
            jax.ShapeDtypeStruct((t, h * MLA_QK_PAD), BF16),
            jax.ShapeDtypeStruct((t, h * MLA_V), BF16),
        ],
        compiler_params=_cparams("parallel"),
        name="mla_proj",
    )(x2d, wdown, q_norm.reshape(1, -1), wuq, kv_norm.reshape(1, -1), w_ukv.astype(BF16),
      c_tab, s_tab)


def _diff_proj_kernel(x_ref, w_ref, c_ref, s_ref, q_ref, k_ref, v_ref, *, scale):
    rope = functools.partial(_rotate_half_in_lanes, half=DIFF_ROT_DIM // 2, group=DIFF_HEAD_DIM,
                             c_tab=c_ref[...], s_tab=s_ref[...])
    qkv = _dot(x_ref[...].astype(BF16), w_ref[...])
    for j in range(D_MODEL // LANES):
        lo, hi = j * LANES, (j + 1) * LANES
        q_ref[:, lo:hi] = (rope(qkv[:, lo:hi]) * scale).astype(BF16)
        k_ref[:, lo:hi] = rope(qkv[:, D_MODEL + lo:D_MODEL + hi]).astype(BF16)
    v_ref[...] = qkv[:, 2 * D_MODEL:].astype(BF16)


def _diff_proj(x2d, c_tab, s_tab, w_qkv, tm):
    t = x2d.shape[0]
    row = lambda i: (i, 0)
    full = lambda i: (0, 0)
    return pl.pallas_call(
        functools.partial(_diff_proj_kernel, scale=LOG2_E * DIFF_HEAD_DIM ** -0.5),
        grid=(t // tm,),
        in_specs=[
            pl.BlockSpec((tm, D_MODEL), row),
            pl.BlockSpec((D_MODEL, 3 * D_MODEL), full),
            pl.BlockSpec((tm, LANES), row),
            pl.BlockSpec((tm, LANES), row),
        ],
        out_specs=[pl.BlockSpec((tm, D_MODEL), row)] * 3,
        out_shape=[jax.ShapeDtypeStruct((t, D_MODEL), BF16)] * 3,
        compiler_params=_cparams("parallel"),
        name="diff_proj",
    )(x2d, w_qkv.astype(BF16), c_tab, s_tab)


def _transpose_v(v_ref, vt_ref, tk):
    dv = v_ref.shape[1]
    for j in range(vt_ref.shape[0]):
        vt_ref[j, :dv, :] = v_ref[j * tk:(j + 1) * tk, :].astype(F32).T.astype(vt_ref.dtype)
        vt_ref[j, dv:, :] = jnp.ones((SUM_ROWS, tk), vt_ref.dtype)


def _flash_head(q_block, k_ref, vt_ref, s_ref, tri_ref, m_ref, acc_ref, finish, n_q, tq):
    rows = tq
    n_buf = s_ref.shape[0]
    dv = acc_ref.shape[1] - SUM_ROWS
    half = tq // 2
    tri_ref[...] = jnp.where(lax.broadcasted_iota(jnp.int32, (half, half), 0)
                             <= lax.broadcasted_iota(jnp.int32, (half, half), 1), 0.0, NEG_BIG)
    steps = [(i, j) for i in range(n_q) for j in range(i + 1)]

    def scores(t):
        i, j = steps[t]
        k = k_ref[j * tq:(j + 1) * tq, :]
        s_ref[t % n_buf] = _dot(k, q_block(i, j == 0))

    def update(par, cols, s, vt, first):
        m_blk = jnp.max(s, axis=0, keepdims=True)
        if first:
            m_new = m_blk
            acc_ref[par, :, cols] = _dot(vt, jnp.exp2(s - m_new).astype(BF16))
        else:
            m_prev = m_ref[par, :, cols]
            m_new = jnp.maximum(m_prev, m_blk)
            alpha = jnp.exp2(m_prev - m_new)
            acc_ref[par, :, cols] = (alpha * acc_ref[par, :, cols]
                                     + _dot(vt, jnp.exp2(s - m_new).astype(BF16)))
        m_ref[par, :, cols] = m_new

    def consume(t):
        i, j = steps[t]
        s_t = s_ref.at[t % n_buf]
        par = i % 2
        if j < i:
            update(par, slice(0, rows), s_t[...], vt_ref[j], first=j == 0)
            return
        for g in range(rows // half):
            cols = slice(g * half, (g + 1) * half)
            if g % 2 == 0:
                s = s_t[:half, cols] + tri_ref[...]
                vt = vt_ref[j, :, :half]
            else:
                s = jnp.concatenate([s_t[:half, cols], s_t[half:, cols] + tri_ref[...]], axis=0)
                vt = vt_ref[j]
            update(par, cols, s, vt, first=i == 0)
        finish(i, acc_ref[par, :dv, :], acc_ref[par, dv:dv + 1, :])

    return len(steps), n_buf - 1, scores, consume


def _run_interleaved(programs):
    n_steps, ahead = programs[0][:2]
    for _, _, scores, _ in programs:
        for t in range(min(ahead, n_steps)):
            scores(t)
    for t in range(n_steps):
        for _, _, scores, consume in programs:
            if t + ahead < n_steps:
                scores(t + ahead)
            consume(t)


def _mla_flash_kernel(q_ref, k_ref, v_ref, o_ref, qt_ref, vt_ref, s_ref, tri_ref, m_ref, acc_ref,
                      *, n_q, tq):
    _transpose_v(v_ref, vt_ref, tq)

    def q_block(i, first):
        if first:
            qt_ref[i % 2] = q_ref[i * tq:(i + 1) * tq, :].astype(F32).T.astype(BF16)
        return qt_ref[i % 2]

    def finish(i, acc, l):
        o_ref[i * tq:(i + 1) * tq, :] = (acc / l).T.astype(o_ref.dtype)

    _run_interleaved([_flash_head(q_block, k_ref, vt_ref, s_ref, tri_ref, m_ref, acc_ref,
                                  finish, n_q, tq)])


def _diff_flash_kernel(q_ref, k_ref, v_ref, lq1_ref, lk1_ref, lq2_ref, lk2_ref, subln_ref,
                       o_ref, vt_ref, *scratch, n_q, tq, lambda_init):
    _transpose_v(v_ref, vt_ref, tq)
    lam = (jnp.exp(jnp.sum(lq1_ref[...] * lk1_ref[...]))
           - jnp.exp(jnp.sum(lq2_ref[...] * lk2_ref[...])) + lambda_init)
    subln = subln_ref[...]
    per_map = len(scratch) // 2
    maps = [scratch[c * per_map:(c + 1) * per_map] for c in range(2)]
    acc1_ref = maps[0][-1]
    dv = v_ref.shape[1]

    programs = []
    for c, (qt_ref, s_ref, tri_ref, m_ref, acc_ref) in enumerate(maps):
        def q_block(i, first, c=c, qt_ref=qt_ref):
            if first:
                q = q_ref[i * tq:(i + 1) * tq, :]
                lane = lax.broadcasted_iota(jnp.int32, q.shape, 1)
                keep = (lane < DIFF_HEAD_DIM) if c == 0 else (lane >= DIFF_HEAD_DIM)
                qt_ref[i % 2] = jnp.where(keep, q, jnp.zeros_like(q)).astype(F32).T.astype(BF16)
            return qt_ref[i % 2]

        def finish(i, acc, l, c=c):
            if c == 0:
                return
            par = i % 2
            o1_t = acc1_ref[par, :dv, :] / acc1_ref[par, dv:dv + 1, :]
            o_t = o1_t - lam * (acc / l)
            o = _rms(o_t.T, subln) * (1.0 - lambda_init)
            o_ref[i * tq:(i + 1) * tq, :] = o.astype(o_ref.dtype)

        programs.append(_flash_head(q_block, k_ref, vt_ref, s_ref, tri_ref, m_ref, acc_ref,
                                    finish, n_q, tq))
    _run_interleaved(programs)


def _flash_scratch(seq, tq, dv, n_buf):
    return [pltpu.VMEM((seq // tq, dv + SUM_ROWS, tq), BF16),
            pltpu.VMEM((n_buf, tq, tq), F32),
            pltpu.VMEM((tq // 2, tq // 2), F32),
            pltpu.VMEM((2, 1, tq), F32),
            pltpu.VMEM((2, dv + SUM_ROWS, tq), F32)]


def _flash_specs(seq, dk, dv):
    head = lambda width: pl.BlockSpec((seq, width), lambda b, h: (b, h))
    return head(dk), head(dk), head(dv), head(dv)


def _mla_flash(q, k, v, batch, seq, tq):
    q_spec, k_spec, v_spec, o_spec = _flash_specs(seq, MLA_QK_PAD, MLA_V)
    return pl.pallas_call(
        functools.partial(_mla_flash_kernel, n_q=seq // tq, tq=tq),
        grid=(batch, MLA_HEADS),
        in_specs=[q_spec, k_spec, v_spec],
        out_specs=o_spec,
        out_shape=jax.ShapeDtypeStruct((batch * seq, MLA_HEADS * MLA_V), BF16),
        scratch_shapes=[pltpu.VMEM((2, MLA_QK_PAD, tq), BF16)]
        + _flash_scratch(seq, tq, MLA_V, MLA_SCORE_BUFFERS),
        compiler_params=_cparams("parallel", "parallel"),
        name="mla_flash",
    )(q, k, v)


def _diff_flash(q, k, v, lq1, lk1, lq2, lk2, subln, lambda_init, batch, seq, tq):
    dv = 2 * DIFF_HEAD_DIM
    q_spec, k_spec, v_spec, o_spec = _flash_specs(seq, dv, dv)
    vec = lambda n: pl.BlockSpec((1, n), lambda b, h: (0, 0))
    return pl.pallas_call(
        functools.partial(_diff_flash_kernel, n_q=seq // tq, tq=tq, lambda_init=lambda_init),
        grid=(batch, DIFF_HEADS),
        in_specs=[q_spec, k_spec, v_spec, vec(DIFF_HEAD_DIM), vec(DIFF_HEAD_DIM),
                  vec(DIFF_HEAD_DIM), vec(DIFF_HEAD_DIM), vec(dv)],
        out_specs=o_spec,
        out_shape=jax.ShapeDtypeStruct((batch * seq, DIFF_HEADS * dv), BF16),
        scratch_shapes=_flash_scratch(seq, tq, dv, DIFF_SCORE_BUFFERS)[:1]
        + 2 * ([pltpu.VMEM((2, dv, tq), BF16)]
               + _flash_scratch(seq, tq, dv, DIFF_SCORE_BUFFERS)[1:]),
        compiler_params=_cparams("parallel", "parallel"),
        name="diff_flash",
    )(q, k, v, lq1.reshape(1, -1), lk1.reshape(1, -1), lq2.reshape(1, -1), lk2.reshape(1, -1),
      subln.reshape(1, -1))


def _post_attn_kernel(o_ref, x_ref, wo_ref, g1_ref, b1_ref, wu_ref, wd_ref, g2_ref, b2_ref,
                      y_ref, x1b_ref, acc_ref, *, n_chain):
    j = pl.program_id(1)
    last = pl.num_programs(1) - 1
    rc = acc_ref.shape[0] // n_chain

    def step(first, final):
        xb, base, hidden = {}, {}, {}

        def rows(c):
            return pl.ds(c * rc, rc)

        def stage0(c):
            if first:
                x1 = _layernorm(DEEPNORM_ALPHA * x_ref[rows(c), :] + _dot(o_ref[rows(c), :], wo_ref[...]),
                                g1_ref[...], b1_ref[...])
                xb[c] = x1.astype(BF16)
                x1b_ref[rows(c), :] = xb[c]
                base[c] = DEEPNORM_ALPHA * x1
            else:
                xb[c] = x1b_ref[rows(c), :]

        def stage1(c):
            hidden[c] = jnp.square(jnp.maximum(_dot(xb[c], wu_ref[...]), 0.0)).astype(BF16)

        def stage2(c):
            total = (base[c] if first else acc_ref[rows(c), :]) + _dot(hidden[c], wd_ref[...])
            if final:
                y_ref[rows(c), :] = _layernorm(total, g2_ref[...], b2_ref[...])
            else:
                acc_ref[rows(c), :] = total

        stages = (stage0, stage1, stage2)
        for slot in range(n_chain + len(stages) - 1):
            for depth, stage in enumerate(stages):
                if 0 <= slot - depth < n_chain:
                    stage(slot - depth)

    @pl.when(j == 0)
    def _():
        step(first=True, final=False)

    @pl.when((j > 0) & (j < last))
    def _():
        step(first=False, final=False)

    @pl.when(j == last)
    def _():
        step(first=False, final=True)


def _post_attn(o, x2d, w_o, g1, b1, w_up_all, w_down_all, layer, g2, b2, tm, tf, n_chain):
    t = x2d.shape[0]
    row = lambda i, j: (i, 0)
    full = lambda i, j: (0, 0)
    vec = pl.BlockSpec((1, D_MODEL), full)
    assert D_FF // tf >= 2
    return pl.pallas_call(
        functools.partial(_post_attn_kernel, n_chain=n_chain),
        grid=(t // tm, D_FF // tf),
        in_specs=[
            pl.BlockSpec((tm, D_MODEL), row),
            pl.BlockSpec((tm, D_MODEL), row),
            pl.BlockSpec((D_MODEL, D_MODEL), full),
            vec, vec,
            pl.BlockSpec((None, D_MODEL, tf), lambda i, j: (layer, 0, j)),
            pl.BlockSpec((None, tf, D_MODEL), lambda i, j: (layer, j, 0)),
            vec, vec,
        ],
        out_specs=pl.BlockSpec((tm, D_MODEL), row),
        out_shape=jax.ShapeDtypeStruct((t, D_MODEL), F32),
        scratch_shapes=[pltpu.VMEM((tm, D_MODEL), BF16), pltpu.VMEM((tm, D_MODEL), F32)],
        compiler_params=_cparams("parallel", "arbitrary"),
        name="post_attn",
    )(o, x2d, w_o.astype(BF16), g1.reshape(1, -1), b1.reshape(1, -1), w_up_all, w_down_all,
      g2.reshape(1, -1), b2.reshape(1, -1))


def kernel(x, positions, ln_mix_g, ln_mix_b, mla_w_dq, mla_q_norm, mla_w_uq, mla_w_dkv, mla_kv_norm, mla_w_ukv, mla_w_o, diff_w_qkv, diff_lambda_q1, diff_lambda_k1, diff_lambda_q2, diff_lambda_k2, diff_subln, diff_w_o, ln_ffn_g, ln_ffn_b, ffn_w_up, ffn_w_down):
    batch, seq, d_model = x.shape
    assert d_model == D_MODEL and seq % ATTN_TILE == 0
    assert (batch * seq) % POST_ROWS == 0 and (batch * seq) % PROJ_ROWS == 0
    x2d = x.reshape(batch * seq, D_MODEL)
    pos = positions.reshape(-1)

    mla_c, mla_s = _rope_tables(pos, MLA_ROPE // 2, LANES, 1, MLA_ROPE_THETA)
    diff_c, diff_s = _rope_tables(pos, DIFF_ROT_DIM // 2, DIFF_HEAD_DIM, 2, DIFF_ROPE_THETA)

    post_tiles = (POST_ROWS, FF_CHUNK, POST_CHAINS)
    w_up_all = ffn_w_up.astype(BF16)
    w_down_all = ffn_w_down.astype(BF16)
    q, k, v = _mla_proj(x2d, mla_c, mla_s, mla_w_dq[0], mla_q_norm[0], mla_w_uq[0],
                        mla_w_dkv[0], mla_kv_norm[0], mla_w_ukv[0], PROJ_ROWS)
    o = _mla_flash(q, k, v, batch, seq, ATTN_TILE)
    x2d = _post_attn(o, x2d, mla_w_o[0], ln_mix_g[0], ln_mix_b[0], w_up_all, w_down_all, 0,
                     ln_ffn_g[0], ln_ffn_b[0], *post_tiles)

    lambda_init = 0.8 - 0.6 * math.exp(-0.3 * 1)
    q, k, v = _diff_proj(x2d, diff_c, diff_s, diff_w_qkv[0], PROJ_ROWS)
    o = _diff_flash(q, k, v, diff_lambda_q1[0], diff_lambda_k1[0], diff_lambda_q2[0],
                    diff_lambda_k2[0], diff_subln[0], lambda_init, batch, seq, ATTN_TILE)
    x2d = _post_attn(o, x2d, diff_w_o[0], ln_mix_g[1], ln_mix_b[1], w_up_all, w_down_all, 1,
                     ln_ffn_g[1], ln_ffn_b[1], *post_tiles)
    return x2d.reshape(batch, seq, D_MODEL)
```

```python
import functools
import math

import jax
import jax.numpy as jnp
from jax import lax
from jax.experimental import pallas as pl
from jax.experimental.pallas import tpu as pltpu

D_MODEL = 1024
DEPTH = 2

MLA_NOPE = 128
MLA_ROPE = 64
MLA_V = 128
MLA_HEADS = 8
MLA_Q_LORA = 384
MLA_KV_LORA = 256
MLA_ROPE_THETA = 10000.0
MLA_QK_PAD = 256

DIFF_HEAD_DIM = 64
DIFF_HEADS = 8
DIFF_ROT_DIM = 16
DIFF_ROPE_THETA = 500000.0

D_FF = 4 * D_MODEL
DEEPNORM_ALPHA = (2 * DEPTH) ** 0.25
LN_EPS = 1e-5
RMS_EPS = 1e-6

LANES = 128
SUM_ROWS = 16
NEG_BIG = -1e30
LOG2_E = math.log2(math.e)
VMEM_LIMIT_BYTES = 48 * 1024 * 1024

F32 = jnp.float32
BF16 = jnp.bfloat16

TABLE_ROWS = 512
PROJ_ROWS = 1024
MLA_PROJ_CHAINS = 4
ATTN_TILE = 512
MLA_SCORE_BUFFERS = 3
DIFF_SCORE_BUFFERS = 2
POST_ROWS = 1024
POST_CHAINS = 4
FF_CHUNK = 1024


def _cparams(*sem):
    return pltpu.CompilerParams(dimension_semantics=sem, vmem_limit_bytes=VMEM_LIMIT_BYTES)


def _dot(a, b):
    return jnp.dot(a, b, preferred_element_type=F32)


def _rms(x, g):
    return x * lax.rsqrt(jnp.mean(jnp.square(x), axis=-1, keepdims=True) + RMS_EPS) * g


def _layernorm(y, g, b):
    mu = jnp.mean(y, axis=-1, keepdims=True)
    d = y - mu
    var = jnp.mean(jnp.square(d), axis=-1, keepdims=True)
    return d * lax.rsqrt(var + LN_EPS) * g + b


def _rope_table_kernel(pos_ref, invf_ref, c_ref, s_ref, *, n_tok, half, dup, group):
    rows = pos_ref.shape[0]
    width = LANES // n_tok
    lane = lax.broadcasted_iota(jnp.int32, (rows, LANES), 1)
    pos = pos_ref[...]
    pos_dense = jnp.broadcast_to(pos[:, n_tok - 1:n_tok], (rows, LANES))
    for a in reversed(range(n_tok - 1)):
        pos_dense = jnp.where(lane < (a + 1) * width, pos[:, a:a + 1], pos_dense)
    ang = pos_dense.astype(F32) * invf_ref[...]
    cos_d = jnp.cos(ang)
    sin_d = jnp.sin(ang)
    neg_sin_d = -sin_d

    def spread(first_d, second_d, a, fill):
        rolled = {}

        def at(x_d, key, shift):
            if (key, shift) not in rolled:
                rolled[key, shift] = x_d if shift == 0 else pltpu.roll(x_d, shift, axis=1)
            return rolled[key, shift]

        out = jnp.full((rows, LANES), fill, F32)
        for g0 in range(0, LANES, group):
            for h, x_d in enumerate((first_d, second_d)):
                src = a * width + (h * half if dup == 2 else 0)
                dst = g0 + h * half
                val = at(x_d, h, (dst - src) % LANES)
                out = jnp.where((lane >= dst) & (lane < dst + half), val, out)
        return out

    for a in range(n_tok):
        c_ref[a] = spread(cos_d, cos_d, a, 1.0)
        s_ref[a] = spread(neg_sin_d, sin_d, a, 0.0)


def _rope_tables(positions, half, group, dup, theta, rows_per_step=TABLE_ROWS):
    t = positions.size
    n_tok = LANES // (half * dup)
    rows = t // n_tok
    assert rows % rows_per_step == 0
    inv_freq = theta ** (-jnp.arange(0, 2 * half, 2, dtype=F32) / (2 * half))
    invf = jnp.tile(inv_freq, LANES // half).reshape(1, LANES)
    pos_t = positions.reshape(n_tok, rows).T
    out_spec = pl.BlockSpec((n_tok, rows_per_step, LANES), lambda i: (0, i, 0))
    c, s = pl.pallas_call(
        functools.partial(_rope_table_kernel, n_tok=n_tok, half=half, dup=dup, group=group),
        grid=(rows // rows_per_step,),
        in_specs=[pl.BlockSpec((rows_per_step, n_tok), lambda i: (i, 0)),
                  pl.BlockSpec((1, LANES), lambda i: (0, 0))],
        out_specs=[out_spec, out_spec],
        out_shape=[jax.ShapeDtypeStruct((n_tok, rows, LANES), F32)] * 2,
        compiler_params=_cparams("parallel"),
        name="rope_table",
    )(pos_t, invf)
    return c.reshape(t, LANES), s.reshape(t, LANES)


def _rotate_half_in_lanes(x, half, group, c_tab, s_tab):
    lane = lax.broadcasted_iota(jnp.int32, x.shape, 1)
    from_above = pltpu.roll(x, LANES - half, axis=1)
    from_below = pltpu.roll(x, half, axis=1)
    partner = jnp.where(lane % group < half, from_above, from_below)
    return x * c_tab + partner * s_tab


def _emit_staggered(stages, n_chain):
    for slot in range(n_chain + len(stages) - 1):
        for depth, stage in enumerate(stages):
            if 0 <= slot - depth < n_chain:
                stage(slot - depth)


def _mla_proj_kernel(x_ref, wdown_ref, qn_ref, wuq_ref, kvn_ref, wukv_ref,
                     c_ref, s_ref, q_ref, k_ref, v_ref, *, scale, n_chain):
    rc = x_ref.shape[0] // n_chain
    cq, c_kv, k_pe, q, kv = {}, {}, {}, {}, {}

    def rows(c):
        return pl.ds(c * rc, rc)

    def rope_of(c):
        return functools.partial(_rotate_half_in_lanes, half=MLA_ROPE // 2, group=LANES,
                                 c_tab=c_ref[rows(c), :], s_tab=s_ref[rows(c), :])

    def down_proj(c):
        down = _dot(x_ref[rows(c), :].astype(BF16), wdown_ref[...])
        cq[c] = _rms(down[:, :MLA_Q_LORA], qn_ref[...]).astype(BF16)
        ckv = down[:, MLA_Q_LORA:]
        c_kv[c] = _rms(ckv[:, :MLA_KV_LORA], kvn_ref[...]).astype(BF16)
        k_pe[c] = rope_of(c)(ckv[:, MLA_KV_LORA:]).astype(BF16)

    def up_proj(c):
        q[c] = _dot(cq[c], wuq_ref[...])
        kv[c] = _dot(c_kv[c], wukv_ref[...])

    def write_heads(c):
        rope = rope_of(c)
        for h in range(MLA_HEADS):
            lo, mid, hi = h * MLA_QK_PAD, h * MLA_QK_PAD + LANES, (h + 1) * MLA_QK_PAD
            q_ref[rows(c), lo:mid] = (q[c][:, lo:mid] * scale).astype(BF16)
            q_ref[rows(c), mid:hi] = (rope(q[c][:, mid:hi]) * scale).astype(BF16)
            k_ref[rows(c), lo:mid] = kv[c][:, lo:mid].astype(BF16)
            k_ref[rows(c), mid:hi] = k_pe[c]
            v_ref[rows(c), h * MLA_V:(h + 1) * MLA_V] = kv[c][:, mid:hi].astype(BF16)

    _emit_staggered((down_proj, up_proj, write_heads), n_chain)


def _mla_proj(x2d, c_tab, s_tab, w_dq, q_norm, w_uq, w_dkv, kv_norm, w_ukv, tm):
    t = x2d.shape[0]
    h = MLA_HEADS
    pad = MLA_QK_PAD - MLA_NOPE - MLA_ROPE
    wuq = jnp.pad(w_uq.reshape(MLA_Q_LORA, h, MLA_NOPE + MLA_ROPE), ((0, 0), (0, 0), (0, pad)))
    wuq = wuq.reshape(MLA_Q_LORA, h * MLA_QK_PAD).astype(BF16)
    wdown = jnp.concatenate([w_dq, jnp.pad(w_dkv, ((0, 0), (0, pad)))], axis=1).astype(BF16)
    row = lambda i: (i, 0)
    full = lambda i: (0, 0)
    return pl.pallas_call(
        functools.partial(_mla_proj_kernel, scale=LOG2_E * (MLA_NOPE + MLA_ROPE) ** -0.5,
                          n_chain=MLA_PROJ_CHAINS),
        grid=(t // tm,),
        in_specs=[
            pl.BlockSpec((tm, D_MODEL), row),
            pl.BlockSpec((D_MODEL, MLA_Q_LORA + MLA_KV_LORA + LANES), full),
            pl.BlockSpec((1, MLA_Q_LORA), full),
            pl.BlockSpec((MLA_Q_LORA, h * MLA_QK_PAD), full),
            pl.BlockSpec((1, MLA_KV_LORA), full),
            pl.BlockSpec((MLA_KV_LORA, h * (MLA_NOPE + MLA_V)), full),
            pl.BlockSpec((tm, LANES), row),
            pl.BlockSpec((tm, LANES), row),
        ],
        out_specs=[
            pl.BlockSpec((tm, h * MLA_QK_PAD), row),
            pl.BlockSpec((tm, h * MLA_QK_PAD), row),
            pl.BlockSpec((tm, h * MLA_V), row),
        ],
        out_shape=[
            jax.ShapeDtypeStruct((t, h * MLA_QK_PAD), BF16),
            jax.ShapeDtypeStruct((t, h * MLA_QK_PAD), BF16),
            jax.ShapeDtypeStruct((t, h * MLA_V), BF16),
        ],
        compiler_params=_cparams("parallel"),
        name="mla_proj",
    )(x2d, wdown, q_norm.reshape(1, -1), wuq, kv_norm.reshape(1, -1), w_ukv.astype(BF16),
      c_tab, s_tab)


def _diff_proj_kernel(x_ref, w_ref, c_ref, s_ref, q_ref, k_ref, v_ref, *, scale):
    rope = functools.partial(_rotate_half_in_lanes, half=DIFF_ROT_DIM // 2, group=DIFF_HEAD_DIM,
                             c_tab=c_ref[...], s_tab=s_ref[...])
    qkv = _dot(x_ref[...].astype(BF16), w_ref[...])
    for j in range(D_MODEL // LANES):
        lo, hi = j * LANES, (j + 1) * LANES
        q_ref[:, lo:hi] = (rope(qkv[:, lo:hi]) * scale).astype(BF16)
        k_ref[:, lo:hi] = rope(qkv[:, D_MODEL + lo:D_MODEL + hi]).astype(BF16)
    v_ref[...] = qkv[:, 2 * D_MODEL:].astype(BF16)


def _diff_proj(x2d, c_tab, s_tab, w_qkv, tm):
    t = x2d.shape[0]
    row = lambda i: (i, 0)
    full = lambda i: (0, 0)
    return pl.pallas_call(
        functools.partial(_diff_proj_kernel, scale=LOG2_E * DIFF_HEAD_DIM ** -0.5),
        grid=(t // tm,),
        in_specs=[
            pl.BlockSpec((tm, D_MODEL), row),
            pl.BlockSpec((D_MODEL, 3 * D_MODEL), full),
            pl.BlockSpec((tm, LANES), row),
            pl.BlockSpec((tm, LANES), row),
        ],
        out_specs=[pl.BlockSpec((tm, D_MODEL), row)] * 3,
        out_shape=[jax.ShapeDtypeStruct((t, D_MODEL), BF16)] * 3,
        compiler_params=_cparams("parallel"),
        name="diff_proj",
    )(x2d, w_qkv.astype(BF16), c_tab, s_tab)


def _transpose_v(v_ref, vt_ref, tk):
    dv = v_ref.shape[1]
    for j in range(vt_ref.shape[0]):
        vt_ref[j, :dv, :] = v_ref[j * tk:(j + 1) * tk, :].astype(F32).T.astype(vt_ref.dtype)
        vt_ref[j, dv:, :] = jnp.ones((SUM_ROWS, tk), vt_ref.dtype)


def _flash_head(q_block, k_ref, vt_ref, s_ref, tri_ref, m_ref, acc_ref, finish, n_q, tq):
    rows = tq
    n_buf = s_ref.shape[0]
    dv = acc_ref.shape[1] - SUM_ROWS
    half = tq // 2
    tri_ref[...] = jnp.where(lax.broadcasted_iota(jnp.int32, (half, half), 0)
                             <= lax.broadcasted_iota(jnp.int32, (half, half), 1), 0.0, NEG_BIG)
    steps = [(i, j) for i in range(n_q) for j in range(i + 1)]

    def scores(t):
        i, j = steps[t]
        k = k_ref[j * tq:(j + 1) * tq, :]
        s_ref[t % n_buf] = _dot(k, q_block(i, j == 0))

    def update(par, cols, s, vt, first):
        m_blk = jnp.max(s, axis=0, keepdims=True)
        if first:
            m_new = m_blk
            acc_ref[par, :, cols] = _dot(vt, jnp.exp2(s - m_new).astype(BF16))
        else:
            m_prev = m_ref[par, :, cols]
            m_new = jnp.maximum(m_prev, m_blk)
            alpha = jnp.exp2(m_prev - m_new)
            acc_ref[par, :, cols] = (alpha * acc_ref[par, :, cols]
                                     + _dot(vt, jnp.exp2(s - m_new).astype(BF16)))
        m_ref[par, :, cols] = m_new

    def consume(t):
        i, j = steps[t]
        s_t = s_ref.at[t % n_buf]
        par = i % 2
        if j < i:
            update(par, slice(0, rows), s_t[...], vt_ref[j], first=j == 0)
            return
        for g in range(rows // half):
            cols = slice(g * half, (g + 1) * half)
            if g % 2 == 0:
                s = s_t[:half, cols] + tri_ref[...]
                vt = vt_ref[j, :, :half]
            else:
                s = jnp.concatenate([s_t[:half, cols], s_t[half:, cols] + tri_ref[...]], axis=0)
                vt = vt_ref[j]
            update(par, cols, s, vt, first=i == 0)
        finish(i, acc_ref[par, :dv, :], acc_ref[par, dv:dv + 1, :])

    return len(steps), n_buf - 1, scores, consume


def _run_interleaved(programs):
    n_steps, ahead = programs[0][:2]
    for _, _, scores, _ in programs:
        for t in range(min(ahead, n_steps)):
            scores(t)
    for t in range(n_steps):
        for _, _, scores, consume in programs:
            if t + ahead < n_steps:
                scores(t + ahead)
            consume(t)


def _mla_flash_kernel(q_ref, k_ref, v_ref, o_ref, qt_ref, vt_ref, s_ref, tri_ref, m_ref, acc_ref,
                      *, n_q, tq):
    _transpose_v(v_ref, vt_ref, tq)

    def q_block(i, first):
        if first:
            qt_ref[i % 2] = q_ref[i * tq:(i + 1) * tq, :].astype(F32).T.astype(BF16)
        return qt_ref[i % 2]

    def finish(i, acc, l):
        o_ref[i * tq:(i + 1) * tq, :] = (acc / l).T.astype(o_ref.dtype)

    _run_interleaved([_flash_head(q_block, k_ref, vt_ref, s_ref, tri_ref, m_ref, acc_ref,
                                  finish, n_q, tq)])


def _diff_flash_kernel(q_ref, k_ref, v_ref, lq1_ref, lk1_ref, lq2_ref, lk2_ref, subln_ref,
                       o_ref, vt_ref, *scratch, n_q, tq, lambda_init):
    _transpose_v(v_ref, vt_ref, tq)
    lam = (jnp.exp(jnp.sum(lq1_ref[...] * lk1_ref[...]))
           - jnp.exp(jnp.sum(lq2_ref[...] * lk2_ref[...])) + lambda_init)
    subln = subln_ref[...]
    per_map = len(scratch) // 2
    maps = [scratch[c * per_map:(c + 1) * per_map] for c in range(2)]
    acc1_ref = maps[0][-1]
    dv = v_ref.shape[1]

    programs = []
    for c, (qt_ref, s_ref, tri_ref, m_ref, acc_ref) in enumerate(maps):
        def q_block(i, first, c=c, qt_ref=qt_ref):
            if first:
                q = q_ref[i * tq:(i + 1) * tq, :]
                lane = lax.broadcasted_iota(jnp.int32, q.shape, 1)
                keep = (lane < DIFF_HEAD_DIM) if c == 0 else (lane >= DIFF_HEAD_DIM)
                qt_ref[i % 2] = jnp.where(keep, q, jnp.zeros_like(q)).astype(F32).T.astype(BF16)
            return qt_ref[i % 2]

        def finish(i, acc, l, c=c):
            if c == 0:
                return
            par = i % 2
            o1_t = acc1_ref[par, :dv, :] / acc1_ref[par, dv:dv + 1, :]
            o_t = o1_t - lam * (acc / l)
            o = _rms(o_t.T, subln) * (1.0 - lambda_init)
            o_ref[i * tq:(i + 1) * tq, :] = o.astype(o_ref.dtype)

        programs.append(_flash_head(q_block, k_ref, vt_ref, s_ref, tri_ref, m_ref, acc_ref,
                                    finish, n_q, tq))
    _run_interleaved(programs)


def _flash_scratch(seq, tq, dv, n_buf):
    return [pltpu.VMEM((seq // tq, dv + SUM_ROWS, tq), BF16),
            pltpu.VMEM((n_buf, tq, tq), F32),
            pltpu.VMEM((tq // 2, tq // 2), F32),
            pltpu.VMEM((2, 1, tq), F32),
            pltpu.VMEM((2, dv + SUM_ROWS, tq), F32)]


def _flash_specs(seq, dk, dv):
    head = lambda width: pl.BlockSpec((seq, width), lambda b, h: (b, h))
    return head(dk), head(dk), head(dv), head(dv)


def _mla_flash(q, k, v, batch, seq, tq):
    q_spec, k_spec, v_spec, o_spec = _flash_specs(seq, MLA_QK_PAD, MLA_V)
    return pl.pallas_call(
        functools.partial(_mla_flash_kernel, n_q=seq // tq, tq=tq),
        grid=(batch, MLA_HEADS),
        in_specs=[q_spec, k_spec, v_spec],
        out_specs=o_spec,
        out_shape=jax.ShapeDtypeStruct((batch * seq, MLA_HEADS * MLA_V), BF16),
        scratch_shapes=[pltpu.VMEM((2, MLA_QK_PAD, tq), BF16)]
        + _flash_scratch(seq, tq, MLA_V, MLA_SCORE_BUFFERS),
        compiler_params=_cparams("parallel", "parallel"),
        name="mla_flash",
    )(q, k, v)


def _diff_flash(q, k, v, lq1, lk1, lq2, lk2, subln, lambda_init, batch, seq, tq):
    dv = 2 * DIFF_HEAD_DIM
    q_spec, k_spec, v_spec, o_spec = _flash_specs(seq, dv, dv)
    vec = lambda n: pl.BlockSpec((1, n), lambda b, h: (0, 0))
    return pl.pallas_call(
        functools.partial(_diff_flash_kernel, n_q=seq // tq, tq=tq, lambda_init=lambda_init),
        grid=(batch, DIFF_HEADS),
        in_specs=[q_spec, k_spec, v_spec, vec(DIFF_HEAD_DIM), vec(DIFF_HEAD_DIM),
                  vec(DIFF_HEAD_DIM), vec(DIFF_HEAD_DIM), vec(dv)],
        out_specs=o_spec,
        out_shape=jax.ShapeDtypeStruct((batch * seq, DIFF_HEADS * dv), BF16),
        scratch_shapes=_flash_scratch(seq, tq, dv, DIFF_SCORE_BUFFERS)[:1]
        + 2 * ([pltpu.VMEM((2, dv, tq), BF16)]
               + _flash_scratch(seq, tq, dv, DIFF_SCORE_BUFFERS)[1:]),
        compiler_params=_cparams("parallel", "parallel"),
        name="diff_flash",
    )(q, k, v, lq1.reshape(1, -1), lk1.reshape(1, -1), lq2.reshape(1, -1), lk2.reshape(1, -1),
      subln.reshape(1, -1))


def _post_attn_kernel(o_ref, x_ref, wo_ref, g1_ref, b1_ref, wu_ref, wd_ref, g2_ref, b2_ref,
                      y_ref, x1b_ref, acc_ref, *, n_chain):
    j = pl.program_id(1)
    last = pl.num_programs(1) - 1
    rc = acc_ref.shape[0] // n_chain

    def step(first, final):
        xb, base, hidden = {}, {}, {}

        def rows(c):
            return pl.ds(c * rc, rc)

        def stage0(c):
            if first:
                x1 = _layernorm(DEEPNORM_ALPHA * x_ref[rows(c), :] + _dot(o_ref[rows(c), :], wo_ref[...]),
                                g1_ref[...], b1_ref[...])
                xb[c] = x1.astype(BF16)
                x1b_ref[rows(c), :] = xb[c]
                base[c] = DEEPNORM_ALPHA * x1
            else:
                xb[c] = x1b_ref[rows(c), :]

        def stage1(c):
            hidden[c] = jnp.square(jnp.maximum(_dot(xb[c], wu_ref[...]), 0.0)).astype(BF16)

        def stage2(c):
            total = (base[c] if first else acc_ref[rows(c), :]) + _dot(hidden[c], wd_ref[...])
            if final:
                y_ref[rows(c), :] = _layernorm(total, g2_ref[...], b2_ref[...])
            else:
                acc_ref[rows(c), :] = total

        _emit_staggered((stage0, stage1, stage2), n_chain)

    @pl.when(j == 0)
    def _():
        step(first=True, final=False)

    @pl.when((j > 0) & (j < last))
    def _():
        step(first=False, final=False)

    @pl.when(j == last)
    def _():
        step(first=False, final=True)


def _post_attn(o, x2d, w_o, g1, b1, w_up_all, w_down_all, layer, g2, b2, tm, tf, n_chain):
    t = x2d.shape[0]
    row = lambda i, j: (i, 0)
    full = lambda i, j: (0, 0)
    vec = pl.BlockSpec((1, D_MODEL), full)
    assert D_FF // tf >= 2
    return pl.pallas_call(
        functools.partial(_post_attn_kernel, n_chain=n_chain),
        grid=(t // tm, D_FF // tf),
        in_specs=[
            pl.BlockSpec((tm, D_MODEL), row),
            pl.BlockSpec((tm, D_MODEL), row),
            pl.BlockSpec((D_MODEL, D_MODEL), full),
            vec, vec,
            pl.BlockSpec((None, D_MODEL, tf), lambda i, j: (layer, 0, j)),
            pl.BlockSpec((None, tf, D_MODEL), lambda i, j: (layer, j, 0)),
            vec, vec,
        ],
        out_specs=pl.BlockSpec((tm, D_MODEL), row),
        out_shape=jax.ShapeDtypeStruct((t, D_MODEL), F32),
        scratch_shapes=[pltpu.VMEM((tm, D_MODEL), BF16), pltpu.VMEM((tm, D_MODEL), F32)],
        compiler_params=_cparams("parallel", "arbitrary"),
        name="post_attn",
    )(o, x2d, w_o.astype(BF16), g1.reshape(1, -1), b1.reshape(1, -1), w_up_all, w_down_all,
      g2.reshape(1, -1), b2.reshape(1, -1))


def kernel(x, positions, ln_mix_g, ln_mix_b, mla_w_dq, mla_q_norm, mla_w_uq, mla_w_dkv, mla_kv_norm, mla_w_ukv, mla_w_o, diff_w_qkv, diff_lambda_q1, diff_lambda_k1, diff_lambda_q2, diff_lambda_k2, diff_subln, diff_w_o, ln_ffn_g, ln_ffn_b, ffn_w_up, ffn_w_down):
    batch, seq, d_model = x.shape
    assert d_model == D_MODEL and seq % ATTN_TILE == 0
    assert (batch * seq) % POST_ROWS == 0 and (batch * seq) % PROJ_ROWS == 0
    x2d = x.reshape(batch * seq, D_MODEL)
    pos = positions.reshape(-1)

    mla_c, mla_s = _rope_tables(pos, MLA_ROPE // 2, LANES, 1, MLA_ROPE_THETA)
    diff_c, diff_s = _rope_tables(pos, DIFF_ROT_DIM // 2, DIFF_HEAD_DIM, 2, DIFF_ROPE_THETA)

    post_tiles = (POST_ROWS, FF_CHUNK, POST_CHAINS)
    w_up_all = ffn_w_up.astype(BF16)
    w_down_all = ffn_w_down.astype(BF16)
    q, k, v = _mla_proj(x2d, mla_c, mla_s, mla_w_dq[0], mla_q_norm[0], mla_w_uq[0],
                        mla_w_dkv[0], mla_kv_norm[0], mla_w_ukv[0], PROJ_ROWS)
    o = _mla_flash(q, k, v, batch, seq, ATTN_TILE)
    x2d = _post_attn(o, x2d, mla_w_o[0], ln_mix_g[0], ln_mix_b[0], w_up_all, w_down_all, 0,
                     ln_ffn_g[0], ln_ffn_b[0], *post_tiles)

    lambda_init = 0.8 - 0.6 * math.exp(-0.3 * 1)
    q, k, v = _diff_proj(x2d, diff_c, diff_s, diff_w_qkv[0], PROJ_ROWS)
    o = _diff_flash(q, k, v, diff_lambda_q1[0], diff_lambda_k1[0], diff_lambda_q2[0],
                    diff_lambda_k2[0], diff_subln[0], lambda_init, batch, seq, ATTN_TILE)
    x2d = _post_attn(o, x2d, diff_w_o[0], ln_mix_g[1], ln_mix_b[1], w_up_all, w_down_all, 1,
                     ln_ffn_g[1], ln_ffn_b[1], *post_tiles)
    return x2d.reshape(batch, seq, D_MODEL)
```

```python
import functools
import math

import jax
import jax.numpy as jnp
from jax import lax
from jax.experimental import pallas as pl
from jax.experimental.pallas import tpu as pltpu

D_MODEL = 1024
DEPTH = 2

MLA_NOPE = 128
MLA_ROPE = 64
MLA_V = 128
MLA_HEADS = 8
MLA_Q_LORA = 384
MLA_KV_LORA = 256
MLA_ROPE_THETA = 10000.0
MLA_QK_PAD = 256

DIFF_HEAD_DIM = 64
DIFF_HEADS = 8
DIFF_ROT_DIM = 16
DIFF_ROPE_THETA = 500000.0

D_FF = 4 * D_MODEL
DEEPNORM_ALPHA = (2 * DEPTH) ** 0.25
LN_EPS = 1e-5
RMS_EPS = 1e-6

LANES = 128
SUM_ROWS = 16
ROW_PAD = 128
NEG_BIG = -1e30
LOG2_E = math.log2(math.e)
VMEM_LIMIT_BYTES = 48 * 1024 * 1024

F32 = jnp.float32
BF16 = jnp.bfloat16

TABLE_ROWS = 512
PROJ_ROWS = 1024
MLA_PROJ_CHAINS = 4
ATTN_TILE = 512
MLA_SCORE_BUFFERS = 3
DIFF_SCORE_BUFFERS = 2
POST_ROWS = 1024
POST_CHAINS = 4
FF_CHUNK = 1024


def _cparams(*sem):
    return pltpu.CompilerParams(dimension_semantics=sem, vmem_limit_bytes=VMEM_LIMIT_BYTES)


def _dot(a, b):
    return jnp.dot(a, b, preferred_element_type=F32)


def _rms(x, g):
    return x * lax.rsqrt(jnp.mean(jnp.square(x), axis=-1, keepdims=True) + RMS_EPS) * g


def _layernorm(y, g, b):
    mu = jnp.mean(y, axis=-1, keepdims=True)
    d = y - mu
    var = jnp.mean(jnp.square(d), axis=-1, keepdims=True)
    return d * lax.rsqrt(var + LN_EPS) * g + b


def _rope_table_kernel(pos_ref, invf_ref, c_ref, s_ref, *, n_tok, half, dup, group):
    rows = pos_ref.shape[0]
    width = LANES // n_tok
    lane = lax.broadcasted_iota(jnp.int32, (rows, LANES), 1)
    pos = pos_ref[...]
    pos_dense = jnp.broadcast_to(pos[:, n_tok - 1:n_tok], (rows, LANES))
    for a in reversed(range(n_tok - 1)):
        pos_dense = jnp.where(lane < (a + 1) * width, pos[:, a:a + 1], pos_dense)
    ang = pos_dense.astype(F32) * invf_ref[...]
    cos_d = jnp.cos(ang)
    sin_d = jnp.sin(ang)
    neg_sin_d = -sin_d

    def spread(first_d, second_d, a, fill):
        rolled = {}

        def at(x_d, key, shift):
            if (key, shift) not in rolled:
                rolled[key, shift] = x_d if shift == 0 else pltpu.roll(x_d, shift, axis=1)
            return rolled[key, shift]

        out = jnp.full((rows, LANES), fill, F32)
        for g0 in range(0, LANES, group):
            for h, x_d in enumerate((first_d, second_d)):
                src = a * width + (h * half if dup == 2 else 0)
                dst = g0 + h * half
                val = at(x_d, h, (dst - src) % LANES)
                out = jnp.where((lane >= dst) & (lane < dst + half), val, out)
        return out

    for a in range(n_tok):
        c_ref[a] = spread(cos_d, cos_d, a, 1.0)
        s_ref[a] = spread(neg_sin_d, sin_d, a, 0.0)


def _rope_tables(positions, half, group, dup, theta, rows_per_step=TABLE_ROWS):
    t = positions.size
    n_tok = LANES // (half * dup)
    rows = t // n_tok
    assert rows % rows_per_step == 0
    inv_freq = theta ** (-jnp.arange(0, 2 * half, 2, dtype=F32) / (2 * half))
    invf = jnp.tile(inv_freq, LANES // half).reshape(1, LANES)
    pos_t = positions.reshape(n_tok, rows).T
    out_spec = pl.BlockSpec((n_tok, rows_per_step, LANES), lambda i: (0, i, 0))
    c, s = pl.pallas_call(
        functools.partial(_rope_table_kernel, n_tok=n_tok, half=half, dup=dup, group=group),
        grid=(rows // rows_per_step,),
        in_specs=[pl.BlockSpec((rows_per_step, n_tok), lambda i: (i, 0)),
                  pl.BlockSpec((1, LANES), lambda i: (0, 0))],
        out_specs=[out_spec, out_spec],
        out_shape=[jax.ShapeDtypeStruct((n_tok, rows, LANES), F32)] * 2,
        compiler_params=_cparams("parallel"),
        name="rope_table",
    )(pos_t, invf)
    return c.reshape(t, LANES), s.reshape(t, LANES)


def _rotate_half_in_lanes(x, half, group, c_tab, s_tab):
    lane = lax.broadcasted_iota(jnp.int32, x.shape, 1)
    from_above = pltpu.roll(x, LANES - half, axis=1)
    from_below = pltpu.roll(x, half, axis=1)
    partner = jnp.where(lane % group < half, from_above, from_below)
    return x * c_tab + partner * s_tab


def _emit_staggered(stages, n_chain):
    for slot in range(n_chain + len(stages) - 1):
        for depth, stage in enumerate(stages):
            if 0 <= slot - depth < n_chain:
                stage(slot - depth)


def _mla_proj_kernel(x_ref, wdown_ref, qn_ref, wuq_ref, kvn_ref, wukv_ref,
                     c_ref, s_ref, q_ref, k_ref, v_ref, *, scale, n_chain):
    rc = x_ref.shape[0] // n_chain
    cq, c_kv, k_pe, q, kv = {}, {}, {}, {}, {}

    def rows(c):
        return pl.ds(c * rc, rc)

    def rope_of(c):
        return functools.partial(_rotate_half_in_lanes, half=MLA_ROPE // 2, group=LANES,
                                 c_tab=c_ref[rows(c), :], s_tab=s_ref[rows(c), :])

    def down_proj(c):
        down = _dot(x_ref[rows(c), :].astype(BF16), wdown_ref[...])
        cq[c] = _rms(down[:, :MLA_Q_LORA], qn_ref[...]).astype(BF16)
        ckv = down[:, MLA_Q_LORA:]
        c_kv[c] = _rms(ckv[:, :MLA_KV_LORA], kvn_ref[...]).astype(BF16)
        k_pe[c] = rope_of(c)(ckv[:, MLA_KV_LORA:]).astype(BF16)

    def up_proj(c):
        q[c] = _dot(cq[c], wuq_ref[...])
        kv[c] = _dot(c_kv[c], wukv_ref[...])

    def write_heads(c):
        rope = rope_of(c)
        for h in range(MLA_HEADS):
            lo, mid, hi = h * MLA_QK_PAD, h * MLA_QK_PAD + LANES, (h + 1) * MLA_QK_PAD
            q_ref[rows(c), lo:mid] = (q[c][:, lo:mid] * scale).astype(BF16)
            q_ref[rows(c), mid:hi] = (rope(q[c][:, mid:hi]) * scale).astype(BF16)
            k_ref[rows(c), lo:mid] = kv[c][:, lo:mid].astype(BF16)
            k_ref[rows(c), mid:hi] = k_pe[c]
            v_ref[rows(c), h * MLA_V:(h + 1) * MLA_V] = kv[c][:, mid:hi].astype(BF16)

    _emit_staggered((down_proj, up_proj, write_heads), n_chain)


def _mla_proj(x2d, c_tab, s_tab, w_dq, q_norm, w_uq, w_dkv, kv_norm, w_ukv, tm):
    t = x2d.shape[0]
    h = MLA_HEADS
    pad = MLA_QK_PAD - MLA_NOPE - MLA_ROPE
    wuq = jnp.pad(w_uq.reshape(MLA_Q_LORA, h, MLA_NOPE + MLA_ROPE), ((0, 0), (0, 0), (0, pad)))
    wuq = wuq.reshape(MLA_Q_LORA, h * MLA_QK_PAD).astype(BF16)
    wdown = jnp.concatenate([w_dq, jnp.pad(w_dkv, ((0, 0), (0, pad)))], axis=1).astype(BF16)
    row = lambda i: (i, 0)
    full = lambda i: (0, 0)
    return pl.pallas_call(
        functools.partial(_mla_proj_kernel, scale=LOG2_E * (MLA_NOPE + MLA_ROPE) ** -0.5,
                          n_chain=MLA_PROJ_CHAINS),
        grid=(t // tm,),
        in_specs=[
            pl.BlockSpec((tm, D_MODEL), row),
            pl.BlockSpec((D_MODEL, MLA_Q_LORA + MLA_KV_LORA + LANES), full),
            pl.BlockSpec((1, MLA_Q_LORA), full),
            pl.BlockSpec((MLA_Q_LORA, h * MLA_QK_PAD), full),
            pl.BlockSpec((1, MLA_KV_LORA), full),
            pl.BlockSpec((MLA_KV_LORA, h * (MLA_NOPE + MLA_V)), full),
            pl.BlockSpec((tm, LANES), row),
            pl.BlockSpec((tm, LANES), row),
        ],
        out_specs=[
            pl.BlockSpec((tm, h * MLA_QK_PAD), row),
            pl.BlockSpec((tm, h * MLA_QK_PAD), row),
            pl.BlockSpec((tm, h * MLA_V), row),
        ],
        out_shape=[
            jax.ShapeDtypeStruct((t, h * MLA_QK_PAD), BF16),
            jax.ShapeDtypeStruct((t, h * MLA_QK_PAD), BF16),
            jax.ShapeDtypeStruct((t, h * MLA_V), BF16),
        ],
        compiler_params=_cparams("parallel"),
        name="mla_proj",
    )(x2d, wdown, q_norm.reshape(1, -1), wuq, kv_norm.reshape(1, -1), w_ukv.astype(BF16),
      c_tab, s_tab)


def _diff_proj_kernel(x_ref, w_ref, c_ref, s_ref, q_ref, k_ref, v_ref, *, scale):
    rope = functools.partial(_rotate_half_in_lanes, half=DIFF_ROT_DIM // 2, group=DIFF_HEAD_DIM,
                             c_tab=c_ref[...], s_tab=s_ref[...])
    qkv = _dot(x_ref[...].astype(BF16), w_ref[...])
    for j in range(D_MODEL // LANES):
        lo, hi = j * LANES, (j + 1) * LANES
        q_ref[:, lo:hi] = (rope(qkv[:, lo:hi]) * scale).astype(BF16)
        k_ref[:, lo:hi] = rope(qkv[:, D_MODEL + lo:D_MODEL + hi]).astype(BF16)
    v_ref[...] = qkv[:, 2 * D_MODEL:].astype(BF16)


def _diff_proj(x2d, c_tab, s_tab, w_qkv, tm):
    t = x2d.shape[0]
    row = lambda i: (i, 0)
    full = lambda i: (0, 0)
    return pl.pallas_call(
        functools.partial(_diff_proj_kernel, scale=LOG2_E * DIFF_HEAD_DIM ** -0.5),
        grid=(t // tm,),
        in_specs=[
            pl.BlockSpec((tm, D_MODEL), row),
            pl.BlockSpec((D_MODEL, 3 * D_MODEL), full),
            pl.BlockSpec((tm, LANES), row),
            pl.BlockSpec((tm, LANES), row),
        ],
        out_specs=[pl.BlockSpec((tm, D_MODEL), row)] * 3,
        out_shape=[jax.ShapeDtypeStruct((t, D_MODEL), BF16)] * 3,
        compiler_params=_cparams("parallel"),
        name="diff_proj",
    )(x2d, w_qkv.astype(BF16), c_tab, s_tab)


def _transpose_v(v_ref, vt_ref, tk):
    dv = v_ref.shape[1]
    for j in range(vt_ref.shape[0]):
        vt_ref[j, :dv, :tk] = v_ref[j * tk:(j + 1) * tk, :].astype(F32).T.astype(vt_ref.dtype)
        vt_ref[j, dv:, :tk] = jnp.ones((SUM_ROWS, tk), vt_ref.dtype)


def _flash_head(q_block, k_ref, vt_ref, s_ref, tri_ref, m_ref, acc_ref, finish, n_q, tq):
    rows = tq
    n_buf = s_ref.shape[0]
    dv = acc_ref.shape[1] - SUM_ROWS
    half = tq // 2
    tri_ref[...] = jnp.where(lax.broadcasted_iota(jnp.int32, (half, half), 0)
                             <= lax.broadcasted_iota(jnp.int32, (half, half), 1), 0.0, NEG_BIG)
    steps = [(i, j) for i in range(n_q) for j in range(i + 1)]

    def scores(t):
        i, j = steps[t]
        k = k_ref[j * tq:(j + 1) * tq, :]
        s_ref[t % n_buf, :, :tq] = _dot(k, q_block(i, j == 0))

    def update(par, cols, s, vt, first):
        m_blk = jnp.max(s, axis=0, keepdims=True)
        if first:
            m_new = m_blk
            acc_ref[par, :, cols] = _dot(vt, jnp.exp2(s - m_new).astype(BF16))
        else:
            m_prev = m_ref[par, :, cols]
            m_new = jnp.maximum(m_prev, m_blk)
            alpha = jnp.exp2(m_prev - m_new)
            acc_ref[par, :, cols] = (alpha * acc_ref[par, :, cols]
                                     + _dot(vt, jnp.exp2(s - m_new).astype(BF16)))
        m_ref[par, :, cols] = m_new

    def consume(t):
        i, j = steps[t]
        s_t = s_ref.at[t % n_buf, :, :tq]
        par = i % 2
        if j < i:
            update(par, slice(0, rows), s_t[...], vt_ref[j, :, :tq], first=j == 0)
            return
        for g in range(rows // half):
            cols = slice(g * half, (g + 1) * half)
            if g % 2 == 0:
                s = s_t[:half, cols] + tri_ref[...]
                vt = vt_ref[j, :, :half]
            else:
                s = jnp.concatenate([s_t[:half, cols], s_t[half:, cols] + tri_ref[...]], axis=0)
                vt = vt_ref[j, :, :tq]
            update(par, cols, s, vt, first=i == 0)
        finish(i, acc_ref[par, :dv, :tq], acc_ref[par, dv:dv + 1, :tq])

    return len(steps), n_buf - 1, scores, consume


def _run_interleaved(programs):
    n_steps, ahead = programs[0][:2]
    for _, _, scores, _ in programs:
        for t in range(min(ahead, n_steps)):
            scores(t)
    for t in range(n_steps):
        for _, _, scores, consume in programs:
            if t + ahead < n_steps:
                scores(t + ahead)
            consume(t)


def _mla_flash_kernel(q_ref, k_ref, v_ref, o_ref, qt_ref, vt_ref, s_ref, tri_ref, m_ref, acc_ref,
                      *, n_q, tq):
    _transpose_v(v_ref, vt_ref, tq)

    def q_block(i, first):
        if first:
            qt_ref[i % 2, :, :tq] = q_ref[i * tq:(i + 1) * tq, :].astype(F32).T.astype(BF16)
        return qt_ref[i % 2, :, :tq]

    def finish(i, acc, l):
        o_ref[i * tq:(i + 1) * tq, :] = (acc / l).T.astype(o_ref.dtype)

    _run_interleaved([_flash_head(q_block, k_ref, vt_ref, s_ref, tri_ref, m_ref, acc_ref,
                                  finish, n_q, tq)])


def _diff_flash_kernel(q_ref, k_ref, v_ref, lq1_ref, lk1_ref, lq2_ref, lk2_ref, subln_ref,
                       o_ref, vt_ref, *scratch, n_q, tq, lambda_init):
    _transpose_v(v_ref, vt_ref, tq)
    lam = (jnp.exp(jnp.sum(lq1_ref[...] * lk1_ref[...]))
           - jnp.exp(jnp.sum(lq2_ref[...] * lk2_ref[...])) + lambda_init)
    subln = subln_ref[...]
    per_map = len(scratch) // 2
    maps = [scratch[c * per_map:(c + 1) * per_map] for c in range(2)]
    acc1_ref = maps[0][-1]
    dv = v_ref.shape[1]

    programs = []
    for c, (qt_ref, s_ref, tri_ref, m_ref, acc_ref) in enumerate(maps):
        def q_block(i, first, c=c, qt_ref=qt_ref):
            if first:
                q = q_ref[i * tq:(i + 1) * tq, :]
                lane = lax.broadcasted_iota(jnp.int32, q.shape, 1)
                keep = (lane < DIFF_HEAD_DIM) if c == 0 else (lane >= DIFF_HEAD_DIM)
                qt_ref[i % 2, :, :tq] = jnp.where(keep, q, jnp.zeros_like(q)).astype(F32).T.astype(BF16)
            return qt_ref[i % 2, :, :tq]

        def finish(i, acc, l, c=c):
            if c == 0:
                return
            par = i % 2
            o1_t = acc1_ref[par, :dv, :tq] / acc1_ref[par, dv:dv + 1, :tq]
            o_t = o1_t - lam * (acc / l)
            o = _rms(o_t.T, subln) * (1.0 - lambda_init)
            o_ref[i * tq:(i + 1) * tq, :] = o.astype(o_ref.dtype)

        programs.append(_flash_head(q_block, k_ref, vt_ref, s_ref, tri_ref, m_ref, acc_ref,
                                    finish, n_q, tq))
    _run_interleaved(programs)


def _flash_scratch(seq, tq, dv, n_buf):
    return [pltpu.VMEM((seq // tq, dv + SUM_ROWS, tq), BF16),
            pltpu.VMEM((n_buf, tq, tq + ROW_PAD), F32),
            pltpu.VMEM((tq // 2, tq // 2), F32),
            pltpu.VMEM((2, 1, tq), F32),
            pltpu.VMEM((2, dv + SUM_ROWS, tq), F32)]


def _flash_specs(seq, dk, dv):
    head = lambda width: pl.BlockSpec((seq, width), lambda b, h: (b, h))
    return head(dk), head(dk), head(dv), head(dv)


def _mla_flash(q, k, v, batch, seq, tq):
    q_spec, k_spec, v_spec, o_spec = _flash_specs(seq, MLA_QK_PAD, MLA_V)
    return pl.pallas_call(
        functools.partial(_mla_flash_kernel, n_q=seq // tq, tq=tq),
        grid=(batch, MLA_HEADS),
        in_specs=[q_spec, k_spec, v_spec],
        out_specs=o_spec,
        out_shape=jax.ShapeDtypeStruct((batch * seq, MLA_HEADS * MLA_V), BF16),
        scratch_shapes=[pltpu.VMEM((2, MLA_QK_PAD, tq), BF16)]
        + _flash_scratch(seq, tq, MLA_V, MLA_SCORE_BUFFERS),
        compiler_params=_cparams("parallel", "parallel"),
        name="mla_flash",
    )(q, k, v)


def _diff_flash(q, k, v, lq1, lk1, lq2, lk2, subln, lambda_init, batch, seq, tq):
    dv = 2 * DIFF_HEAD_DIM
    q_spec, k_spec, v_spec, o_spec = _flash_specs(seq, dv, dv)
    vec = lambda n: pl.BlockSpec((1, n), lambda b, h: (0, 0))
    return pl.pallas_call(
        functools.partial(_diff_flash_kernel, n_q=seq // tq, tq=tq, lambda_init=lambda_init),
        grid=(batch, DIFF_HEADS),
        in_specs=[q_spec, k_spec, v_spec, vec(DIFF_HEAD_DIM), vec(DIFF_HEAD_DIM),
                  vec(DIFF_HEAD_DIM), vec(DIFF_HEAD_DIM), vec(dv)],
        out_specs=o_spec,
        out_shape=jax.ShapeDtypeStruct((batch * seq, DIFF_HEADS * dv), BF16),
        scratch_shapes=_flash_scratch(seq, tq, dv, DIFF_SCORE_BUFFERS)[:1]
        + 2 * ([pltpu.VMEM((2, dv, tq), BF16)]
               + _flash_scratch(seq, tq, dv, DIFF_SCORE_BUFFERS)[1:]),
        compiler_params=_cparams("parallel", "parallel"),
        name="diff_flash",
    )(q, k, v, lq1.reshape(1, -1), lk1.reshape(1, -1), lq2.reshape(1, -1), lk2.reshape(1, -1),
      subln.reshape(1, -1))


def _post_attn_kernel(o_ref, x_ref, wo_ref, g1_ref, b1_ref, wu_ref, wd_ref, g2_ref, b2_ref,
                      y_ref, x1b_ref, acc_ref, *, n_chain):
    j = pl.program_id(1)
    last = pl.num_programs(1) - 1
    rc = acc_ref.shape[0] // n_chain

    def step(first, final):
        xb, base, hidden = {}, {}, {}

        def rows(c):
            return pl.ds(c * rc, rc)

        def stage0(c):
            if first:
                x1 = _layernorm(DEEPNORM_ALPHA * x_ref[rows(c), :] + _dot(o_ref[rows(c), :], wo_ref[...]),
                                g1_ref[...], b1_ref[...])
                xb[c] = x1.astype(BF16)
                x1b_ref[rows(c), :] = xb[c]
                base[c] = DEEPNORM_ALPHA * x1
            else:
                xb[c] = x1b_ref[rows(c), :]

        def stage1(c):
            hidden[c] = jnp.square(jnp.maximum(_dot(xb[c], wu_ref[...]), 0.0)).astype(BF16)

        def stage2(c):
            total = (base[c] if first else acc_ref[rows(c), :]) + _dot(hidden[c], wd_ref[...])
            if final:
                y_ref[rows(c), :] = _layernorm(total, g2_ref[...], b2_ref[...])
            else:
                acc_ref[rows(c), :] = total

        _emit_staggered((stage0, stage1, stage2), n_chain)

    @pl.when(j == 0)
    def _():
        step(first=True, final=False)

    @pl.when((j > 0) & (j < last))
    def _():
        step(first=False, final=False)

    @pl.when(j == last)
    def _():
        step(first=False, final=True)


def _post_attn(o, x2d, w_o, g1, b1, w_up_all, w_down_all, layer, g2, b2, tm, tf, n_chain):
    t = x2d.shape[0]
    row = lambda i, j: (i, 0)
    full = lambda i, j: (0, 0)
    vec = pl.BlockSpec((1, D_MODEL), full)
    assert D_FF // tf >= 2
    return pl.pallas_call(
        functools.partial(_post_attn_kernel, n_chain=n_chain),
        grid=(t // tm, D_FF // tf),
        in_specs=[
            pl.BlockSpec((tm, D_MODEL), row),
            pl.BlockSpec((tm, D_MODEL), row),
            pl.BlockSpec((D_MODEL, D_MODEL), full),
            vec, vec,
            pl.BlockSpec((None, D_MODEL, tf), lambda i, j: (layer, 0, j)),
            pl.BlockSpec((None, tf, D_MODEL), lambda i, j: (layer, j, 0)),
            vec, vec,
        ],
        out_specs=pl.BlockSpec((tm, D_MODEL), row),
        out_shape=jax.ShapeDtypeStruct((t, D_MODEL), F32),
        scratch_shapes=[pltpu.VMEM((tm, D_MODEL), BF16), pltpu.VMEM((tm, D_MODEL), F32)],
        compiler_params=_cparams("parallel", "arbitrary"),
        name="post_attn",
    )(o, x2d, w_o.astype(BF16), g1.reshape(1, -1), b1.reshape(1, -1), w_up_all, w_down_all,
      g2.reshape(1, -1), b2.reshape(1, -1))


def kernel(x, positions, ln_mix_g, ln_mix_b, mla_w_dq, mla_q_norm, mla_w_uq, mla_w_dkv, mla_kv_norm, mla_w_ukv, mla_w_o, diff_w_qkv, diff_lambda_q1, diff_lambda_k1, diff_lambda_q2, diff_lambda_k2, diff_subln, diff_w_o, ln_ffn_g, ln_ffn_b, ffn_w_up, ffn_w_down):
    batch, seq, d_model = x.shape
    assert d_model == D_MODEL and seq % ATTN_TILE == 0
    assert (batch * seq) % POST_ROWS == 0 and (batch * seq) % PROJ_ROWS == 0
    x2d = x.reshape(batch * seq, D_MODEL)
    pos = positions.reshape(-1)

    mla_c, mla_s = _rope_tables(pos, MLA_ROPE // 2, LANES, 1, MLA_ROPE_THETA)
    diff_c, diff_s = _rope_tables(pos, DIFF_ROT_DIM // 2, DIFF_HEAD_DIM, 2, DIFF_ROPE_THETA)

    post_tiles = (POST_ROWS, FF_CHUNK, POST_CHAINS)
    w_up_all = ffn_w_up.astype(BF16)
    w_down_all = ffn_w_down.astype(BF16)
    q, k, v = _mla_proj(x2d, mla_c, mla_s, mla_w_dq[0], mla_q_norm[0], mla_w_uq[0],
                        mla_w_dkv[0], mla_kv_norm[0], mla_w_ukv[0], PROJ_ROWS)
    o = _mla_flash(q, k, v, batch, seq, ATTN_TILE)
    x2d = _post_attn(o, x2d, mla_w_o[0], ln_mix_g[0], ln_mix_b[0], w_up_all, w_down_all, 0,
                     ln_ffn_g[0], ln_ffn_b[0], *post_tiles)

    lambda_init = 0.8 - 0.6 * math.exp(-0.3 * 1)
    q, k, v = _diff_proj(x2d, diff_c, diff_s, diff_w_qkv[0], PROJ_ROWS)
    o = _diff_flash(q, k, v, diff_lambda_q1[0], diff_lambda_k1[0], diff_lambda_q2[0],
                    diff_lambda_k2[0], diff_subln[0], lambda_init, batch, seq, ATTN_TILE)
    x2d = _post_attn(o, x2d, diff_w_o[0], ln_mix_g[1], ln_mix_b[1], w_up_all, w_down_all, 1,
                     ln_ffn_g[1], ln_ffn_b[1], *post_tiles)
    return x2d.reshape(batch, seq, D_MODEL)
```

```python
import functools
import math

import jax
import jax.numpy as jnp
from jax import lax
from jax.experimental import pallas as pl
from jax.experimental.pallas import tpu as pltpu

D_MODEL = 1024
DEPTH = 2

MLA_NOPE = 128
MLA_ROPE = 64
MLA_V = 128
MLA_HEADS = 8
MLA_Q_LORA = 384
MLA_KV_LORA = 256
MLA_ROPE_THETA = 10000.0
MLA_QK_PAD = 256

DIFF_HEAD_DIM = 64
DIFF_HEADS = 8
DIFF_ROT_DIM = 16
DIFF_ROPE_THETA = 500000.0

D_FF = 4 * D_MODEL
DEEPNORM_ALPHA = (2 * DEPTH) ** 0.25
LN_EPS = 1e-5
RMS_EPS = 1e-6

LANES = 128
SUM_ROWS = 16
NEG_BIG = -1e30
LOG2_E = math.log2(math.e)
VMEM_LIMIT_BYTES = 48 * 1024 * 1024
POST_VMEM_LIMIT_BYTES = 56 * 1024 * 1024

F32 = jnp.float32
BF16 = jnp.bfloat16

TABLE_ROWS = 512
PROJ_ROWS = 1024
MLA_PROJ_CHAINS = 4
ATTN_TILE = 512
MLA_SCORE_BUFFERS = 3
DIFF_SCORE_BUFFERS = 2
POST_ROWS = 1024
POST_CHAINS = 4
FF_CHUNK = 2048


def _cparams(*sem, vmem_limit_bytes=VMEM_LIMIT_BYTES):
    return pltpu.CompilerParams(dimension_semantics=sem, vmem_limit_bytes=vmem_limit_bytes)


def _dot(a, b):
    return jnp.dot(a, b, preferred_element_type=F32)


def _rms(x, g):
    return x * lax.rsqrt(jnp.mean(jnp.square(x), axis=-1, keepdims=True) + RMS_EPS) * g


def _layernorm(y, g, b):
    mu = jnp.mean(y, axis=-1, keepdims=True)
    d = y - mu
    var = jnp.mean(jnp.square(d), axis=-1, keepdims=True)
    return d * lax.rsqrt(var + LN_EPS) * g + b


def _rope_table_kernel(pos_ref, invf_ref, c_ref, s_ref, *, n_tok, half, dup, group):
    rows = pos_ref.shape[0]
    width = LANES // n_tok
    lane = lax.broadcasted_iota(jnp.int32, (rows, LANES), 1)
    pos = pos_ref[...]
    pos_dense = jnp.broadcast_to(pos[:, n_tok - 1:n_tok], (rows, LANES))
    for a in reversed(range(n_tok - 1)):
        pos_dense = jnp.where(lane < (a + 1) * width, pos[:, a:a + 1], pos_dense)
    ang = pos_dense.astype(F32) * invf_ref[...]
    cos_d = jnp.cos(ang)
    sin_d = jnp.sin(ang)
    neg_sin_d = -sin_d

    def spread(first_d, second_d, a, fill):
        rolled = {}

        def at(x_d, key, shift):
            if (key, shift) not in rolled:
                rolled[key, shift] = x_d if shift == 0 else pltpu.roll(x_d, shift, axis=1)
            return rolled[key, shift]

        out = jnp.full((rows, LANES), fill, F32)
        for g0 in range(0, LANES, group):
            for h, x_d in enumerate((first_d, second_d)):
                src = a * width + (h * half if dup == 2 else 0)
                dst = g0 + h * half
                val = at(x_d, h, (dst - src) % LANES)
                out = jnp.where((lane >= dst) & (lane < dst + half), val, out)
        return out

    for a in range(n_tok):
        c_ref[a] = spread(cos_d, cos_d, a, 1.0)
        s_ref[a] = spread(neg_sin_d, sin_d, a, 0.0)


def _rope_tables(positions, half, group, dup, theta, rows_per_step=TABLE_ROWS):
    t = positions.size
    n_tok = LANES // (half * dup)
    rows = t // n_tok
    assert rows % rows_per_step == 0
    inv_freq = theta ** (-jnp.arange(0, 2 * half, 2, dtype=F32) / (2 * half))
    invf = jnp.tile(inv_freq, LANES // half).reshape(1, LANES)
    pos_t = positions.reshape(n_tok, rows).T
    out_spec = pl.BlockSpec((n_tok, rows_per_step, LANES), lambda i: (0, i, 0))
    c, s = pl.pallas_call(
        functools.partial(_rope_table_kernel, n_tok=n_tok, half=half, dup=dup, group=group),
        grid=(rows // rows_per_step,),
        in_specs=[pl.BlockSpec((rows_per_step, n_tok), lambda i: (i, 0)),
                  pl.BlockSpec((1, LANES), lambda i: (0, 0))],
        out_specs=[out_spec, out_spec],
        out_shape=[jax.ShapeDtypeStruct((n_tok, rows, LANES), F32)] * 2,
        compiler_params=_cparams("parallel"),
        name="rope_table",
    )(pos_t, invf)
    return c.reshape(t, LANES), s.reshape(t, LANES)


def _rotate_half_in_lanes(x, half, group, c_tab, s_tab):
    lane = lax.broadcasted_iota(jnp.int32, x.shape, 1)
    from_above = pltpu.roll(x, LANES - half, axis=1)
    from_below = pltpu.roll(x, half, axis=1)
    partner = jnp.where(lane % group < half, from_above, from_below)
    return x * c_tab + partner * s_tab


def _emit_staggered(stages, n_chain):
    for slot in range(n_chain + len(stages) - 1):
        for depth, stage in enumerate(stages):
            if 0 <= slot - depth < n_chain:
                stage(slot - depth)


def _mla_proj_kernel(x_ref, wdown_ref, qn_ref, wuq_ref, kvn_ref, wukv_ref,
                     c_ref, s_ref, q_ref, k_ref, v_ref, *, scale, n_chain):
    rc = x_ref.shape[0] // n_chain
    cq, c_kv, k_pe, q, kv = {}, {}, {}, {}, {}

    def rows(c):
        return pl.ds(c * rc, rc)

    def rope_of(c):
        return functools.partial(_rotate_half_in_lanes, half=MLA_ROPE // 2, group=LANES,
                                 c_tab=c_ref[rows(c), :], s_tab=s_ref[rows(c), :])

    def down_proj(c):
        down = _dot(x_ref[rows(c), :].astype(BF16), wdown_ref[...])
        cq[c] = _rms(down[:, :MLA_Q_LORA], qn_ref[...]).astype(BF16)
        ckv = down[:, MLA_Q_LORA:]
        c_kv[c] = _rms(ckv[:, :MLA_KV_LORA], kvn_ref[...]).astype(BF16)
        k_pe[c] = rope_of(c)(ckv[:, MLA_KV_LORA:]).astype(BF16)

    def up_proj(c):
        q[c] = _dot(cq[c], wuq_ref[...])
        kv[c] = _dot(c_kv[c], wukv_ref[...])

    def write_heads(c):
        rope = rope_of(c)
        for h in range(MLA_HEADS):
            lo, mid, hi = h * MLA_QK_PAD, h * MLA_QK_PAD + LANES, (h + 1) * MLA_QK_PAD
            q_ref[rows(c), lo:mid] = (q[c][:, lo:mid] * scale).astype(BF16)
            q_ref[rows(c), mid:hi] = (rope(q[c][:, mid:hi]) * scale).astype(BF16)
            k_ref[rows(c), lo:mid] = kv[c][:, lo:mid].astype(BF16)
            k_ref[rows(c), mid:hi] = k_pe[c]
            v_ref[rows(c), h * MLA_V:(h + 1) * MLA_V] = kv[c][:, mid:hi].astype(BF16)

    _emit_staggered((down_proj, up_proj, write_heads), n_chain)


def _mla_proj(x2d, c_tab, s_tab, w_dq, q_norm, w_uq, w_dkv, kv_norm, w_ukv, tm):
    t = x2d.shape[0]
    h = MLA_HEADS
    pad = MLA_QK_PAD - MLA_NOPE - MLA_ROPE
    wuq = jnp.pad(w_uq.reshape(MLA_Q_LORA, h, MLA_NOPE + MLA_ROPE), ((0, 0), (0, 0), (0, pad)))
    wuq = wuq.reshape(MLA_Q_LORA, h * MLA_QK_PAD).astype(BF16)
    wdown = jnp.concatenate([w_dq, jnp.pad(w_dkv, ((0, 0), (0, pad)))], axis=1).astype(BF16)
    row = lambda i: (i, 0)
    full = lambda i: (0, 0)
    return pl.pallas_call(
        functools.partial(_mla_proj_kernel, scale=LOG2_E * (MLA_NOPE + MLA_ROPE) ** -0.5,
                          n_chain=MLA_PROJ_CHAINS),
        grid=(t // tm,),
        in_specs=[
            pl.BlockSpec((tm, D_MODEL), row),
            pl.BlockSpec((D_MODEL, MLA_Q_LORA + MLA_KV_LORA + LANES), full),
            pl.BlockSpec((1, MLA_Q_LORA), full),
            pl.BlockSpec((MLA_Q_LORA, h * MLA_QK_PAD), full),
            pl.BlockSpec((1, MLA_KV_LORA), full),
            pl.BlockSpec((MLA_KV_LORA, h * (MLA_NOPE + MLA_V)), full),
            pl.BlockSpec((tm, LANES), row),
            pl.BlockSpec((tm, LANES), row),
        ],
        out_specs=[
            pl.BlockSpec((tm, h * MLA_QK_PAD), row),
            pl.BlockSpec((tm, h * MLA_QK_PAD), row),
            pl.BlockSpec((tm, h * MLA_V), row),
        ],
        out_shape=[
            jax.ShapeDtypeStruct((t, h * MLA_QK_PAD), BF16),
            jax.ShapeDtypeStruct((t, h * MLA_QK_PAD), BF16),
            jax.ShapeDtypeStruct((t, h * MLA_V), BF16),
        ],
        compiler_params=_cparams("parallel"),
        name="mla_proj",
    )(x2d, wdown, q_norm.reshape(1, -1), wuq, kv_norm.reshape(1, -1), w_ukv.astype(BF16),
      c_tab, s_tab)


def _diff_proj_kernel(x_ref, w_ref, c_ref, s_ref, q_ref, k_ref, v_ref, *, scale):
    rope = functools.partial(_rotate_half_in_lanes, half=DIFF_ROT_DIM // 2, group=DIFF_HEAD_DIM,
                             c_tab=c_ref[...], s_tab=s_ref[...])
    qkv = _dot(x_ref[...].astype(BF16), w_ref[...])
    for j in range(D_MODEL // LANES):
        lo, hi = j * LANES, (j + 1) * LANES
        q_ref[:, lo:hi] = (rope(qkv[:, lo:hi]) * scale).astype(BF16)
        k_ref[:, lo:hi] = rope(qkv[:, D_MODEL + lo:D_MODEL + hi]).astype(BF16)
    v_ref[...] = qkv[:, 2 * D_MODEL:].astype(BF16)


def _diff_proj(x2d, c_tab, s_tab, w_qkv, tm):
    t = x2d.shape[0]
    row = lambda i: (i, 0)
    full = lambda i: (0, 0)
    return pl.pallas_call(
        functools.partial(_diff_proj_kernel, scale=LOG2_E * DIFF_HEAD_DIM ** -0.5),
        grid=(t // tm,),
        in_specs=[
            pl.BlockSpec((tm, D_MODEL), row),
            pl.BlockSpec((D_MODEL, 3 * D_MODEL), full),
            pl.BlockSpec((tm, LANES), row),
            pl.BlockSpec((tm, LANES), row),
        ],
        out_specs=[pl.BlockSpec((tm, D_MODEL), row)] * 3,
        out_shape=[jax.ShapeDtypeStruct((t, D_MODEL), BF16)] * 3,
        compiler_params=_cparams("parallel"),
        name="diff_proj",
    )(x2d, w_qkv.astype(BF16), c_tab, s_tab)


def _transpose_v(v_ref, vt_ref, tk):
    dv = v_ref.shape[1]
    for j in range(vt_ref.shape[0]):
        vt_ref[j, :dv, :] = v_ref[j * tk:(j + 1) * tk, :].astype(F32).T.astype(vt_ref.dtype)
        vt_ref[j, dv:, :] = jnp.ones((SUM_ROWS, tk), vt_ref.dtype)


def _flash_head(q_block, k_ref, vt_ref, s_ref, tri_ref, m_ref, acc_ref, finish, n_q, tq):
    rows = tq
    n_buf = s_ref.shape[0]
    dv = acc_ref.shape[1] - SUM_ROWS
    half = tq // 2
    tri_ref[...] = jnp.where(lax.broadcasted_iota(jnp.int32, (half, half), 0)
                             <= lax.broadcasted_iota(jnp.int32, (half, half), 1), 0.0, NEG_BIG)
    steps = [(i, j) for i in range(n_q) for j in range(i + 1)]

    def scores(t):
        i, j = steps[t]
        k = k_ref[j * tq:(j + 1) * tq, :]
        s_ref[t % n_buf] = _dot(k, q_block(i, j == 0))

    def update(par, cols, s, vt, first):
        m_blk = jnp.max(s, axis=0, keepdims=True)
        if first:
            m_new = m_blk
            acc_ref[par, :, cols] = _dot(vt, jnp.exp2(s - m_new).astype(BF16))
        else:
            m_prev = m_ref[par, :, cols]
            m_new = jnp.maximum(m_prev, m_blk)
            alpha = jnp.exp2(m_prev - m_new)
            acc_ref[par, :, cols] = (alpha * acc_ref[par, :, cols]
                                     + _dot(vt, jnp.exp2(s - m_new).astype(BF16)))
        m_ref[par, :, cols] = m_new

    def consume(t):
        i, j = steps[t]
        s_t = s_ref.at[t % n_buf]
        par = i % 2
        if j < i:
            update(par, slice(0, rows), s_t[...], vt_ref[j], first=j == 0)
            return
        for g in range(rows // half):
            cols = slice(g * half, (g + 1) * half)
            if g % 2 == 0:
                s = s_t[:half, cols] + tri_ref[...]
                vt = vt_ref[j, :, :half]
            else:
                s = jnp.concatenate([s_t[:half, cols], s_t[half:, cols] + tri_ref[...]], axis=0)
                vt = vt_ref[j]
            update(par, cols, s, vt, first=i == 0)
        finish(i, acc_ref[par, :dv, :], acc_ref[par, dv:dv + 1, :])

    return len(steps), n_buf - 1, scores, consume


def _run_interleaved(programs):
    n_steps, ahead = programs[0][:2]
    for _, _, scores, _ in programs:
        for t in range(min(ahead, n_steps)):
            scores(t)
    for t in range(n_steps):
        for _, _, scores, consume in programs:
            if t + ahead < n_steps:
                scores(t + ahead)
            consume(t)


def _mla_flash_kernel(q_ref, k_ref, v_ref, o_ref, qt_ref, vt_ref, s_ref, tri_ref, m_ref, acc_ref,
                      *, n_q, tq):
    _transpose_v(v_ref, vt_ref, tq)

    def q_block(i, first):
        if first:
            qt_ref[i % 2] = q_ref[i * tq:(i + 1) * tq, :].astype(F32).T.astype(BF16)
        return qt_ref[i % 2]

    def finish(i, acc, l):
        o_ref[i * tq:(i + 1) * tq, :] = (acc / l).T.astype(o_ref.dtype)

    _run_interleaved([_flash_head(q_block, k_ref, vt_ref, s_ref, tri_ref, m_ref, acc_ref,
                                  finish, n_q, tq)])


def _diff_flash_kernel(q_ref, k_ref, v_ref, lq1_ref, lk1_ref, lq2_ref, lk2_ref, subln_ref,
                       o_ref, vt_ref, *scratch, n_q, tq, lambda_init):
    _transpose_v(v_ref, vt_ref, tq)
    lam = (jnp.exp(jnp.sum(lq1_ref[...] * lk1_ref[...]))
           - jnp.exp(jnp.sum(lq2_ref[...] * lk2_ref[...])) + lambda_init)
    subln = subln_ref[...]
    per_map = len(scratch) // 2
    maps = [scratch[c * per_map:(c + 1) * per_map] for c in range(2)]
    acc1_ref = maps[0][-1]
    dv = v_ref.shape[1]

    programs = []
    for c, (qt_ref, s_ref, tri_ref, m_ref, acc_ref) in enumerate(maps):
        def q_block(i, first, c=c, qt_ref=qt_ref):
            if first:
                q = q_ref[i * tq:(i + 1) * tq, :]
                lane = lax.broadcasted_iota(jnp.int32, q.shape, 1)
                keep = (lane < DIFF_HEAD_DIM) if c == 0 else (lane >= DIFF_HEAD_DIM)
                qt_ref[i % 2] = jnp.where(keep, q, jnp.zeros_like(q)).astype(F32).T.astype(BF16)
            return qt_ref[i % 2]

        def finish(i, acc, l, c=c):
            if c == 0:
                return
            par = i % 2
            o1_t = acc1_ref[par, :dv, :] / acc1_ref[par, dv:dv + 1, :]
            o_t = o1_t - lam * (acc / l)
            o = _rms(o_t.T, subln) * (1.0 - lambda_init)
            o_ref[i * tq:(i + 1) * tq, :] = o.astype(o_ref.dtype)

        programs.append(_flash_head(q_block, k_ref, vt_ref, s_ref, tri_ref, m_ref, acc_ref,
                                    finish, n_q, tq))
    _run_interleaved(programs)


def _flash_scratch(seq, tq, dv, n_buf):
    return [pltpu.VMEM((seq // tq, dv + SUM_ROWS, tq), BF16),
            pltpu.VMEM((n_buf, tq, tq), F32),
            pltpu.VMEM((tq // 2, tq // 2), F32),
            pltpu.VMEM((2, 1, tq), F32),
            pltpu.VMEM((2, dv + SUM_ROWS, tq), F32)]


def _flash_specs(seq, dk, dv):
    head = lambda width: pl.BlockSpec((seq, width), lambda b, h: (b, h))
    return head(dk), head(dk), head(dv), head(dv)


def _mla_flash(q, k, v, batch, seq, tq):
    q_spec, k_spec, v_spec, o_spec = _flash_specs(seq, MLA_QK_PAD, MLA_V)
    return pl.pallas_call(
        functools.partial(_mla_flash_kernel, n_q=seq // tq, tq=tq),
        grid=(batch, MLA_HEADS),
        in_specs=[q_spec, k_spec, v_spec],
        out_specs=o_spec,
        out_shape=jax.ShapeDtypeStruct((batch * seq, MLA_HEADS * MLA_V), BF16),
        scratch_shapes=[pltpu.VMEM((2, MLA_QK_PAD, tq), BF16)]
        + _flash_scratch(seq, tq, MLA_V, MLA_SCORE_BUFFERS),
        compiler_params=_cparams("parallel", "parallel"),
        name="mla_flash",
    )(q, k, v)


def _diff_flash(q, k, v, lq1, lk1, lq2, lk2, subln, lambda_init, batch, seq, tq):
    dv = 2 * DIFF_HEAD_DIM
    q_spec, k_spec, v_spec, o_spec = _flash_specs(seq, dv, dv)
    vec = lambda n: pl.BlockSpec((1, n), lambda b, h: (0, 0))
    return pl.pallas_call(
        functools.partial(_diff_flash_kernel, n_q=seq // tq, tq=tq, lambda_init=lambda_init),
        grid=(batch, DIFF_HEADS),
        in_specs=[q_spec, k_spec, v_spec, vec(DIFF_HEAD_DIM), vec(DIFF_HEAD_DIM),
                  vec(DIFF_HEAD_DIM), vec(DIFF_HEAD_DIM), vec(dv)],
        out_specs=o_spec,
        out_shape=jax.ShapeDtypeStruct((batch * seq, DIFF_HEADS * dv), BF16),
        scratch_shapes=_flash_scratch(seq, tq, dv, DIFF_SCORE_BUFFERS)[:1]
        + 2 * ([pltpu.VMEM((2, dv, tq), BF16)]
               + _flash_scratch(seq, tq, dv, DIFF_SCORE_BUFFERS)[1:]),
        compiler_params=_cparams("parallel", "parallel"),
        name="diff_flash",
    )(q, k, v, lq1.reshape(1, -1), lk1.reshape(1, -1), lq2.reshape(1, -1), lk2.reshape(1, -1),
      subln.reshape(1, -1))


def _post_attn_kernel(o_ref, x_ref, wo_ref, g1_ref, b1_ref, wu_ref, wd_ref, g2_ref, b2_ref,
                      y_ref, x1b_ref, acc_ref, *, n_chain):
    j = pl.program_id(1)
    last = pl.num_programs(1) - 1
    rc = acc_ref.shape[0] // n_chain

    def step(first, final):
        xb, base, hidden = {}, {}, {}

        def rows(c):
            return pl.ds(c * rc, rc)

        def stage0(c):
            if first:
                x1 = _layernorm(DEEPNORM_ALPHA * x_ref[rows(c), :] + _dot(o_ref[rows(c), :], wo_ref[...]),
                                g1_ref[...], b1_ref[...])
                xb[c] = x1.astype(BF16)
                x1b_ref[rows(c), :] = xb[c]
                base[c] = DEEPNORM_ALPHA * x1
            else:
                xb[c] = x1b_ref[rows(c), :]

        def stage1(c):
            hidden[c] = jnp.square(jnp.maximum(_dot(xb[c], wu_ref[...]), 0.0)).astype(BF16)

        def stage2(c):
            total = (base[c] if first else acc_ref[rows(c), :]) + _dot(hidden[c], wd_ref[...])
            if final:
                y_ref[rows(c), :] = _layernorm(total, g2_ref[...], b2_ref[...])
            else:
                acc_ref[rows(c), :] = total

        _emit_staggered((stage0, stage1, stage2), n_chain)

    @pl.when(j == 0)
    def _():
        step(first=True, final=False)

    @pl.when((j > 0) & (j < last))
    def _():
        step(first=False, final=False)

    @pl.when(j == last)
    def _():
        step(first=False, final=True)


def _post_attn(o, x2d, w_o, g1, b1, w_up_all, w_down_all, layer, g2, b2, tm, tf, n_chain):
    t = x2d.shape[0]
    row = lambda i, j: (i, 0)
    full = lambda i, j: (0, 0)
    vec = pl.BlockSpec((1, D_MODEL), full)
    assert D_FF // tf >= 2
    return pl.pallas_call(
        functools.partial(_post_attn_kernel, n_chain=n_chain),
        grid=(t // tm, D_FF // tf),
        in_specs=[
            pl.BlockSpec((tm, D_MODEL), row),
            pl.BlockSpec((tm, D_MODEL), row),
            pl.BlockSpec((D_MODEL, D_MODEL), full),
            vec, vec,
            pl.BlockSpec((None, D_MODEL, tf), lambda i, j: (layer, 0, j)),
            pl.BlockSpec((None, tf, D_MODEL), lambda i, j: (layer, j, 0)),
            vec, vec,
        ],
        out_specs=pl.BlockSpec((tm, D_MODEL), row),
        out_shape=jax.ShapeDtypeStruct((t, D_MODEL), F32),
        scratch_shapes=[pltpu.VMEM((tm, D_MODEL), BF16), pltpu.VMEM((tm, D_MODEL), F32)],
        compiler_params=_cparams("parallel", "arbitrary", vmem_limit_bytes=POST_VMEM_LIMIT_BYTES),
        name="post_attn",
    )(o, x2d, w_o.astype(BF16), g1.reshape(1, -1), b1.reshape(1, -1), w_up_all, w_down_all,
      g2.reshape(1, -1), b2.reshape(1, -1))


def kernel(x, positions, ln_mix_g, ln_mix_b, mla_w_dq, mla_q_norm, mla_w_uq, mla_w_dkv, mla_kv_norm, mla_w_ukv, mla_w_o, diff_w_qkv, diff_lambda_q1, diff_lambda_k1, diff_lambda_q2, diff_lambda_k2, diff_subln, diff_w_o, ln_ffn_g, ln_ffn_b, ffn_w_up, ffn_w_down):
    batch, seq, d_model = x.shape
    assert d_model == D_MODEL and seq % ATTN_TILE == 0
    assert (batch * seq) % POST_ROWS == 0 and (batch * seq) % PROJ_ROWS == 0
    x2d = x.reshape(batch * seq, D_MODEL)
    pos = positions.reshape(-1)

    mla_c, mla_s = _rope_tables(pos, MLA_ROPE // 2, LANES, 1, MLA_ROPE_THETA)
    diff_c, diff_s = _rope_tables(pos, DIFF_ROT_DIM // 2, DIFF_HEAD_DIM, 2, DIFF_ROPE_THETA)

    post_tiles = (POST_ROWS, FF_CHUNK, POST_CHAINS)
    w_up_all = ffn_w_up.astype(BF16)
    w_down_all = ffn_w_down.astype(BF16)
    q, k, v = _mla_proj(x2d, mla_c, mla_s, mla_w_dq[0], mla_q_norm[0], mla_w_uq[0],
                        mla_w_dkv[0], mla_kv_norm[0], mla_w_ukv[0], PROJ_ROWS)
    o = _mla_flash(q, k, v, batch, seq, ATTN_TILE)
    x2d = _post_attn(o, x2d, mla_w_o[0], ln_mix_g[0], ln_mix_b[0], w_up_all, w_down_all, 0,
                     ln_ffn_g[0], ln_ffn_b[0], *post_tiles)

    lambda_init = 0.8 - 0.6 * math.exp(-0.3 * 1)
    q, k, v = _diff_proj(x2d, diff_c, diff_s, diff_w_qkv[0], PROJ_ROWS)
    o = _diff_flash(q, k, v, diff_lambda_q1[0], diff_lambda_k1[0], diff_lambda_q2[0],
                    diff_lambda_k2[0], diff_subln[0], lambda_init, batch, seq, ATTN_TILE)
    x2d = _post_attn(o, x2d, diff_w_o[0], ln_mix_g[1], ln_mix_b[1], w_up_all, w_down_all, 1,
                     ln_ffn_g[1], ln_ffn_b[1], *post_tiles)
    return x2d.reshape(batch, seq, D_MODEL)
```

```python
import functools
import math

import jax
import jax.numpy as jnp
from jax import lax
from jax.experimental import pallas as pl
from jax.experimental.pallas import tpu as pltpu

D_MODEL = 1024
DEPTH = 2

MLA_NOPE = 128
MLA_ROPE = 64
MLA_V = 128
MLA_HEADS = 8
MLA_Q_LORA = 384
MLA_KV_LORA = 256
MLA_ROPE_THETA = 10000.0
MLA_QK_PAD = 256

DIFF_HEAD_DIM = 64
DIFF_HEADS = 8
DIFF_ROT_DIM = 16
DIFF_ROPE_THETA = 500000.0

D_FF = 4 * D_MODEL
DEEPNORM_ALPHA = (2 * DEPTH) ** 0.25
LN_EPS = 1e-5
RMS_EPS = 1e-6

LANES = 128
SUM_ROWS = 16
NEG_BIG = -1e30
LOG2_E = math.log2(math.e)
VMEM_LIMIT_BYTES = 48 * 1024 * 1024
POST_VMEM_LIMIT_BYTES = 56 * 1024 * 1024

F32 = jnp.float32
BF16 = jnp.bfloat16

TABLE_ROWS = 512
PROJ_ROWS = 1024
MLA_PROJ_CHAINS = 4
ATTN_TILE = 512
MLA_SCORE_BUFFERS = 3
DIFF_SCORE_BUFFERS = 2
POST_ROWS = 1024
POST_CHAINS = 4
FF_CHUNK = 1024


def _cparams(*sem):
    return pltpu.CompilerParams(dimension_semantics=sem, vmem_limit_bytes=VMEM_LIMIT_BYTES)


def _dot(a, b):
    return jnp.dot(a, b, preferred_element_type=F32)


def _rms(x, g):
    return x * lax.rsqrt(jnp.mean(jnp.square(x), axis=-1, keepdims=True) + RMS_EPS) * g


def _layernorm(y, g, b):
    mu = jnp.mean(y, axis=-1, keepdims=True)
    d = y - mu
    var = jnp.mean(jnp.square(d), axis=-1, keepdims=True)
    return d * lax.rsqrt(var + LN_EPS) * g + b


def _rope_table_kernel(pos_ref, invf_ref, c_ref, s_ref, *, n_tok, half, dup, group):
    rows = pos_ref.shape[0]
    width = LANES // n_tok
    lane = lax.broadcasted_iota(jnp.int32, (rows, LANES), 1)
    pos = pos_ref[...]
    pos_dense = jnp.broadcast_to(pos[:, n_tok - 1:n_tok], (rows, LANES))
    for a in reversed(range(n_tok - 1)):
        pos_dense = jnp.where(lane < (a + 1) * width, pos[:, a:a + 1], pos_dense)
    ang = pos_dense.astype(F32) * invf_ref[...]
    cos_d = jnp.cos(ang)
    sin_d = jnp.sin(ang)
    neg_sin_d = -sin_d

    def spread(first_d, second_d, a, fill):
        rolled = {}

        def at(x_d, key, shift):
            if (key, shift) not in rolled:
                rolled[key, shift] = x_d if shift == 0 else pltpu.roll(x_d, shift, axis=1)
            return rolled[key, shift]

        out = jnp.full((rows, LANES), fill, F32)
        for g0 in range(0, LANES, group):
            for h, x_d in enumerate((first_d, second_d)):
                src = a * width + (h * half if dup == 2 else 0)
                dst = g0 + h * half
                val = at(x_d, h, (dst - src) % LANES)
                out = jnp.where((lane >= dst) & (lane < dst + half), val, out)
        return out

    for a in range(n_tok):
        c_ref[a] = spread(cos_d, cos_d, a, 1.0)
        s_ref[a] = spread(neg_sin_d, sin_d, a, 0.0)


def _rope_tables(positions, half, group, dup, theta, rows_per_step=TABLE_ROWS):
    t = positions.size
    n_tok = LANES // (half * dup)
    rows = t // n_tok
    assert rows % rows_per_step == 0
    inv_freq = theta ** (-jnp.arange(0, 2 * half, 2, dtype=F32) / (2 * half))
    invf = jnp.tile(inv_freq, LANES // half).reshape(1, LANES)
    pos_t = positions.reshape(n_tok, rows).T
    out_spec = pl.BlockSpec((n_tok, rows_per_step, LANES), lambda i: (0, i, 0))
    c, s = pl.pallas_call(
        functools.partial(_rope_table_kernel, n_tok=n_tok, half=half, dup=dup, group=group),
        grid=(rows // rows_per_step,),
        in_specs=[pl.BlockSpec((rows_per_step, n_tok), lambda i: (i, 0)),
                  pl.BlockSpec((1, LANES), lambda i: (0, 0))],
        out_specs=[out_spec, out_spec],
        out_shape=[jax.ShapeDtypeStruct((n_tok, rows, LANES), F32)] * 2,
        compiler_params=_cparams("parallel"),
        name="rope_table",
    )(pos_t, invf)
    return c.reshape(t, LANES), s.reshape(t, LANES)


def _rotate_half_in_lanes(x, half, group, c_tab, s_tab):
    lane = lax.broadcasted_iota(jnp.int32, x.shape, 1)
    from_above = pltpu.roll(x, LANES - half, axis=1)
    from_below = pltpu.roll(x, half, axis=1)
    partner = jnp.where(lane % group < half, from_above, from_below)
    return x * c_tab + partner * s_tab


def _emit_staggered(stages, n_chain):
    for slot in range(n_chain + len(stages) - 1):
        for depth, stage in enumerate(stages):
            if 0 <= slot - depth < n_chain:
                stage(slot - depth)


def _mla_proj_kernel(x_ref, wdown_ref, qn_ref, wuq_ref, kvn_ref, wukv_ref,
                     c_ref, s_ref, q_ref, k_ref, v_ref, *, scale, n_chain):
    rc = x_ref.shape[0] // n_chain
    cq, c_kv, k_pe, q, kv = {}, {}, {}, {}, {}

    def rows(c):
        return pl.ds(c * rc, rc)

    def rope_of(c):
        return functools.partial(_rotate_half_in_lanes, half=MLA_ROPE // 2, group=LANES,
                                 c_tab=c_ref[rows(c), :], s_tab=s_ref[rows(c), :])

    def down_proj(c):
        down = _dot(x_ref[rows(c), :].astype(BF16), wdown_ref[...])
        cq[c] = _rms(down[:, :MLA_Q_LORA], qn_ref[...]).astype(BF16)
        ckv = down[:, MLA_Q_LORA:]
        c_kv[c] = _rms(ckv[:, :MLA_KV_LORA], kvn_ref[...]).astype(BF16)
        k_pe[c] = rope_of(c)(ckv[:, MLA_KV_LORA:]).astype(BF16)

    def up_proj(c):
        q[c] = _dot(cq[c], wuq_ref[...])
        kv[c] = _dot(c_kv[c], wukv_ref[...])

    def write_heads(c):
        rope = rope_of(c)
        for h in range(MLA_HEADS):
            lo, mid, hi = h * MLA_QK_PAD, h * MLA_QK_PAD + LANES, (h + 1) * MLA_QK_PAD
            q_ref[rows(c), lo:mid] = (q[c][:, lo:mid] * scale).astype(BF16)
            q_ref[rows(c), mid:hi] = (rope(q[c][:, mid:hi]) * scale).astype(BF16)
            k_ref[rows(c), lo:mid] = kv[c][:, lo:mid].astype(BF16)
            k_ref[rows(c), mid:hi] = k_pe[c]
            v_ref[rows(c), h * MLA_V:(h + 1) * MLA_V] = kv[c][:, mid:hi].astype(BF16)

    _emit_staggered((down_proj, up_proj, write_heads), n_chain)


def _mla_proj(x2d, c_tab, s_tab, w_dq, q_norm, w_uq, w_dkv, kv_norm, w_ukv, tm):
    t = x2d.shape[0]
    h = MLA_HEADS
    pad = MLA_QK_PAD - MLA_NOPE - MLA_ROPE
    wuq = jnp.pad(w_uq.reshape(MLA_Q_LORA, h, MLA_NOPE + MLA_ROPE), ((0, 0), (0, 0), (0, pad)))
    wuq = wuq.reshape(MLA_Q_LORA, h * MLA_QK_PAD).astype(BF16)
    wdown = jnp.concatenate([w_dq, jnp.pad(w_dkv, ((0, 0), (0, pad)))], axis=1).astype(BF16)
    row = lambda i: (i, 0)
    full = lambda i: (0, 0)
    return pl.pallas_call(
        functools.partial(_mla_proj_kernel, scale=LOG2_E * (MLA_NOPE + MLA_ROPE) ** -0.5,
                          n_chain=MLA_PROJ_CHAINS),
        grid=(t // tm,),
        in_specs=[
            pl.BlockSpec((tm, D_MODEL), row),
            pl.BlockSpec((D_MODEL, MLA_Q_LORA + MLA_KV_LORA + LANES), full),
            pl.BlockSpec((1, MLA_Q_LORA), full),
            pl.BlockSpec((MLA_Q_LORA, h * MLA_QK_PAD), full),
            pl.BlockSpec((1, MLA_KV_LORA), full),
            pl.BlockSpec((MLA_KV_LORA, h * (MLA_NOPE + MLA_V)), full),
            pl.BlockSpec((tm, LANES), row),
            pl.BlockSpec((tm, LANES), row),
        ],
        out_specs=[
            pl.BlockSpec((tm, h * MLA_QK_PAD), row),
            pl.BlockSpec((tm, h * MLA_QK_PAD), row),
            pl.BlockSpec((tm, h * MLA_V), row),
        ],
        out_shape=[
            jax.ShapeDtypeStruct((t, h * MLA_QK_PAD), BF16),
            jax.ShapeDtypeStruct((t, h * MLA_QK_PAD), BF16),
            jax.ShapeDtypeStruct((t, h * MLA_V), BF16),
        ],
        compiler_params=_cparams("parallel"),
        name="mla_proj",
    )(x2d, wdown, q_norm.reshape(1, -1), wuq, kv_norm.reshape(1, -1), w_ukv.astype(BF16),
      c_tab, s_tab)


def _diff_proj_kernel(x_ref, w_ref, c_ref, s_ref, q_ref, k_ref, v_ref, *, scale):
    rope = functools.partial(_rotate_half_in_lanes, half=DIFF_ROT_DIM // 2, group=DIFF_HEAD_DIM,
                             c_tab=c_ref[...], s_tab=s_ref[...])
    qkv = _dot(x_ref[...].astype(BF16), w_ref[...])
    for j in range(D_MODEL // LANES):
        lo, hi = j * LANES, (j + 1) * LANES
        q_ref[:, lo:hi] = (rope(qkv[:, lo:hi]) * scale).astype(BF16)
        k_ref[:, lo:hi] = rope(qkv[:, D_MODEL + lo:D_MODEL + hi]).astype(BF16)
    v_ref[...] = qkv[:, 2 * D_MODEL:].astype(BF16)


def _diff_proj(x2d, c_tab, s_tab, w_qkv, tm):
    t = x2d.shape[0]
    row = lambda i: (i, 0)
    full = lambda i: (0, 0)
    return pl.pallas_call(
        functools.partial(_diff_proj_kernel, scale=LOG2_E * DIFF_HEAD_DIM ** -0.5),
        grid=(t // tm,),
        in_specs=[
            pl.BlockSpec((tm, D_MODEL), row),
            pl.BlockSpec((D_MODEL, 3 * D_MODEL), full),
            pl.BlockSpec((tm, LANES), row),
            pl.BlockSpec((tm, LANES), row),
        ],
        out_specs=[pl.BlockSpec((tm, D_MODEL), row)] * 3,
        out_shape=[jax.ShapeDtypeStruct((t, D_MODEL), BF16)] * 3,
        compiler_params=_cparams("parallel"),
        name="diff_proj",
    )(x2d, w_qkv.astype(BF16), c_tab, s_tab)


def _transpose_v(v_ref, vt_ref, tk):
    dv = v_ref.shape[1]
    for j in range(vt_ref.shape[0]):
        vt_ref[j, :dv, :] = v_ref[j * tk:(j + 1) * tk, :].astype(F32).T.astype(vt_ref.dtype)
        vt_ref[j, dv:, :] = jnp.ones((SUM_ROWS, tk), vt_ref.dtype)


def _flash_head(q_block, k_ref, vt_ref, s_ref, tri_ref, m_ref, acc_ref, finish, n_q, tq):
    rows = tq
    n_buf = s_ref.shape[0]
    dv = acc_ref.shape[1] - SUM_ROWS
    half = tq // 2
    tri_ref[...] = jnp.where(lax.broadcasted_iota(jnp.int32, (half, half), 0)
                             <= lax.broadcasted_iota(jnp.int32, (half, half), 1), 0.0, NEG_BIG)
    steps = [(i, j) for i in range(n_q) for j in range(i + 1)]

    def scores(t):
        i, j = steps[t]
        k = k_ref[j * tq:(j + 1) * tq, :]
        s_ref[t % n_buf] = _dot(k, q_block(i, j == 0))

    def update(par, cols, s, vt, first):
        m_blk = jnp.max(s, axis=0, keepdims=True)
        if first:
            m_new = m_blk
            acc_ref[par, :, cols] = _dot(vt, jnp.exp2(s - m_new).astype(BF16))
        else:
            m_prev = m_ref[par, :, cols]
            m_new = jnp.maximum(m_prev, m_blk)
            alpha = jnp.exp2(m_prev - m_new)
            acc_ref[par, :, cols] = (alpha * acc_ref[par, :, cols]
                                     + _dot(vt, jnp.exp2(s - m_new).astype(BF16)))
        m_ref[par, :, cols] = m_new

    def consume(t):
        i, j = steps[t]
        s_t = s_ref.at[t % n_buf]
        par = i % 2
        if j < i:
            update(par, slice(0, rows), s_t[...], vt_ref[j], first=j == 0)
            return
        for g in range(rows // half):
            cols = slice(g * half, (g + 1) * half)
            if g % 2 == 0:
                s = s_t[:half, cols] + tri_ref[...]
                vt = vt_ref[j, :, :half]
            else:
                s = jnp.concatenate([s_t[:half, cols], s_t[half:, cols] + tri_ref[...]], axis=0)
                vt = vt_ref[j]
            update(par, cols, s, vt, first=i == 0)
        finish(i, acc_ref[par, :dv, :], acc_ref[par, dv:dv + 1, :])

    return len(steps), n_buf - 1, scores, consume


def _run_interleaved(programs):
    n_steps, ahead = programs[0][:2]
    for _, _, scores, _ in programs:
        for t in range(min(ahead, n_steps)):
            scores(t)
    for t in range(n_steps):
        for _, _, scores, consume in programs:
            if t + ahead < n_steps:
                scores(t + ahead)
            consume(t)


def _mla_flash_kernel(q_ref, k_ref, v_ref, o_ref, qt_ref, vt_ref, s_ref, tri_ref, m_ref, acc_ref,
                      *, n_q, tq):
    _transpose_v(v_ref, vt_ref, tq)

    def q_block(i, first):
        if first:
            qt_ref[i % 2] = q_ref[i * tq:(i + 1) * tq, :].astype(F32).T.astype(BF16)
        return qt_ref[i % 2]

    def finish(i, acc, l):
        o_ref[i * tq:(i + 1) * tq, :] = (acc / l).T.astype(o_ref.dtype)

    _run_interleaved([_flash_head(q_block, k_ref, vt_ref, s_ref, tri_ref, m_ref, acc_ref,
                                  finish, n_q, tq)])


def _diff_flash_kernel(q_ref, k_ref, v_ref, lq1_ref, lk1_ref, lq2_ref, lk2_ref, subln_ref,
                       o_ref, vt_ref, *scratch, n_q, tq, lambda_init):
    _transpose_v(v_ref, vt_ref, tq)
    lam = (jnp.exp(jnp.sum(lq1_ref[...] * lk1_ref[...]))
           - jnp.exp(jnp.sum(lq2_ref[...] * lk2_ref[...])) + lambda_init)
    subln = subln_ref[...]
    per_map = len(scratch) // 2
    maps = [scratch[c * per_map:(c + 1) * per_map] for c in range(2)]
    acc1_ref = maps[0][-1]
    dv = v_ref.shape[1]

    programs = []
    for c, (qt_ref, s_ref, tri_ref, m_ref, acc_ref) in enumerate(maps):
        def q_block(i, first, c=c, qt_ref=qt_ref):
            if first:
                q = q_ref[i * tq:(i + 1) * tq, :]
                lane = lax.broadcasted_iota(jnp.int32, q.shape, 1)
                keep = (lane < DIFF_HEAD_DIM) if c == 0 else (lane >= DIFF_HEAD_DIM)
                qt_ref[i % 2] = jnp.where(keep, q, jnp.zeros_like(q)).astype(F32).T.astype(BF16)
            return qt_ref[i % 2]

        def finish(i, acc, l, c=c):
            if c == 0:
                return
            par = i % 2
            o1_t = acc1_ref[par, :dv, :] / acc1_ref[par, dv:dv + 1, :]
            o_t = o1_t - lam * (acc / l)
            o = _rms(o_t.T, subln) * (1.0 - lambda_init)
            o_ref[i * tq:(i + 1) * tq, :] = o.astype(o_ref.dtype)

        programs.append(_flash_head(q_block, k_ref, vt_ref, s_ref, tri_ref, m_ref, acc_ref,
                                    finish, n_q, tq))
    _run_interleaved(programs)


def _flash_scratch(seq, tq, dv, n_buf):
    return [pltpu.VMEM((seq // tq, dv + SUM_ROWS, tq), BF16),
            pltpu.VMEM((n_buf, tq, tq), F32),
            pltpu.VMEM((tq // 2, tq // 2), F32),
            pltpu.VMEM((2, 1, tq), F32),
            pltpu.VMEM((2, dv + SUM_ROWS, tq), F32)]


def _flash_specs(seq, dk, dv):
    head = lambda width: pl.BlockSpec((seq, width), lambda b, h: (b, h))
    return head(dk), head(dk), head(dv), head(dv)


def _mla_flash(q, k, v, batch, seq, tq):
    q_spec, k_spec, v_spec, o_spec = _flash_specs(seq, MLA_QK_PAD, MLA_V)
    return pl.pallas_call(
        functools.partial(_mla_flash_kernel, n_q=seq // tq, tq=tq),
        grid=(batch, MLA_HEADS),
        in_specs=[q_spec, k_spec, v_spec],
        out_specs=o_spec,
        out_shape=jax.ShapeDtypeStruct((batch * seq, MLA_HEADS * MLA_V), BF16),
        scratch_shapes=[pltpu.VMEM((2, MLA_QK_PAD, tq), BF16)]
        + _flash_scratch(seq, tq, MLA_V, MLA_SCORE_BUFFERS),
        compiler_params=_cparams("parallel", "parallel"),
        name="mla_flash",
    )(q, k, v)


def _diff_flash(q, k, v, lq1, lk1, lq2, lk2, subln, lambda_init, batch, seq, tq):
    dv = 2 * DIFF_HEAD_DIM
    q_spec, k_spec, v_spec, o_spec = _flash_specs(seq, dv, dv)
    vec = lambda n: pl.BlockSpec((1, n), lambda b, h: (0, 0))
    return pl.pallas_call(
        functools.partial(_diff_flash_kernel, n_q=seq // tq, tq=tq, lambda_init=lambda_init),
        grid=(batch, DIFF_HEADS),
        in_specs=[q_spec, k_spec, v_spec, vec(DIFF_HEAD_DIM), vec(DIFF_HEAD_DIM),
                  vec(DIFF_HEAD_DIM), vec(DIFF_HEAD_DIM), vec(dv)],
        out_specs=o_spec,
        out_shape=jax.ShapeDtypeStruct((batch * seq, DIFF_HEADS * dv), BF16),
        scratch_shapes=_flash_scratch(seq, tq, dv, DIFF_SCORE_BUFFERS)[:1]
        + 2 * ([pltpu.VMEM((2, dv, tq), BF16)]
               + _flash_scratch(seq, tq, dv, DIFF_SCORE_BUFFERS)[1:]),
        compiler_params=_cparams("parallel", "parallel"),
        name="diff_flash",
    )(q, k, v, lq1.reshape(1, -1), lk1.reshape(1, -1), lq2.reshape(1, -1), lk2.reshape(1, -1),
      subln.reshape(1, -1))


def _post_attn_kernel(o_ref, x_ref, wo_ref, g1_ref, b1_ref, wu_ref, wd_ref, g2_ref, b2_ref,
                      y_ref, x1b_ref, acc_ref, *, n_chain, tf):
    rc = acc_ref.shape[0] // n_chain
    n_ff = wu_ref.shape[1] // tf
    hidden = {}

    def rows(c):
        return pl.ds(c * rc, rc)

    def attn_out(c):
        x1 = _layernorm(DEEPNORM_ALPHA * x_ref[rows(c), :] + _dot(o_ref[rows(c), :], wo_ref[...]),
                        g1_ref[...], b1_ref[...])
        x1b_ref[rows(c), :] = x1.astype(BF16)
        acc_ref[rows(c), :] = DEEPNORM_ALPHA * x1

    def up(f):
        def stage(c):
            h = _dot(x1b_ref[rows(c), :], wu_ref[:, f * tf:(f + 1) * tf])
            hidden[c] = jnp.square(jnp.maximum(h, 0.0)).astype(BF16)
        return stage

    def down(f):
        def stage(c):
            total = acc_ref[rows(c), :] + _dot(hidden[c], wd_ref[f * tf:(f + 1) * tf, :])
            if f == n_ff - 1:
                y_ref[rows(c), :] = _layernorm(total, g2_ref[...], b2_ref[...])
            else:
                acc_ref[rows(c), :] = total
        return stage

    stages = [attn_out]
    for f in range(n_ff):
        stages += [up(f), down(f)]
    _emit_staggered(stages, n_chain)


def _post_attn(o, x2d, w_o, g1, b1, w_up_all, w_down_all, layer, g2, b2, tm, tf, n_chain):
    t = x2d.shape[0]
    row = lambda i: (i, 0)
    full = lambda i: (0, 0)
    once = pl.Buffered(1)
    vec = pl.BlockSpec((1, D_MODEL), full)
    return pl.pallas_call(
        functools.partial(_post_attn_kernel, n_chain=n_chain, tf=tf),
        grid=(t // tm,),
        in_specs=[
            pl.BlockSpec((tm, D_MODEL), row),
            pl.BlockSpec((tm, D_MODEL), row),
            pl.BlockSpec((D_MODEL, D_MODEL), full, pipeline_mode=once),
            vec, vec,
            pl.BlockSpec((None, D_MODEL, D_FF), lambda i: (layer, 0, 0), pipeline_mode=once),
            pl.BlockSpec((None, D_FF, D_MODEL), lambda i: (layer, 0, 0), pipeline_mode=once),
            vec, vec,
        ],
        out_specs=pl.BlockSpec((tm, D_MODEL), row),
        out_shape=jax.ShapeDtypeStruct((t, D_MODEL), F32),
        scratch_shapes=[pltpu.VMEM((tm, D_MODEL), BF16), pltpu.VMEM((tm, D_MODEL), F32)],
        compiler_params=pltpu.CompilerParams(dimension_semantics=("parallel",),
                                             vmem_limit_bytes=POST_VMEM_LIMIT_BYTES),
        name="post_attn",
    )(o, x2d, w_o.astype(BF16), g1.reshape(1, -1), b1.reshape(1, -1), w_up_all, w_down_all,
      g2.reshape(1, -1), b2.reshape(1, -1))


def kernel(x, positions, ln_mix_g, ln_mix_b, mla_w_dq, mla_q_norm, mla_w_uq, mla_w_dkv, mla_kv_norm, mla_w_ukv, mla_w_o, diff_w_qkv, diff_lambda_q1, diff_lambda_k1, diff_lambda_q2, diff_lambda_k2, diff_subln, diff_w_o, ln_ffn_g, ln_ffn_b, ffn_w_up, ffn_w_down):
    batch, seq, d_model = x.shape
    assert d_model == D_MODEL and seq % ATTN_TILE == 0
    assert (batch * seq) % POST_ROWS == 0 and (batch * seq) % PROJ_ROWS == 0
    x2d = x.reshape(batch * seq, D_MODEL)
    pos = positions.reshape(-1)

    mla_c, mla_s = _rope_tables(pos, MLA_ROPE // 2, LANES, 1, MLA_ROPE_THETA)
    diff_c, diff_s = _rope_tables(pos, DIFF_ROT_DIM // 2, DIFF_HEAD_DIM, 2, DIFF_ROPE_THETA)

    post_tiles = (POST_ROWS, FF_CHUNK, POST_CHAINS)
    w_up_all = ffn_w_up.astype(BF16)
    w_down_all = ffn_w_down.astype(BF16)
    q, k, v = _mla_proj(x2d, mla_c, mla_s, mla_w_dq[0], mla_q_norm[0], mla_w_uq[0],
                        mla_w_dkv[0], mla_kv_norm[0], mla_w_ukv[0], PROJ_ROWS)
    o = _mla_flash(q, k, v, batch, seq, ATTN_TILE)
    x2d = _post_attn(o, x2d, mla_w_o[0], ln_mix_g[0], ln_mix_b[0], w_up_all, w_down_all, 0,
                     ln_ffn_g[0], ln_ffn_b[0], *post_tiles)

    lambda_init = 0.8 - 0.6 * math.exp(-0.3 * 1)
    q, k, v = _diff_proj(x2d, diff_c, diff_s, diff_w_qkv[0], PROJ_ROWS)
    o = _diff_flash(q, k, v, diff_lambda_q1[0], diff_lambda_k1[0], diff_lambda_q2[0],
                    diff_lambda_k2[0], diff_subln[0], lambda_init, batch, seq, ATTN_TILE)
    x2d = _post_attn(o, x2d, diff_w_o[0], ln_mix_g[1], ln_mix_b[1], w_up_all, w_down_all, 1,
                     ln_ffn_g[1], ln_ffn_b[1], *post_tiles)
    return x2d.reshape(batch, seq, D_MODEL)
```

```python
import functools
import math

import jax
import jax.numpy as jnp
from jax import lax
from jax.experimental import pallas as pl
from jax.experimental.pallas import tpu as pltpu

D_MODEL = 1024
DEPTH = 2

MLA_NOPE = 128
MLA_ROPE = 64
MLA_V = 128
MLA_HEADS = 8
MLA_Q_LORA = 384
MLA_KV_LORA = 256
MLA_ROPE_THETA = 10000.0
MLA_QK_PAD = 256

DIFF_HEAD_DIM = 64
DIFF_HEADS = 8
DIFF_ROT_DIM = 16
DIFF_ROPE_THETA = 500000.0

D_FF = 4 * D_MODEL
DEEPNORM_ALPHA = (2 * DEPTH) ** 0.25
LN_EPS = 1e-5
RMS_EPS = 1e-6

LANES = 128
SUM_ROWS = 16
NEG_BIG = -1e30
LOG2_E = math.log2(math.e)
VMEM_LIMIT_BYTES = 48 * 1024 * 1024
POST_VMEM_LIMIT_BYTES = 56 * 1024 * 1024

F32 = jnp.float32
BF16 = jnp.bfloat16

TABLE_ROWS = 512
PROJ_ROWS = 1024
MLA_PROJ_CHAINS = 4
ATTN_TILE = 512
MLA_SCORE_BUFFERS = 3
DIFF_SCORE_BUFFERS = 2
POST_ROWS = 1024
POST_CHAINS = 4
FF_CHUNK = 1024


def _cparams(*sem):
    return pltpu.CompilerParams(dimension_semantics=sem, vmem_limit_bytes=VMEM_LIMIT_BYTES)


def _dot(a, b):
    return jnp.dot(a, b, preferred_element_type=F32)


def _rms(x, g):
    return x * lax.rsqrt(jnp.mean(jnp.square(x), axis=-1, keepdims=True) + RMS_EPS) * g


def _layernorm(y, g, b):
    mu = jnp.mean(y, axis=-1, keepdims=True)
    d = y - mu
    var = jnp.mean(jnp.square(d), axis=-1, keepdims=True)
    return d * lax.rsqrt(var + LN_EPS) * g + b


def _rope_table_kernel(pos_ref, invf_ref, c_ref, s_ref, *, n_tok, half, dup, group):
    rows = pos_ref.shape[0]
    width = LANES // n_tok
    lane = lax.broadcasted_iota(jnp.int32, (rows, LANES), 1)
    pos = pos_ref[...]
    pos_dense = jnp.broadcast_to(pos[:, n_tok - 1:n_tok], (rows, LANES))
    for a in reversed(range(n_tok - 1)):
        pos_dense = jnp.where(lane < (a + 1) * width, pos[:, a:a + 1], pos_dense)
    ang = pos_dense.astype(F32) * invf_ref[...]
    cos_d = jnp.cos(ang)
    sin_d = jnp.sin(ang)
    neg_sin_d = -sin_d

    def spread(first_d, second_d, a, fill):
        rolled = {}

        def at(x_d, key, shift):
            if (key, shift) not in rolled:
                rolled[key, shift] = x_d if shift == 0 else pltpu.roll(x_d, shift, axis=1)
            return rolled[key, shift]

        out = jnp.full((rows, LANES), fill, F32)
        for g0 in range(0, LANES, group):
            for h, x_d in enumerate((first_d, second_d)):
                src = a * width + (h * half if dup == 2 else 0)
                dst = g0 + h * half
                val = at(x_d, h, (dst - src) % LANES)
                out = jnp.where((lane >= dst) & (lane < dst + half), val, out)
        return out

    for a in range(n_tok):
        c_ref[a] = spread(cos_d, cos_d, a, 1.0)
        s_ref[a] = spread(neg_sin_d, sin_d, a, 0.0)


def _rope_tables(positions, half, group, dup, theta, rows_per_step=TABLE_ROWS):
    t = positions.size
    n_tok = LANES // (half * dup)
    rows = t // n_tok
    assert rows % rows_per_step == 0
    inv_freq = theta ** (-jnp.arange(0, 2 * half, 2, dtype=F32) / (2 * half))
    invf = jnp.tile(inv_freq, LANES // half).reshape(1, LANES)
    pos_t = positions.reshape(n_tok, rows).T
    out_spec = pl.BlockSpec((n_tok, rows_per_step, LANES), lambda i: (0, i, 0))
    c, s = pl.pallas_call(
        functools.partial(_rope_table_kernel, n_tok=n_tok, half=half, dup=dup, group=group),
        grid=(rows // rows_per_step,),
        in_specs=[pl.BlockSpec((rows_per_step, n_tok), lambda i: (i, 0)),
                  pl.BlockSpec((1, LANES), lambda i: (0, 0))],
        out_specs=[out_spec, out_spec],
        out_shape=[jax.ShapeDtypeStruct((n_tok, rows, LANES), F32)] * 2,
        compiler_params=_cparams("parallel"),
        name="rope_table",
    )(pos_t, invf)
    return c.reshape(t, LANES), s.reshape(t, LANES)


def _rotate_half_in_lanes(x, half, group, c_tab, s_tab):
    lane = lax.broadcasted_iota(jnp.int32, x.shape, 1)
    from_above = pltpu.roll(x, LANES - half, axis=1)
    from_below = pltpu.roll(x, half, axis=1)
    partner = jnp.where(lane % group < half, from_above, from_below)
    return x * c_tab + partner * s_tab


def _emit_staggered(stages, n_chain):
    for slot in range(n_chain + len(stages) - 1):
        for depth, stage in enumerate(stages):
            if 0 <= slot - depth < n_chain:
                stage(slot - depth)


def _mla_proj_kernel(x_ref, wdown_ref, qn_ref, wuq_ref, kvn_ref, wukv_ref,
                     c_ref, s_ref, q_ref, k_ref, v_ref, *, scale, n_chain):
    rc = x_ref.shape[0] // n_chain
    cq, c_kv, k_pe, q, kv = {}, {}, {}, {}, {}

    def rows(c):
        return pl.ds(c * rc, rc)

    def rope_of(c):
        return functools.partial(_rotate_half_in_lanes, half=MLA_ROPE // 2, group=LANES,
                                 c_tab=c_ref[rows(c), :], s_tab=s_ref[rows(c), :])

    def down_proj(c):
        down = _dot(x_ref[rows(c), :].astype(BF16), wdown_ref[...])
        cq[c] = _rms(down[:, :MLA_Q_LORA], qn_ref[...]).astype(BF16)
        ckv = down[:, MLA_Q_LORA:]
        c_kv[c] = _rms(ckv[:, :MLA_KV_LORA], kvn_ref[...]).astype(BF16)
        k_pe[c] = rope_of(c)(ckv[:, MLA_KV_LORA:]).astype(BF16)

    def up_proj(c):
        q[c] = _dot(cq[c], wuq_ref[...])
        kv[c] = _dot(c_kv[c], wukv_ref[...])

    def write_heads(c):
        rope = rope_of(c)
        for h in range(MLA_HEADS):
            lo, mid, hi = h * MLA_QK_PAD, h * MLA_QK_PAD + LANES, (h + 1) * MLA_QK_PAD
            q_ref[rows(c), lo:mid] = (q[c][:, lo:mid] * scale).astype(BF16)
            q_ref[rows(c), mid:hi] = (rope(q[c][:, mid:hi]) * scale).astype(BF16)
            k_ref[rows(c), lo:mid] = kv[c][:, lo:mid].astype(BF16)
            k_ref[rows(c), mid:hi] = k_pe[c]
            v_ref[rows(c), h * MLA_V:(h + 1) * MLA_V] = kv[c][:, mid:hi].astype(BF16)

    _emit_staggered((down_proj, up_proj, write_heads), n_chain)


def _mla_proj(x2d, c_tab, s_tab, w_dq, q_norm, w_uq, w_dkv, kv_norm, w_ukv, tm):
    t = x2d.shape[0]
    h = MLA_HEADS
    pad = MLA_QK_PAD - MLA_NOPE - MLA_ROPE
    wuq = jnp.pad(w_uq.reshape(MLA_Q_LORA, h, MLA_NOPE + MLA_ROPE), ((0, 0), (0, 0), (0, pad)))
    wuq = wuq.reshape(MLA_Q_LORA, h * MLA_QK_PAD).astype(BF16)
    wdown = jnp.concatenate([w_dq, jnp.pad(w_dkv, ((0, 0), (0, pad)))], axis=1).astype(BF16)
    row = lambda i: (i, 0)
    full = lambda i: (0, 0)
    return pl.pallas_call(
        functools.partial(_mla_proj_kernel, scale=LOG2_E * (MLA_NOPE + MLA_ROPE) ** -0.5,
                          n_chain=MLA_PROJ_CHAINS),
        grid=(t // tm,),
        in_specs=[
            pl.BlockSpec((tm, D_MODEL), row),
            pl.BlockSpec((D_MODEL, MLA_Q_LORA + MLA_KV_LORA + LANES), full),
            pl.BlockSpec((1, MLA_Q_LORA), full),
            pl.BlockSpec((MLA_Q_LORA, h * MLA_QK_PAD), full),
            pl.BlockSpec((1, MLA_KV_LORA), full),
            pl.BlockSpec((MLA_KV_LORA, h * (MLA_NOPE + MLA_V)), full),
            pl.BlockSpec((tm, LANES), row),
            pl.BlockSpec((tm, LANES), row),
        ],
        out_specs=[
            pl.BlockSpec((tm, h * MLA_QK_PAD), row),
            pl.BlockSpec((tm, h * MLA_QK_PAD), row),
            pl.BlockSpec((tm, h * MLA_V), row),
        ],
        out_shape=[
            jax.ShapeDtypeStruct((t, h * MLA_QK_PAD), BF16),
            jax.ShapeDtypeStruct((t, h * MLA_QK_PAD), BF16),
            jax.ShapeDtypeStruct((t, h * MLA_V), BF16),
        ],
        compiler_params=_cparams("parallel"),
        name="mla_proj",
    )(x2d, wdown, q_norm.reshape(1, -1), wuq, kv_norm.reshape(1, -1), w_ukv.astype(BF16),
      c_tab, s_tab)


def _diff_proj_kernel(x_ref, w_ref, c_ref, s_ref, q_ref, k_ref, v_ref, *, scale):
    rope = functools.partial(_rotate_half_in_lanes, half=DIFF_ROT_DIM // 2, group=DIFF_HEAD_DIM,
                             c_tab=c_ref[...], s_tab=s_ref[...])
    qkv = _dot(x_ref[...].astype(BF16), w_ref[...])
    for j in range(D_MODEL // LANES):
        lo, hi = j * LANES, (j + 1) * LANES
        q_ref[:, lo:hi] = (rope(qkv[:, lo:hi]) * scale).astype(BF16)
        k_ref[:, lo:hi] = rope(qkv[:, D_MODEL + lo:D_MODEL + hi]).astype(BF16)
    v_ref[...] = qkv[:, 2 * D_MODEL:].astype(BF16)


def _diff_proj(x2d, c_tab, s_tab, w_qkv, tm):
    t = x2d.shape[0]
    row = lambda i: (i, 0)
    full = lambda i: (0, 0)
    return pl.pallas_call(
        functools.partial(_diff_proj_kernel, scale=LOG2_E * DIFF_HEAD_DIM ** -0.5),
        grid=(t // tm,),
        in_specs=[
            pl.BlockSpec((tm, D_MODEL), row),
            pl.BlockSpec((D_MODEL, 3 * D_MODEL), full),
            pl.BlockSpec((tm, LANES), row),
            pl.BlockSpec((tm, LANES), row),
        ],
        out_specs=[pl.BlockSpec((tm, D_MODEL), row)] * 3,
        out_shape=[jax.ShapeDtypeStruct((t, D_MODEL), BF16)] * 3,
        compiler_params=_cparams("parallel"),
        name="diff_proj",
    )(x2d, w_qkv.astype(BF16), c_tab, s_tab)


def _transpose_v(v_ref, vt_ref, tk):
    dv = v_ref.shape[1]
    for j in range(vt_ref.shape[0]):
        vt_ref[j, :dv, :] = v_ref[j * tk:(j + 1) * tk, :].astype(F32).T.astype(vt_ref.dtype)
        vt_ref[j, dv:, :] = jnp.ones((SUM_ROWS, tk), vt_ref.dtype)


def _flash_head(q_block, k_ref, vt_ref, s_ref, tri_ref, m_ref, acc_ref, finish, n_q, tq,
                split_columns):
    rows = tq
    n_buf = s_ref.shape[0]
    dv = acc_ref.shape[1] - SUM_ROWS
    half = tq // 2
    tri_ref[...] = jnp.where(lax.broadcasted_iota(jnp.int32, (half, half), 0)
                             <= lax.broadcasted_iota(jnp.int32, (half, half), 1), 0.0, NEG_BIG)
    steps = [(i, j) for i in range(n_q) for j in range(i + 1)]

    def scores(t):
        i, j = steps[t]
        k = k_ref[j * tq:(j + 1) * tq, :]
        s_ref[t % n_buf] = _dot(k, q_block(i, j == 0))

    def update(par, cols, s, vt, first):
        m_blk = jnp.max(s, axis=0, keepdims=True)
        if first:
            m_new = m_blk
            acc_ref[par, :, cols] = _dot(vt, jnp.exp2(s - m_new).astype(BF16))
        else:
            m_prev = m_ref[par, :, cols]
            m_new = jnp.maximum(m_prev, m_blk)
            alpha = jnp.exp2(m_prev - m_new)
            acc_ref[par, :, cols] = (alpha * acc_ref[par, :, cols]
                                     + _dot(vt, jnp.exp2(s - m_new).astype(BF16)))
        m_ref[par, :, cols] = m_new

    def consume(t):
        i, j = steps[t]
        s_t = s_ref.at[t % n_buf]
        par = i % 2
        if j < i:
            width = half if split_columns else rows
            for c0 in range(0, rows, width):
                cols = slice(c0, c0 + width)
                update(par, cols, s_t[:, cols], vt_ref[j], first=j == 0)
            return
        for g in range(rows // half):
            cols = slice(g * half, (g + 1) * half)
            if g % 2 == 0:
                s = s_t[:half, cols] + tri_ref[...]
                vt = vt_ref[j, :, :half]
            else:
                s = jnp.concatenate([s_t[:half, cols], s_t[half:, cols] + tri_ref[...]], axis=0)
                vt = vt_ref[j]
            update(par, cols, s, vt, first=i == 0)
        finish(i, acc_ref[par, :dv, :], acc_ref[par, dv:dv + 1, :])

    return len(steps), n_buf - 1, scores, consume


def _run_interleaved(programs):
    n_steps, ahead = programs[0][:2]
    for _, _, scores, _ in programs:
        for t in range(min(ahead, n_steps)):
            scores(t)
    for t in range(n_steps):
        for _, _, scores, consume in programs:
            if t + ahead < n_steps:
                scores(t + ahead)
            consume(t)


def _mla_flash_kernel(q_ref, k_ref, v_ref, o_ref, qt_ref, vt_ref, s_ref, tri_ref, m_ref, acc_ref,
                      *, n_q, tq):
    _transpose_v(v_ref, vt_ref, tq)

    def q_block(i, first):
        if first:
            qt_ref[i % 2] = q_ref[i * tq:(i + 1) * tq, :].astype(F32).T.astype(BF16)
        return qt_ref[i % 2]

    def finish(i, acc, l):
        o_ref[i * tq:(i + 1) * tq, :] = (acc / l).T.astype(o_ref.dtype)

    _run_interleaved([_flash_head(q_block, k_ref, vt_ref, s_ref, tri_ref, m_ref, acc_ref,
                                  finish, n_q, tq, split_columns=True)])


def _diff_flash_kernel(q_ref, k_ref, v_ref, lq1_ref, lk1_ref, lq2_ref, lk2_ref, subln_ref,
                       o_ref, vt_ref, *scratch, n_q, tq, lambda_init):
    _transpose_v(v_ref, vt_ref, tq)
    lam = (jnp.exp(jnp.sum(lq1_ref[...] * lk1_ref[...]))
           - jnp.exp(jnp.sum(lq2_ref[...] * lk2_ref[...])) + lambda_init)
    subln = subln_ref[...]
    per_map = len(scratch) // 2
    maps = [scratch[c * per_map:(c + 1) * per_map] for c in range(2)]
    acc1_ref = maps[0][-1]
    dv = v_ref.shape[1]

    programs = []
    for c, (qt_ref, s_ref, tri_ref, m_ref, acc_ref) in enumerate(maps):
        def q_block(i, first, c=c, qt_ref=qt_ref):
            if first:
                q = q_ref[i * tq:(i + 1) * tq, :]
                lane = lax.broadcasted_iota(jnp.int32, q.shape, 1)
                keep = (lane < DIFF_HEAD_DIM) if c == 0 else (lane >= DIFF_HEAD_DIM)
                qt_ref[i % 2] = jnp.where(keep, q, jnp.zeros_like(q)).astype(F32).T.astype(BF16)
            return qt_ref[i % 2]

        def finish(i, acc, l, c=c):
            if c == 0:
                return
            par = i % 2
            o1_t = acc1_ref[par, :dv, :] / acc1_ref[par, dv:dv + 1, :]
            o_t = o1_t - lam * (acc / l)
            o = _rms(o_t.T, subln) * (1.0 - lambda_init)
            o_ref[i * tq:(i + 1) * tq, :] = o.astype(o_ref.dtype)

        programs.append(_flash_head(q_block, k_ref, vt_ref, s_ref, tri_ref, m_ref, acc_ref,
                                    finish, n_q, tq, split_columns=False))
    _run_interleaved(programs)


def _flash_scratch(seq, tq, dv, n_buf):
    return [pltpu.VMEM((seq // tq, dv + SUM_ROWS, tq), BF16),
            pltpu.VMEM((n_buf, tq, tq), F32),
            pltpu.VMEM((tq // 2, tq // 2), F32),
            pltpu.VMEM((2, 1, tq), F32),
            pltpu.VMEM((2, dv + SUM_ROWS, tq), F32)]


def _flash_specs(seq, dk, dv):
    head = lambda width: pl.BlockSpec((seq, width), lambda b, h: (b, h))
    return head(dk), head(dk), head(dv), head(dv)


def _mla_flash(q, k, v, batch, seq, tq):
    q_spec, k_spec, v_spec, o_spec = _flash_specs(seq, MLA_QK_PAD, MLA_V)
    return pl.pallas_call(
        functools.partial(_mla_flash_kernel, n_q=seq // tq, tq=tq),
        grid=(batch, MLA_HEADS),
        in_specs=[q_spec, k_spec, v_spec],
        out_specs=o_spec,
        out_shape=jax.ShapeDtypeStruct((batch * seq, MLA_HEADS * MLA_V), BF16),
        scratch_shapes=[pltpu.VMEM((2, MLA_QK_PAD, tq), BF16)]
        + _flash_scratch(seq, tq, MLA_V, MLA_SCORE_BUFFERS),
        compiler_params=_cparams("parallel", "parallel"),
        name="mla_flash",
    )(q, k, v)


def _diff_flash(q, k, v, lq1, lk1, lq2, lk2, subln, lambda_init, batch, seq, tq):
    dv = 2 * DIFF_HEAD_DIM
    q_spec, k_spec, v_spec, o_spec = _flash_specs(seq, dv, dv)
    vec = lambda n: pl.BlockSpec((1, n), lambda b, h: (0, 0))
    return pl.pallas_call(
        functools.partial(_diff_flash_kernel, n_q=seq // tq, tq=tq, lambda_init=lambda_init),
        grid=(batch, DIFF_HEADS),
        in_specs=[q_spec, k_spec, v_spec, vec(DIFF_HEAD_DIM), vec(DIFF_HEAD_DIM),
                  vec(DIFF_HEAD_DIM), vec(DIFF_HEAD_DIM), vec(dv)],
        out_specs=o_spec,
        out_shape=jax.ShapeDtypeStruct((batch * seq, DIFF_HEADS * dv), BF16),
        scratch_shapes=_flash_scratch(seq, tq, dv, DIFF_SCORE_BUFFERS)[:1]
        + 2 * ([pltpu.VMEM((2, dv, tq), BF16)]
               + _flash_scratch(seq, tq, dv, DIFF_SCORE_BUFFERS)[1:]),
        compiler_params=_cparams("parallel", "parallel"),
        name="diff_flash",
    )(q, k, v, lq1.reshape(1, -1), lk1.reshape(1, -1), lq2.reshape(1, -1), lk2.reshape(1, -1),
      subln.reshape(1, -1))


def _post_attn_kernel(o_ref, x_ref, wo_ref, g1_ref, b1_ref, wu_ref, wd_ref, g2_ref, b2_ref,
                      y_ref, x1b_ref, acc_ref, *, n_chain, tf):
    rc = acc_ref.shape[0] // n_chain
    n_ff = wu_ref.shape[1] // tf
    hidden = {}

    def rows(c):
        return pl.ds(c * rc, rc)

    def attn_out(c):
        x1 = _layernorm(DEEPNORM_ALPHA * x_ref[rows(c), :] + _dot(o_ref[rows(c), :], wo_ref[...]),
                        g1_ref[...], b1_ref[...])
        x1b_ref[rows(c), :] = x1.astype(BF16)
        acc_ref[rows(c), :] = DEEPNORM_ALPHA * x1

    def up(f):
        def stage(c):
            h = _dot(x1b_ref[rows(c), :], wu_ref[:, f * tf:(f + 1) * tf])
            hidden[c] = jnp.square(jnp.maximum(h, 0.0)).astype(BF16)
        return stage

    def down(f):
        def stage(c):
            total = acc_ref[rows(c), :] + _dot(hidden[c], wd_ref[f * tf:(f + 1) * tf, :])
            if f == n_ff - 1:
                y_ref[rows(c), :] = _layernorm(total, g2_ref[...], b2_ref[...])
            else:
                acc_ref[rows(c), :] = total
        return stage

    stages = [attn_out]
    for f in range(n_ff):
        stages += [up(f), down(f)]
    _emit_staggered(stages, n_chain)


def _post_attn(o, x2d, w_o, g1, b1, w_up_all, w_down_all, layer, g2, b2, tm, tf, n_chain):
    t = x2d.shape[0]
    row = lambda i: (i, 0)
    full = lambda i: (0, 0)
    once = pl.Buffered(1)
    vec = pl.BlockSpec((1, D_MODEL), full)
    return pl.pallas_call(
        functools.partial(_post_attn_kernel, n_chain=n_chain, tf=tf),
        grid=(t // tm,),
        in_specs=[
            pl.BlockSpec((tm, D_MODEL), row),
            pl.BlockSpec((tm, D_MODEL), row),
            pl.BlockSpec((D_MODEL, D_MODEL), full, pipeline_mode=once),
            vec, vec,
            pl.BlockSpec((None, D_MODEL, D_FF), lambda i: (layer, 0, 0), pipeline_mode=once),
            pl.BlockSpec((None, D_FF, D_MODEL), lambda i: (layer, 0, 0), pipeline_mode=once),
            vec, vec,
        ],
        out_specs=pl.BlockSpec((tm, D_MODEL), row),
        out_shape=jax.ShapeDtypeStruct((t, D_MODEL), F32),
        scratch_shapes=[pltpu.VMEM((tm, D_MODEL), BF16), pltpu.VMEM((tm, D_MODEL), F32)],
        compiler_params=pltpu.CompilerParams(dimension_semantics=("parallel",),
                                             vmem_limit_bytes=POST_VMEM_LIMIT_BYTES),
        name="post_attn",
    )(o, x2d, w_o.astype(BF16), g1.reshape(1, -1), b1.reshape(1, -1), w_up_all, w_down_all,
      g2.reshape(1, -1), b2.reshape(1, -1))


def kernel(x, positions, ln_mix_g, ln_mix_b, mla_w_dq, mla_q_norm, mla_w_uq, mla_w_dkv, mla_kv_norm, mla_w_ukv, mla_w_o, diff_w_qkv, diff_lambda_q1, diff_lambda_k1, diff_lambda_q2, diff_lambda_k2, diff_subln, diff_w_o, ln_ffn_g, ln_ffn_b, ffn_w_up, ffn_w_down):
    batch, seq, d_model = x.shape
    assert d_model == D_MODEL and seq % ATTN_TILE == 0
    assert (batch * seq) % POST_ROWS == 0 and (batch * seq) % PROJ_ROWS == 0
    x2d = x.reshape(batch * seq, D_MODEL)
    pos = positions.reshape(-1)

    mla_c, mla_s = _rope_tables(pos, MLA_ROPE // 2, LANES, 1, MLA_ROPE_THETA)
    diff_c, diff_s = _rope_tables(pos, DIFF_ROT_DIM // 2, DIFF_HEAD_DIM, 2, DIFF_ROPE_THETA)

    post_tiles = (POST_ROWS, FF_CHUNK, POST_CHAINS)
    w_up_all = ffn_w_up.astype(BF16)
    w_down_all = ffn_w_down.astype(BF16)
    q, k, v = _mla_proj(x2d, mla_c, mla_s, mla_w_dq[0], mla_q_norm[0], mla_w_uq[0],
                        mla_w_dkv[0], mla_kv_norm[0], mla_w_ukv[0], PROJ_ROWS)
    o = _mla_flash(q, k, v, batch, seq, ATTN_TILE)
    x2d = _post_attn(o, x2d, mla_w_o[0], ln_mix_g[0], ln_mix_b[0], w_up_all, w_down_all, 0,
                     ln_ffn_g[0], ln_ffn_b[0], *post_tiles)

    lambda_init = 0.8 - 0.6 * math.exp(-0.3 * 1)
    q, k, v = _diff_proj(x2d, diff_c, diff_s, diff_w_qkv[0], PROJ_ROWS)
    o = _diff_flash(q, k, v, diff_lambda_q1[0], diff_lambda_k1[0], diff_lambda_q2[0],
                    diff_lambda_k2[0], diff_subln[0], lambda_init, batch, seq, ATTN_TILE)
    x2d = _post_attn(o, x2d, diff_w_o[0], ln_mix_g[1], ln_mix_b[1], w_up_all, w_down_all, 1,
                     ln_ffn_g[1], ln_ffn_b[1], *post_tiles)
    return x2d.reshape(batch, seq, D_MODEL)
```
